```python
import jax, jax.numpy as jnp
from jax import lax
import numpy as np

D_MODEL = 1024
BATCH = 32
SEQ = 2048
DEPTH = 2

HEAD_DIM = 64
N_MIX_HEADS = D_MODEL // HEAD_DIM
A_Q_HEADS = N_MIX_HEADS // 2
A_KV_HEADS = 2
B_Q_HEADS = N_MIX_HEADS - A_Q_HEADS
B_KV_HEADS = 2
MIX_WIDTH = (A_Q_HEADS + B_Q_HEADS) * HEAD_DIM
QA_W = A_Q_HEADS * HEAD_DIM
KVA_W = A_KV_HEADS * HEAD_DIM
QB_W = B_Q_HEADS * HEAD_DIM
KVB_W = B_KV_HEADS * HEAD_DIM
IN_WIDTH = QA_W + 2 * KVA_W + QB_W + 2 * KVB_W
SPLITS = [QA_W, QA_W + KVA_W, QA_W + 2 * KVA_W, QA_W + 2 * KVA_W + QB_W,
          QA_W + 2 * KVA_W + QB_W + KVB_W]

BLOCK = 128
WINDOW = 128
GRID_W = 64
ROPE_THETA = 10000.0
ROPE_FREQS = HEAD_DIM // 4

PEER_HEADS = 8
PEER_NKEYS = 128
PEER_EXPERTS = PEER_NKEYS * PEER_NKEYS
PEER_DKEY = 256
PEER_TOPK = 16
PEER_CHUNK = 128

DEEPNORM_ALPHA = (2.0 * DEPTH) ** 0.25
DEEPNORM_BETA = (8.0 * DEPTH) ** -0.25
LN_EPS = 1e-5
RMS_EPS = 1e-6
NEG_INF = -1e30

kernel_name = 'hymba_axial_window_peer_encoder'


def alibi_slopes(n):
    return (2.0 ** (-(np.arange(1, n + 1, dtype=np.float32) * 8.0 / n))).astype(np.float32)


def layer_norm(x, g, b):
    xf = x.astype(jnp.float32)
    mu = xf.mean(-1, keepdims=True)
    var = jnp.square(xf - mu).mean(-1, keepdims=True)
    return ((xf - mu) * lax.rsqrt(var + LN_EPS) * g + b).astype(x.dtype)


def rms_norm(x, g):
    xf = x.astype(jnp.float32)
    return (xf * lax.rsqrt(jnp.square(xf).mean(-1, keepdims=True) + RMS_EPS) * g).astype(x.dtype)


def axial_rope_tables(S):
    rows = S // GRID_W
    row = jnp.repeat(jnp.arange(rows), GRID_W)
    col = jnp.tile(jnp.arange(GRID_W), rows)
    pos = jnp.stack([row, col], axis=-1).astype(jnp.float32)
    inv_freq = ROPE_THETA ** (-jnp.arange(ROPE_FREQS, dtype=jnp.float32) / ROPE_FREQS)
    ang = pos[:, :, None] * inv_freq
    return jnp.cos(ang), jnp.sin(ang)


def apply_axial_rope(x, cos, sin):
    B, S, H, dh = x.shape
    xr = x.astype(jnp.float32).reshape(B, S, H, 2, 2, ROPE_FREQS)
    x1, x2 = xr[..., 0, :], xr[..., 1, :]
    c, s = cos[None, :, None], sin[None, :, None]
    out = jnp.stack([x1 * c - x2 * s, x1 * s + x2 * c], axis=-2)
    return out.reshape(B, S, H, dh).astype(x.dtype)


def global_attention(q, k, v):
    B, S, Hq, dh = q.shape
    Hkv = k.shape[2]
    G = Hq // Hkv
    nb = S // BLOCK
    scale = dh ** -0.5
    qb = q.reshape(B, nb, BLOCK, Hkv, G, dh).transpose(1, 0, 2, 3, 4, 5)

    def one_block(qblk):
        sc = jnp.einsum('bqkgd,bskd->bkgqs', qblk, k, preferred_element_type=jnp.float32) * scale
        p = jax.nn.softmax(sc, axis=-1).astype(v.dtype)
        return jnp.einsum('bkgqs,bskd->bqkgd', p, v)

    o = lax.map(one_block, qb)
    return o.transpose(1, 0, 2, 3, 4, 5).reshape(B, S, Hq, dh)


def window_sink_attention(q, k, v, sink, slopes):
    B, S, Hq, dh = q.shape
    Hkv = k.shape[2]
    G = Hq // Hkv
    nb = S // BLOCK
    span = BLOCK + 2 * WINDOW
    scale = dh ** -0.5
    qb = q.reshape(B, nb, BLOCK, Hkv, G, dh).transpose(1, 0, 2, 3, 4, 5)
    pad = ((0, 0), (WINDOW, WINDOW), (0, 0), (0, 0))
    kp = jnp.pad(k, pad)
    vp = jnp.pad(v, pad)
    slope = slopes.reshape(Hkv, G, 1, 1)
    sink_l = sink.astype(jnp.float32).reshape(Hkv, G, 1)
    q_off = jnp.arange(BLOCK)
    k_off = jnp.arange(span) - WINDOW

    def one_block(args):
        qblk, j = args
        start = j * BLOCK
        kb = lax.dynamic_slice_in_dim(kp, start, span, axis=1)
        vb = lax.dynamic_slice_in_dim(vp, start, span, axis=1)
        t = start + q_off
        s_pos = start + k_off
        dist = jnp.abs(t[:, None] - s_pos[None, :])
        valid = (dist <= WINDOW) & (s_pos >= 0)[None, :] & (s_pos < S)[None, :]
        sc = jnp.einsum('bqkgd,bskd->bkgqs', qblk, kb, preferred_element_type=jnp.float32) * scale
        sc = sc - slope * dist.astype(jnp.float32)
        sc = jnp.where(valid, sc, NEG_INF)
        m = jnp.maximum(sc.max(-1), sink_l)
        p = jnp.exp(sc - m[..., None])
        denom = p.sum(-1) + jnp.exp(sink_l - m)
        p = (p / denom[..., None]).astype(v.dtype)
        return jnp.einsum('bkgqs,bskd->bqkgd', p, vb)

    o = lax.map(one_block, (qb, jnp.arange(nb)))
    return o.transpose(1, 0, 2, 3, 4, 5).reshape(B, S, Hq, dh)


def peer_ffn(x, wq, sub_keys, u, v):
    B, S, D = x.shape
    xt = x.reshape(-1, PEER_CHUNK, D)
    half = PEER_DKEY // 2

    def chunk(xc):
        q = (xc @ wq).reshape(PEER_CHUNK, PEER_HEADS, 2, half)
        sc = jnp.einsum('chpd,hpnd->chpn', q, sub_keys, preferred_element_type=jnp.float32)
        s1, i1 = lax.top_k(sc[:, :, 0], PEER_TOPK)
        s2, i2 = lax.top_k(sc[:, :, 1], PEER_TOPK)
        cand = (s1[..., :, None] + s2[..., None, :]).reshape(PEER_CHUNK, PEER_HEADS, PEER_TOPK * PEER_TOPK)
        cidx = (i1[..., :, None] * PEER_NKEYS + i2[..., None, :]).reshape(PEER_CHUNK, PEER_HEADS, PEER_TOPK * PEER_TOPK)
        top_s, pos = lax.top_k(cand, PEER_TOPK)
        eidx = jnp.take_along_axis(cidx, pos, axis=-1)
        g = jax.nn.softmax(top_s, axis=-1)
        u_sel = jnp.take(u, eidx, axis=0)
        a = jax.nn.gelu(jnp.einsum('chkd,cd->chk', u_sel, xc, preferred_element_type=jnp.float32), approximate=False)
        w = (g * a).astype(x.dtype)
        v_sel = jnp.take(v, eidx, axis=0)
        return jnp.einsum('chk,chkd->cd', w, v_sel)

    return lax.map(chunk, xt).reshape(B, S, D)


def setup_inputs(seed: int = 0) -> dict:
    key = jax.random.key(seed)
    ks = jax.random.split(key, 24)
    L, D = DEPTH, D_MODEL
    beta = DEEPNORM_BETA

    def nrm(k, shape, std):
        return jax.random.normal(k, shape, jnp.float32) * std

    x = nrm(ks[0], (BATCH, SEQ, D), 1.0)
    ln_in_g = 1.0 + nrm(ks[1], (D,), 0.02)
    ln_in_b = nrm(ks[2], (D,), 0.02)
    col_scale = jnp.concatenate([
        jnp.ones((QA_W + KVA_W,), jnp.float32), jnp.full((KVA_W,), beta, jnp.float32),
        jnp.ones((QB_W + KVB_W,), jnp.float32), jnp.full((KVB_W,), beta, jnp.float32)])
    w_in = nrm(ks[3], (L, D, IN_WIDTH), D ** -0.5) * col_scale
    qn_g = 1.0 + nrm(ks[4], (L, HEAD_DIM), 0.02)
    kn_g = 1.0 + nrm(ks[5], (L, HEAD_DIM), 0.02)
    sink = nrm(ks[6], (L, B_Q_HEADS), 0.5)
    gn_a_g = 1.0 + nrm(ks[7], (L, QA_W), 0.02)
    gn_b_g = 1.0 + nrm(ks[8], (L, QB_W), 0.02)
    w_o = nrm(ks[9], (L, MIX_WIDTH, D), beta * MIX_WIDTH ** -0.5)
    ln1_g = 1.0 + nrm(ks[10], (L, D), 0.02)
    ln1_b = nrm(ks[11], (L, D), 0.02)
    peer_wq = nrm(ks[12], (L, D, PEER_HEADS * PEER_DKEY), D ** -0.5)
    peer_keys = nrm(ks[13], (L, PEER_HEADS, 2, PEER_NKEYS, PEER_DKEY // 2), (PEER_DKEY // 2) ** -0.5)
    peer_u = nrm(ks[14], (L, PEER_EXPERTS, D), D ** -0.5)
    peer_v = nrm(ks[15], (L, PEER_EXPERTS, D), beta * (PEER_HEADS * PEER_TOPK) ** -0.5)
    ln2_g = 1.0 + nrm(ks[16], (L, D), 0.02)
    ln2_b = nrm(ks[17], (L, D), 0.02)
    return {'x': x, 'ln_in_g': ln_in_g, 'ln_in_b': ln_in_b, 'w_in': w_in,
            'qn_g': qn_g, 'kn_g': kn_g, 'sink': sink, 'gn_a_g': gn_a_g, 'gn_b_g': gn_b_g,
            'w_o': w_o, 'ln1_g': ln1_g, 'ln1_b': ln1_b, 'peer_wq': peer_wq,
            'peer_keys': peer_keys, 'peer_u': peer_u, 'peer_v': peer_v,
            'ln2_g': ln2_g, 'ln2_b': ln2_b}


def reference(x, ln_in_g, ln_in_b, w_in, qn_g, kn_g, sink, gn_a_g, gn_b_g, w_o,
              ln1_g, ln1_b, peer_wq, peer_keys, peer_u, peer_v, ln2_g, ln2_b):
    B, S, D = x.shape
    cos, sin = axial_rope_tables(S)
    slopes = jnp.asarray(alibi_slopes(B_Q_HEADS))
    h = layer_norm(x, ln_in_g, ln_in_b)
    for l in range(DEPTH):
        proj = h @ w_in[l]
        qa, ka, va, qb, kb, vb = jnp.split(proj, SPLITS, axis=-1)
        qa = qa.reshape(B, S, A_Q_HEADS, HEAD_DIM)
        ka = ka.reshape(B, S, A_KV_HEADS, HEAD_DIM)
        va = va.reshape(B, S, A_KV_HEADS, HEAD_DIM)
        qb = qb.reshape(B, S, B_Q_HEADS, HEAD_DIM)
        kb = kb.reshape(B, S, B_KV_HEADS, HEAD_DIM)
        vb = vb.reshape(B, S, B_KV_HEADS, HEAD_DIM)
        qa = apply_axial_rope(rms_norm(qa, qn_g[l]), cos, sin)
        ka = apply_axial_rope(rms_norm(ka, kn_g[l]), cos, sin)
        oa = global_attention(qa, ka, va)
        ob = window_sink_attention(qb, kb, vb, sink[l], slopes)
        oa = rms_norm(oa, gn_a_g[l].reshape(A_Q_HEADS, HEAD_DIM)).reshape(B, S, QA_W)
        ob = rms_norm(ob, gn_b_g[l].reshape(B_Q_HEADS, HEAD_DIM)).reshape(B, S, QB_W)
        mix = jnp.concatenate([oa, ob], axis=-1) @ w_o[l]
        h = layer_norm(DEEPNORM_ALPHA * h + mix, ln1_g[l], ln1_b[l])
        ffn = peer_ffn(h, peer_wq[l], peer_keys[l], peer_u[l], peer_v[l])
        h = layer_norm(DEEPNORM_ALPHA * h + ffn, ln2_g[l], ln2_b[l])
    return h
```

```python
import functools

import numpy as np
import jax
import jax.numpy as jnp
from jax import lax
from jax.experimental import pallas as pl
from jax.experimental.pallas import tpu as pltpu

D_MODEL = 1024
HEAD_DIM = 64
N_A_HEADS = 8
N_B_HEADS = 8
PAIR_W = 2 * HEAD_DIM
Q_W = N_A_HEADS * HEAD_DIM
KV_W = 2 * HEAD_DIM
IN_WIDTH = 2 * (Q_W + 2 * KV_W)
BLOCK = 128
WINDOW = 128
GRID_W = 64
ROPE_THETA = 10000.0
ROPE_FREQS = HEAD_DIM // 4

PEER_HEADS = 8
PEER_NKEYS = 128
PEER_EXPERTS = PEER_NKEYS * PEER_NKEYS
PEER_DKEY = 256
PEER_TOPK = 16

LN_EPS = 1e-5
RMS_EPS = 1e-6
NEG_INF = -1e30

VMEM_LIMIT_BYTES = 56 * 1024 * 1024

BF16 = jnp.bfloat16
F32 = jnp.float32


def _alibi_slopes(n):
    return [float(2.0 ** (-((i + 1) * 8.0 / n))) for i in range(n)]


def _params(*sem):
    return pltpu.CompilerParams(dimension_semantics=sem,
                                vmem_limit_bytes=VMEM_LIMIT_BYTES)


def _layer_norm(z, g, b):
    mu = jnp.mean(z, axis=-1, keepdims=True)
    zc = z - mu
    var = jnp.mean(zc * zc, axis=-1, keepdims=True)
    return zc * lax.rsqrt(var + LN_EPS) * g + b


def _half_rms(t, lo):
    sq = t * t
    s_lo = jnp.sum(jnp.where(lo, sq, 0.0), axis=-1, keepdims=True)
    s_hi = jnp.sum(jnp.where(lo, 0.0, sq), axis=-1, keepdims=True)
    return jnp.where(lo, lax.rsqrt(s_lo / HEAD_DIM + RMS_EPS),
                     lax.rsqrt(s_hi / HEAD_DIM + RMS_EPS))


def _ln_kernel(x_ref, g_ref, b_ref, o_ref):
    o_ref[...] = _layer_norm(x_ref[...], g_ref[...], b_ref[...])


def _ln_call(x, g, b, tm):
    T, D = x.shape
    return pl.pallas_call(
        _ln_kernel,
        grid=(T // tm,),
        in_specs=[pl.BlockSpec((tm, D), lambda i: (i, 0)),
                  pl.BlockSpec((1, D), lambda i: (0, 0)),
                  pl.BlockSpec((1, D), lambda i: (0, 0))],
        out_specs=pl.BlockSpec((tm, D), lambda i: (i, 0)),
        out_shape=jax.ShapeDtypeStruct((T, D), F32),
        compiler_params=_params("parallel"),
        name="ln_in",
    )(x, g, b)


def _proj_kernel(h_ref, w_ref, qg_ref, kg_ref, cos_ref, sa_ref, sb_ref,
                 qa_ref, ka_ref, va_ref, qb_ref, kb_ref, vb_ref):
    tm = h_ref.shape[0]
    y = jnp.dot(h_ref[...].astype(BF16), w_ref[...], preferred_element_type=F32)
    lo = lax.broadcasted_iota(jnp.int32, (tm, PAIR_W), 1) < HEAD_DIM
    cos, sa, sb = cos_ref[...], sa_ref[...], sb_ref[...]
    scale = HEAD_DIM ** -0.5

    def norm_rope(t, g):
        xn = t * _half_rms(t, lo) * g
        return xn * cos + pltpu.roll(xn, PAIR_W - ROPE_FREQS, 1) * sa + pltpu.roll(xn, ROPE_FREQS, 1) * sb

    for i in range(Q_W // PAIR_W):
        t = y[:, i * PAIR_W:(i + 1) * PAIR_W]
        qa_ref[:, i * PAIR_W:(i + 1) * PAIR_W] = (norm_rope(t, qg_ref[...]) * scale).astype(BF16)
    off = Q_W
    ka_ref[...] = norm_rope(y[:, off:off + KV_W], kg_ref[...]).astype(BF16)
    off += KV_W
    va_ref[...] = y[:, off:off + KV_W].astype(BF16)
    off += KV_W
    qb_ref[...] = (y[:, off:off + Q_W] * scale).astype(BF16)
    off += Q_W
    kb_ref[...] = y[:, off:off + KV_W].astype(BF16)
    off += KV_W
    vb_ref[...] = y[:, off:off + KV_W].astype(BF16)


def _proj_call(h, w, qg, kg, cos, sa, sb, S, tm):
    T, D = h.shape
    n_pos = S // tm
    row = lambda i: (i, 0)
    const = lambda i: (0, 0)
    pos = lambda i: (i % n_pos, 0)
    outs = [jax.ShapeDtypeStruct((T, w_), BF16) for w_ in (Q_W, KV_W, KV_W, Q_W, KV_W, KV_W)]
    return pl.pallas_call(
        _proj_kernel,
        grid=(T // tm,),
        in_specs=[pl.BlockSpec((tm, D), row),
                  pl.BlockSpec((D, IN_WIDTH), const),
                  pl.BlockSpec((1, PAIR_W), const),
                  pl.BlockSpec((1, PAIR_W), const),
                  pl.BlockSpec((tm, PAIR_W), pos),
                  pl.BlockSpec((tm, PAIR_W), pos),
                  pl.BlockSpec((tm, PAIR_W), pos)],
        out_specs=[pl.BlockSpec((tm, o.shape[1]), row) for o in outs],
        out_shape=outs,
        compiler_params=_params("parallel"),
        name="in_proj",
    )(h, w, qg, kg, cos, sa, sb)


def _attn_a_kernel(q_ref, k_ref, v_ref, g_ref, o_ref):
    tq = q_ref.shape[0]
    k = k_ref[...]
    v = v_ref[...]
    lo = lax.broadcasted_iota(jnp.int32, (tq, PAIR_W), 1) < HEAD_DIM
    nt = (((1,), (1,)), ((), ()))
    for i in range(Q_W // PAIR_W):
        qp = q_ref[:, i * PAIR_W:(i + 1) * PAIR_W]
        halves = []
        for grp in range(2):
            qz = jnp.where(lo if grp == 0 else jnp.logical_not(lo), qp, jnp.zeros_like(qp))
            s = lax.dot_general(qz, k, nt, preferred_element_type=F32)
            m = jnp.max(s, axis=-1, keepdims=True)
            p = jnp.exp(s - m)
            l = jnp.sum(p, axis=-1, keepdims=True)
            halves.append(jnp.dot(p.astype(BF16), v, preferred_element_type=F32) / l)
        o = jnp.where(lo, halves[0], halves[1])
        o = o * _half_rms(o, lo) * g_ref[:, i * PAIR_W:(i + 1) * PAIR_W]
        o_ref[:, i * PAIR_W:(i + 1) * PAIR_W] = o.astype(BF16)


def _attn_a_call(q, k, v, g, B, S, tq):
    T = q.shape[0]
    nq = S // tq
    return pl.pallas_call(
        _attn_a_kernel,
        grid=(B, nq),
        in_specs=[pl.BlockSpec((tq, Q_W), lambda b, i: (b * nq + i, 0)),
                  pl.BlockSpec((S, KV_W), lambda b, i: (b, 0)),
                  pl.BlockSpec((S, KV_W), lambda b, i: (b, 0)),
                  pl.BlockSpec((1, Q_W), lambda b, i: (0, 0))],
        out_specs=pl.BlockSpec((tq, Q_W), lambda b, i: (b * nq + i, 0)),
        out_shape=jax.ShapeDtypeStruct((T, Q_W), BF16),
        compiler_params=_params("parallel", "parallel"),
        name="attn_global",
    )(q, k, v, g)


def _attn_b_kernel(sink_ref, q_ref, k_ref, v_ref, g_ref, o_ref, *, slopes):
    j = pl.program_id(1)
    nb = k_ref.shape[0] // BLOCK
    span = BLOCK + 2 * WINDOW
    ks, vs = [], []
    for c in (-1, 0, 1):
        jc = j + c
        start = pl.multiple_of(jnp.clip(jc, 0, nb - 1) * BLOCK, BLOCK)
        ks.append(k_ref[pl.ds(start, BLOCK), :])
        vs.append(v_ref[pl.ds(start, BLOCK), :])
    kslab = jnp.concatenate(ks, axis=0)
    vslab = jnp.concatenate(vs, axis=0)
    qi = lax.broadcasted_iota(jnp.int32, (BLOCK, span), 0)
    ki = lax.broadcasted_iota(jnp.int32, (BLOCK, span), 1)
    dist = jnp.abs(qi + WINDOW - ki)
    s_pos = (j - 1) * BLOCK + ki
    in_seq = jnp.logical_and(s_pos >= 0, s_pos < nb * BLOCK)
    valid = jnp.logical_and(dist <= WINDOW, in_seq)
    distf = dist.astype(F32)

    lo = lax.broadcasted_iota(jnp.int32, (BLOCK, PAIR_W), 1) < HEAD_DIM
    hi = jnp.logical_not(lo)
    n_pairs = Q_W // PAIR_W
    heads = [(i, grp) for grp in range(2) for i in range(n_pairs)]
    qs = []
    for i, grp in heads:
        qp = q_ref[:, i * PAIR_W:(i + 1) * PAIR_W]
        qs.append(jnp.where(lo if grp == 0 else hi, qp, jnp.zeros_like(qp)))
    s_all = lax.dot_general(jnp.concatenate(qs, axis=0), kslab, (((1,), (1,)), ((), ())),
                            preferred_element_type=F32)
    ps, inv = [], []
    for n, (i, grp) in enumerate(heads):
        hd = i + n_pairs * grp
        s = s_all[n * BLOCK:(n + 1) * BLOCK] - slopes[hd] * distf
        s = jnp.where(valid, s, NEG_INF)
        sink = sink_ref[hd]
        m = jnp.maximum(jnp.max(s, axis=-1, keepdims=True), sink)
        p = jnp.exp(s - m)
        inv.append(1.0 / (jnp.sum(p, axis=-1, keepdims=True) + jnp.exp(sink - m)))
        ps.append(p.astype(BF16))
    o_all = jnp.dot(jnp.concatenate(ps, axis=0), vslab, preferred_element_type=F32)
    for i in range(n_pairs):
        o0 = o_all[i * BLOCK:(i + 1) * BLOCK] * inv[i]
        o1 = o_all[(i + n_pairs) * BLOCK:(i + n_pairs + 1) * BLOCK] * inv[i + n_pairs]
        o = jnp.where(lo, o0, o1)
        o = o * _half_rms(o, lo) * g_ref[:, i * PAIR_W:(i + 1) * PAIR_W]
        o_ref[:, i * PAIR_W:(i + 1) * PAIR_W] = o.astype(BF16)


def _attn_b_call(sink, q, k, v, g, B, S):
    T = q.shape[0]
    nb = S // BLOCK
    kern = functools.partial(_attn_b_kernel, slopes=_alibi_slopes(N_B_HEADS))
    return pl.pallas_call(
        kern,
        grid=(B, nb),
        in_specs=[pl.BlockSpec(memory_space=pltpu.SMEM),
                  pl.BlockSpec((BLOCK, Q_W), lambda b, i: (b * nb + i, 0)),
                  pl.BlockSpec((S, KV_W), lambda b, i: (b, 0)),
                  pl.BlockSpec((S, KV_W), lambda b, i: (b, 0)),
                  pl.BlockSpec((1, Q_W), lambda b, i: (0, 0))],
        out_specs=pl.BlockSpec((BLOCK, Q_W), lambda b, i: (b * nb + i, 0)),
        out_shape=jax.ShapeDtypeStruct((T, Q_W), BF16),
        compiler_params=_params("parallel", "parallel"),
        name="attn_window",
    )(sink, q, k, v, g)


def _oproj_kernel(oa_ref, ob_ref, wa_ref, wb_ref, h_ref, g_ref, b_ref, o_ref, *, alpha):
    mix = jnp.dot(oa_ref[...], wa_ref[...], preferred_element_type=F32)
    mix = mix + jnp.dot(ob_ref[...], wb_ref[...], preferred_element_type=F32)
    o_ref[...] = _layer_norm(alpha * h_ref[...] + mix, g_ref[...], b_ref[...])


def _oproj_call(oa, ob, wa, wb, h, g, b, alpha, tm):
    T, D = h.shape
    row = lambda i: (i, 0)
    const = lambda i: (0, 0)
    return pl.pallas_call(
        functools.partial(_oproj_kernel, alpha=alpha),
        grid=(T // tm,),
        in_specs=[pl.BlockSpec((tm, Q_W), row), pl.BlockSpec((tm, Q_W), row),
                  pl.BlockSpec((Q_W, D), const), pl.BlockSpec((Q_W, D), const),
                  pl.BlockSpec((tm, D), row),
                  pl.BlockSpec((1, D), const), pl.BlockSpec((1, D), const)],
        out_specs=pl.BlockSpec((tm, D), row),
        out_shape=jax.ShapeDtypeStruct((T, D), F32),
        compiler_params=_params("parallel"),
        name="out_proj_ln",
    )(oa, ob, wa, wb, h, g, b)


def _peer_scores_kernel(h_ref, wq_ref, keys_ref, st_ref):
    q = jnp.dot(h_ref[...].astype(BF16), wq_ref[...], preferred_element_type=F32).astype(BF16)
    half = PEER_DKEY // 2
    for n in range(2 * PEER_HEADS):
        qs = q[:, n * half:(n + 1) * half]
        st_ref[n] = lax.dot_general(keys_ref[n], qs, (((1,), (1,)), ((), ())),
                                    preferred_element_type=F32)


def _peer_scores_call(h, wq, keys, tm):
    T, D = h.shape
    n_sets = 2 * PEER_HEADS
    half = PEER_DKEY // 2
    return pl.pallas_call(
        _peer_scores_kernel,
        grid=(T // tm,),
        in_specs=[pl.BlockSpec((tm, D), lambda i: (i, 0)),
                  pl.BlockSpec((D, PEER_HEADS * PEER_DKEY), lambda i: (0, 0)),
                  pl.BlockSpec((n_sets, PEER_NKEYS, half), lambda i: (0, 0, 0))],
        out_specs=pl.BlockSpec((n_sets, PEER_NKEYS, tm), lambda i: (0, 0, i)),
        out_shape=jax.ShapeDtypeStruct((n_sets, PEER_NKEYS, T), F32),
        compiler_params=_params("parallel"),
        name="peer_scores",
    )(h, wq, keys)


def _router_kernel(st_ref, r1_ref, c1_ref, n2_ref, e2_ref):
    tt = st_ref.shape[-1]
    K = PEER_TOPK
    n_lead = 4
    neg = -jnp.inf
    kio = lax.broadcasted_iota(jnp.int32, (PEER_NKEYS, tt), 0)
    rio = lax.broadcasted_iota(jnp.int32, (K, tt), 0)
    blk = kio // K
    rr = kio % K
    pos = jnp.where(blk < n_lead, rr * K + blk, (blk - n_lead) * K + rr)
    dup = jnp.logical_and(blk >= n_lead, rr < n_lead)

    def top16(s):
        rank = jnp.full((PEER_NKEYS, tt), K, jnp.int32)
        arr = jnp.zeros((K, tt), F32)
        rows = []
        for r in range(K):
            m = jnp.max(s, axis=0, keepdims=True)
            first = jnp.min(jnp.where(s == m, kio, PEER_NKEYS), axis=0, keepdims=True)
            hit = kio == first
            rank = jnp.where(hit, r, rank)
            s = jnp.where(hit, neg, s)
            arr = jnp.where(rio == r, m, arr)
            rows.append(m)
        return rank, arr, rows

    def head(h, carry):
        s1 = st_ref[2 * h]
        s2 = st_ref[2 * h + 1]
        rank1, a1, a1r = top16(s1)
        rank2, a2, a2r = top16(s2)
        blocks = [a1 + a2r[r2] for r2 in range(n_lead)] + [a1r[r1] + a2 for r1 in range(n_lead)]
        c0 = jnp.where(dup, neg, jnp.concatenate(blocks, axis=0))
        c = c0
        sel = jnp.zeros((PEER_NKEYS, tt), F32)
        for _ in range(K):
            m = jnp.max(c, axis=0, keepdims=True)
            first = jnp.min(jnp.where(c == m, pos, K * K), axis=0, keepdims=True)
            hit = pos == first
            sel = jnp.where(hit, 1.0, sel)
            c = jnp.where(hit, neg, c)
        top = a1r[0] + a2r[0]
        z = jnp.sum(jnp.where(sel > 0.0, jnp.exp(c0 - top), 0.0), axis=0, keepdims=True)
        tail = sel[n_lead * K:(n_lead + 1) * K]
        for r1 in range(1, n_lead):
            tail = tail + sel[(n_lead + r1) * K:(n_lead + r1 + 1) * K]
        n2 = jnp.zeros((PEER_NKEYS, tt), F32)
        for r2 in range(K):
            if r2 < n_lead:
                cnt = jnp.sum(sel[r2 * K:(r2 + 1) * K], axis=0, keepdims=True)
            else:
                cnt = tail[r2:r2 + 1]
            n2 = jnp.where(rank2 == r2, cnt, n2)
        r1_ref[h] = rank1.astype(F32)
        c1_ref[h] = jnp.exp(s1 - a1r[0]) / z
        n2_ref[h] = n2
        e2_ref[h] = jnp.exp(s2 - a2r[0])
        return carry

    lax.fori_loop(0, PEER_HEADS, head, 0)


def _router_call(st, tt):
    T = st.shape[-1]
    out = jax.ShapeDtypeStruct((PEER_HEADS, PEER_NKEYS, T), F32)
    spec = pl.BlockSpec((PEER_HEADS, PEER_NKEYS, tt), lambda i: (0, 0, i))
    return pl.pallas_call(
        _router_kernel,
        grid=(T // tt,),
        in_specs=[pl.BlockSpec((2 * PEER_HEADS, PEER_NKEYS, tt), lambda i: (0, 0, i))],
        out_specs=[spec] * 4,
        out_shape=[out] * 4,
        compiler_params=_params("parallel"),
        name="peer_router",
    )(st)


def _peer_dense_kernel(h_ref, r1_ref, c1_ref, n2_ref, e2_ref, u_ref, vt_ref, g_ref, b_ref,
                       o_ref, xb_ref, acc_ref, *, alpha, n_sub):
    j = pl.program_id(1)

    @pl.when(j == 0)
    def _():
        xb_ref[...] = h_ref[...].astype(BF16)
        acc_ref[...] = jnp.zeros_like(acc_ref)

    act = lax.dot_general(u_ref[...], xb_ref[...], (((1,), (1,)), ((), ())),
                          preferred_element_type=F32)
    sqrt_half = float(np.sqrt(0.5))
    gts = []
    for ib in range(n_sub):
        i1 = j * n_sub + ib
        a = act[ib * PEER_NKEYS:(ib + 1) * PEER_NKEYS]
        w = None
        for hd in range(PEER_HEADS):
            r1 = r1_ref[hd, pl.ds(i1, 1), :]
            c1 = c1_ref[hd, pl.ds(i1, 1), :]
            term = jnp.where(r1 < n2_ref[hd], e2_ref[hd], 0.0) * c1
            w = term if w is None else w + term
        ge = 0.5 * a * (1.0 + lax.erf(a * sqrt_half))
        gts.append((ge * w).astype(BF16))
    gt = jnp.concatenate(gts, axis=0)
    acc_ref[...] += jnp.dot(vt_ref[...], gt, preferred_element_type=F32)

    @pl.when(j == pl.num_programs(1) - 1)
    def _():
        z = alpha * h_ref[...] + acc_ref[...].T
        o_ref[...] = _layer_norm(z, g_ref[...], b_ref[...])


def _peer_dense_call(h, r1, c1, n2, e2, u, vt, g, b, alpha, tt, te):
    T, D = h.shape
    n_sub = te // PEER_NKEYS
    rspec = pl.BlockSpec((PEER_HEADS, PEER_NKEYS, tt), lambda i, j: (0, 0, i))
    return pl.pallas_call(
        functools.partial(_peer_dense_kernel, alpha=alpha, n_sub=n_sub),
        grid=(T // tt, PEER_EXPERTS // te),
        in_specs=[pl.BlockSpec((tt, D), lambda i, j: (i, 0)),
                  rspec, rspec, rspec, rspec,
                  pl.BlockSpec((te, D), lambda i, j: (j, 0)),
                  pl.BlockSpec((D, te), lambda i, j: (0, j)),
                  pl.BlockSpec((1, D), lambda i, j: (0, 0)),
                  pl.BlockSpec((1, D), lambda i, j: (0, 0))],
        out_specs=pl.BlockSpec((tt, D), lambda i, j: (i, 0)),
        out_shape=jax.ShapeDtypeStruct((T, D), F32),
        scratch_shapes=[pltpu.VMEM((tt, D), BF16), pltpu.VMEM((D, tt), F32)],
        compiler_params=_params("parallel", "arbitrary"),
        name="peer_dense",
    )(h, r1, c1, n2, e2, u, vt, g, b)


def _pair_perm():
    idx = []
    for i in range(N_A_HEADS // 2):
        for grp in range(2):
            hd = i + (N_A_HEADS // 2) * grp
            idx.extend(range(hd * HEAD_DIM, (hd + 1) * HEAD_DIM))
    return np.asarray(idx, dtype=np.int32)


def _rope_tables(S):
    pos = np.arange(S)
    row = (pos // GRID_W).astype(np.float32)
    col = (pos % GRID_W).astype(np.float32)
    inv_freq = jnp.asarray(ROPE_THETA, F32) ** (-jnp.arange(ROPE_FREQS, dtype=F32) / ROPE_FREQS)
    ang_r = jnp.asarray(row)[:, None] * inv_freq
    ang_c = jnp.asarray(col)[:, None] * inv_freq
    cr, sr, cc, sc = jnp.cos(ang_r), jnp.sin(ang_r), jnp.cos(ang_c), jnp.sin(ang_c)
    zero = jnp.zeros_like(sr)
    cos = jnp.concatenate([cr, cr, cc, cc], axis=-1)
    sa = jnp.concatenate([-sr, zero, -sc, zero], axis=-1)
    sb = jnp.concatenate([zero, sr, zero, sc], axis=-1)
    tile2 = lambda t: jnp.concatenate([t, t], axis=-1)
    return tile2(cos), tile2(sa), tile2(sb)


def kernel(x, ln_in_g, ln_in_b, w_in, qn_g, kn_g, sink, gn_a_g, gn_b_g, w_o, ln1_g, ln1_b,
           peer_wq, peer_keys, peer_u, peer_v, ln2_g, ln2_b):
    B, S, D = x.shape
    depth = w_in.shape[0]
    T = B * S
    alpha = float((2.0 * depth) ** 0.25)
    tm = min(512, S)
    tq = min(256, S)
    tt_route = 128
    tt_dense = min(512, T)
    te = 512

    perm = _pair_perm()
    col_perm = np.concatenate([perm, Q_W + np.arange(2 * KV_W, dtype=np.int32),
                               Q_W + 2 * KV_W + perm,
                               2 * Q_W + 2 * KV_W + np.arange(2 * KV_W, dtype=np.int32)])
    cos, sa, sb = _rope_tables(S)
    row2 = lambda v: v.reshape(1, -1)
    tile2 = lambda v: jnp.concatenate([v, v]).reshape(1, -1)

    h = _ln_call(x.reshape(T, D), row2(ln_in_g), row2(ln_in_b), tm)
    for l in range(depth):
        w = w_in[l][:, col_perm].astype(BF16)
        qa, ka, va, qb, kb, vb = _proj_call(h, w, tile2(qn_g[l]), tile2(kn_g[l]), cos, sa, sb, S, tm)
        oa = _attn_a_call(qa, ka, va, row2(gn_a_g[l][perm]), B, S, tq)
        ob = _attn_b_call(sink[l], qb, kb, vb, row2(gn_b_g[l][perm]), B, S)
        wa = w_o[l][:Q_W][perm].astype(BF16)
        wb = w_o[l][Q_W:][perm].astype(BF16)
        h = _oproj_call(oa, ob, wa, wb, h, row2(ln1_g[l]), row2(ln1_b[l]), alpha, tm)
        keys = peer_keys[l].reshape(2 * PEER_HEADS, PEER_NKEYS, PEER_DKEY // 2).astype(BF16)
        st = _peer_scores_call(h, peer_wq[l].astype(BF16), keys, tm)
        r1, c1, n2, e2 = _router_call(st, tt_route)
        h = _peer_dense_call(h, r1, c1, n2, e2, peer_u[l].astype(BF16), peer_v[l].T.astype(BF16),
                             row2(ln2_g[l]), row2(ln2_b[l]), alpha, tt_dense, te)
    return h.reshape(B, S, D)
```

```python
import functools

import numpy as np
import jax
import jax.numpy as jnp
from jax import lax
from jax.experimental import pallas as pl
from jax.experimental.pallas import tpu as pltpu

D_MODEL = 1024
HEAD_DIM = 64
N_A_HEADS = 8
N_B_HEADS = 8
LANES = 128
PAIR_W = 2 * HEAD_DIM
Q_W = N_A_HEADS * HEAD_DIM
KV_W = 2 * HEAD_DIM
IN_WIDTH = 2 * (Q_W + 2 * KV_W)
BLOCK = 128
WINDOW = 128
GRID_W = 64
ROPE_THETA = 10000.0
ROPE_FREQS = HEAD_DIM // 4

PEER_HEADS = 8
PEER_NKEYS = 128
PEER_EXPERTS = PEER_NKEYS * PEER_NKEYS
PEER_DKEY = 256
PEER_TOPK = 16

LN_EPS = 1e-5
RMS_EPS = 1e-6
NEG_INF = -1e30

VMEM_LIMIT_BYTES = 56 * 1024 * 1024

BF16 = jnp.bfloat16
F32 = jnp.float32


def _alibi_slopes(n):
    return [float(2.0 ** (-((i + 1) * 8.0 / n))) for i in range(n)]


def _params(*sem):
    return pltpu.CompilerParams(dimension_semantics=sem,
                                vmem_limit_bytes=VMEM_LIMIT_BYTES)


def _pack_bf16_rows(w):
    r, c = w.shape
    pairs = w.astype(BF16).reshape(r // 2, 2, c).swapaxes(1, 2)
    return lax.bitcast_convert_type(pairs, jnp.uint32)


def _unpack_bf16(words):
    return pltpu.bitcast(words, BF16)


def _layer_norm(z, g, b):
    mu = jnp.mean(z, axis=-1, keepdims=True)
    zc = z - mu
    var = jnp.mean(zc * zc, axis=-1, keepdims=True)
    return zc * lax.rsqrt(var + LN_EPS) * g + b


def _half_rms(t, lo):
    sq = t * t
    s_lo = jnp.sum(jnp.where(lo, sq, 0.0), axis=-1, keepdims=True)
    s_hi = jnp.sum(jnp.where(lo, 0.0, sq), axis=-1, keepdims=True)
    return jnp.where(lo, lax.rsqrt(s_lo / HEAD_DIM + RMS_EPS),
                     lax.rsqrt(s_hi / HEAD_DIM + RMS_EPS))


def _ln_kernel(x_ref, g_ref, b_ref, o_ref):
    o_ref[...] = _layer_norm(x_ref[...], g_ref[...], b_ref[...])


def _ln_call(x, g, b, tm):
    T, D = x.shape
    return pl.pallas_call(
        _ln_kernel,
        grid=(T // tm,),
        in_specs=[pl.BlockSpec((tm, D), lambda i: (i, 0)),
                  pl.BlockSpec((1, D), lambda i: (0, 0)),
                  pl.BlockSpec((1, D), lambda i: (0, 0))],
        out_specs=pl.BlockSpec((tm, D), lambda i: (i, 0)),
        out_shape=jax.ShapeDtypeStruct((T, D), F32),
        compiler_params=_params("parallel"),
        name="ln_in",
    )(x, g, b)


def _proj_kernel(h_ref, w_ref, qg_ref, kg_ref, cos_ref, sa_ref, sb_ref,
                 qa_ref, ka_ref, va_ref, qb_ref, kb_ref, vb_ref):
    tm = h_ref.shape[0]
    y = jnp.dot(h_ref[...].astype(BF16), w_ref[...], preferred_element_type=F32)
    lo = lax.broadcasted_iota(jnp.int32, (tm, PAIR_W), 1) < HEAD_DIM
    cos, sa, sb = cos_ref[...], sa_ref[...], sb_ref[...]
    scale = HEAD_DIM ** -0.5

    def norm_rope(t, g):
        xn = t * _half_rms(t, lo) * g
        return xn * cos + pltpu.roll(xn, PAIR_W - ROPE_FREQS, 1) * sa + pltpu.roll(xn, ROPE_FREQS, 1) * sb

    for i in range(Q_W // PAIR_W):
        t = y[:, i * PAIR_W:(i + 1) * PAIR_W]
        qa_ref[:, i * PAIR_W:(i + 1) * PAIR_W] = (norm_rope(t, qg_ref[...]) * scale).astype(BF16)
    off = Q_W
    ka_ref[...] = norm_rope(y[:, off:off + KV_W], kg_ref[...]).astype(BF16)
    off += KV_W
    va_ref[...] = y[:, off:off + KV_W].astype(BF16)
    off += KV_W
    qb_ref[...] = (y[:, off:off + Q_W] * scale).astype(BF16)
    off += Q_W
    kb_ref[...] = y[:, off:off + KV_W].astype(BF16)
    off += KV_W
    vb_ref[...] = y[:, off:off + KV_W].astype(BF16)


def _proj_call(h, w, qg, kg, cos, sa, sb, S, tm):
    T, D = h.shape
    n_pos = S // tm
    row = lambda i: (i, 0)
    const = lambda i: (0, 0)
    pos = lambda i: (i % n_pos, 0)
    outs = [jax.ShapeDtypeStruct((T, w_), BF16) for w_ in (Q_W, KV_W, KV_W, Q_W, KV_W, KV_W)]
    return pl.pallas_call(
        _proj_kernel,
        grid=(T // tm,),
        in_specs=[pl.BlockSpec((tm, D), row),
                  pl.BlockSpec((D, IN_WIDTH), const),
                  pl.BlockSpec((1, PAIR_W), const),
                  pl.BlockSpec((1, PAIR_W), const),
                  pl.BlockSpec((tm, PAIR_W), pos),
                  pl.BlockSpec((tm, PAIR_W), pos),
                  pl.BlockSpec((tm, PAIR_W), pos)],
        out_specs=[pl.BlockSpec((tm, o.shape[1]), row) for o in outs],
        out_shape=outs,
        compiler_params=_params("parallel"),
        name="in_proj",
    )(h, w, qg, kg, cos, sa, sb)


def _attn_a_kernel(q_ref, k_ref, v_ref, g_ref, o_ref):
    tq = q_ref.shape[0]
    k = k_ref[...]
    v = v_ref[...]
    lo = lax.broadcasted_iota(jnp.int32, (tq, PAIR_W), 1) < HEAD_DIM
    nt = (((1,), (1,)), ((), ()))
    for i in range(Q_W // PAIR_W):
        qp = q_ref[:, i * PAIR_W:(i + 1) * PAIR_W]
        halves = []
        for grp in range(2):
            qz = jnp.where(lo if grp == 0 else jnp.logical_not(lo), qp, jnp.zeros_like(qp))
            s = lax.dot_general(qz, k, nt, preferred_element_type=F32)
            m = jnp.max(s, axis=-1, keepdims=True)
            p = jnp.exp(s - m)
            l = jnp.sum(p, axis=-1, keepdims=True)
            halves.append(jnp.dot(p.astype(BF16), v, preferred_element_type=F32) / l)
        o = jnp.where(lo, halves[0], halves[1])
        o = o * _half_rms(o, lo) * g_ref[:, i * PAIR_W:(i + 1) * PAIR_W]
        o_ref[:, i * PAIR_W:(i + 1) * PAIR_W] = o.astype(BF16)


def _attn_a_call(q, k, v, g, B, S, tq):
    T = q.shape[0]
    nq = S // tq
    return pl.pallas_call(
        _attn_a_kernel,
        grid=(B, nq),
        in_specs=[pl.BlockSpec((tq, Q_W), lambda b, i: (b * nq + i, 0)),
                  pl.BlockSpec((S, KV_W), lambda b, i: (b, 0)),
                  pl.BlockSpec((S, KV_W), lambda b, i: (b, 0)),
                  pl.BlockSpec((1, Q_W), lambda b, i: (0, 0))],
        out_specs=pl.BlockSpec((tq, Q_W), lambda b, i: (b * nq + i, 0)),
        out_shape=jax.ShapeDtypeStruct((T, Q_W), BF16),
        compiler_params=_params("parallel", "parallel"),
        name="attn_global",
    )(q, k, v, g)


def _attn_b_kernel(sink_ref, q_ref, k_ref, v_ref, g_ref, o_ref, *, slopes):
    j = pl.program_id(1)
    nb = k_ref.shape[0] // BLOCK
    span = BLOCK + 2 * WINDOW
    ks, vs = [], []
    for c in (-1, 0, 1):
        jc = j + c
        start = pl.multiple_of(jnp.clip(jc, 0, nb - 1) * BLOCK, BLOCK)
        ks.append(k_ref[pl.ds(start, BLOCK), :])
        vs.append(v_ref[pl.ds(start, BLOCK), :])
    kslab = jnp.concatenate(ks, axis=0)
    vslab = jnp.concatenate(vs, axis=0)
    qi = lax.broadcasted_iota(jnp.int32, (BLOCK, span), 0)
    ki = lax.broadcasted_iota(jnp.int32, (BLOCK, span), 1)
    dist = jnp.abs(qi + WINDOW - ki)
    s_pos = (j - 1) * BLOCK + ki
    in_seq = jnp.logical_and(s_pos >= 0, s_pos < nb * BLOCK)
    valid = jnp.logical_and(dist <= WINDOW, in_seq)
    distf = dist.astype(F32)

    lo = lax.broadcasted_iota(jnp.int32, (BLOCK, PAIR_W), 1) < HEAD_DIM
    hi = jnp.logical_not(lo)
    n_pairs = Q_W // PAIR_W
    heads = [(i, grp) for grp in range(2) for i in range(n_pairs)]
    qs = []
    for i, grp in heads:
        qp = q_ref[:, i * PAIR_W:(i + 1) * PAIR_W]
        qs.append(jnp.where(lo if grp == 0 else hi, qp, jnp.zeros_like(qp)))
    s_all = lax.dot_general(jnp.concatenate(qs, axis=0), kslab, (((1,), (1,)), ((), ())),
                            preferred_element_type=F32)
    ps, inv = [], []
    for n, (i, grp) in enumerate(heads):
        hd = i + n_pairs * grp
        s = s_all[n * BLOCK:(n + 1) * BLOCK] - slopes[hd] * distf
        s = jnp.where(valid, s, NEG_INF)
        sink = sink_ref[hd]
        m = jnp.maximum(jnp.max(s, axis=-1, keepdims=True), sink)
        p = jnp.exp(s - m)
        inv.append(1.0 / (jnp.sum(p, axis=-1, keepdims=True) + jnp.exp(sink - m)))
        ps.append(p.astype(BF16))
    o_all = jnp.dot(jnp.concatenate(ps, axis=0), vslab, preferred_element_type=F32)
    for i in range(n_pairs):
        o0 = o_all[i * BLOCK:(i + 1) * BLOCK] * inv[i]
        o1 = o_all[(i + n_pairs) * BLOCK:(i + n_pairs + 1) * BLOCK] * inv[i + n_pairs]
        o = jnp.where(lo, o0, o1)
        o = o * _half_rms(o, lo) * g_ref[:, i * PAIR_W:(i + 1) * PAIR_W]
        o_ref[:, i * PAIR_W:(i + 1) * PAIR_W] = o.astype(BF16)


def _attn_b_call(sink, q, k, v, g, B, S):
    T = q.shape[0]
    nb = S // BLOCK
    kern = functools.partial(_attn_b_kernel, slopes=_alibi_slopes(N_B_HEADS))
    return pl.pallas_call(
        kern,
        grid=(B, nb),
        in_specs=[pl.BlockSpec(memory_space=pltpu.SMEM),
                  pl.BlockSpec((BLOCK, Q_W), lambda b, i: (b * nb + i, 0)),
                  pl.BlockSpec((S, KV_W), lambda b, i: (b, 0)),
                  pl.BlockSpec((S, KV_W), lambda b, i: (b, 0)),
                  pl.BlockSpec((1, Q_W), lambda b, i: (0, 0))],
        out_specs=pl.BlockSpec((BLOCK, Q_W), lambda b, i: (b * nb + i, 0)),
        out_shape=jax.ShapeDtypeStruct((T, Q_W), BF16),
        compiler_params=_params("parallel", "parallel"),
        name="attn_window",
    )(sink, q, k, v, g)


def _oproj_kernel(oa_ref, ob_ref, wa_ref, wb_ref, h_ref, g_ref, b_ref, o_ref, *, alpha):
    mix = jnp.dot(oa_ref[...], wa_ref[...], preferred_element_type=F32)
    mix = mix + jnp.dot(ob_ref[...], wb_ref[...], preferred_element_type=F32)
    o_ref[...] = _layer_norm(alpha * h_ref[...] + mix, g_ref[...], b_ref[...])


def _oproj_call(oa, ob, wa, wb, h, g, b, alpha, tm):
    T, D = h.shape
    row = lambda i: (i, 0)
    const = lambda i: (0, 0)
    return pl.pallas_call(
        functools.partial(_oproj_kernel, alpha=alpha),
        grid=(T // tm,),
        in_specs=[pl.BlockSpec((tm, Q_W), row), pl.BlockSpec((tm, Q_W), row),
                  pl.BlockSpec((Q_W, D), const), pl.BlockSpec((Q_W, D), const),
                  pl.BlockSpec((tm, D), row),
                  pl.BlockSpec((1, D), const), pl.BlockSpec((1, D), const)],
        out_specs=pl.BlockSpec((tm, D), row),
        out_shape=jax.ShapeDtypeStruct((T, D), F32),
        compiler_params=_params("parallel"),
        name="out_proj_ln",
    )(oa, ob, wa, wb, h, g, b)


def _peer_scores_kernel(h_ref, wq_ref, keys_ref, st_ref):
    q = jnp.dot(h_ref[...].astype(BF16), wq_ref[...], preferred_element_type=F32).astype(BF16)
    half = PEER_DKEY // 2
    for n in range(2 * PEER_HEADS):
        qs = q[:, n * half:(n + 1) * half]
        st_ref[n] = lax.dot_general(keys_ref[n], qs, (((1,), (1,)), ((), ())),
                                    preferred_element_type=F32)


def _peer_scores_call(h, wq, keys, tm):
    T, D = h.shape
    n_sets = 2 * PEER_HEADS
    half = PEER_DKEY // 2
    return pl.pallas_call(
        _peer_scores_kernel,
        grid=(T // tm,),
        in_specs=[pl.BlockSpec((tm, D), lambda i: (i, 0)),
                  pl.BlockSpec((D, PEER_HEADS * PEER_DKEY), lambda i: (0, 0)),
                  pl.BlockSpec((n_sets, PEER_NKEYS, half), lambda i: (0, 0, 0))],
        out_specs=pl.BlockSpec((n_sets, PEER_NKEYS, tm), lambda i: (0, 0, i)),
        out_shape=jax.ShapeDtypeStruct((n_sets, PEER_NKEYS, T), F32),
        compiler_params=_params("parallel"),
        name="peer_scores",
    )(h, wq, keys)


def _router_kernel(st_ref, r1_ref, c1_ref, n2_ref, e2_ref):
    tt = st_ref.shape[-1]
    K = PEER_TOPK
    n_lead = 4
    neg = -jnp.inf
    kio_i = lax.broadcasted_iota(jnp.int32, (PEER_NKEYS, tt), 0)
    rio = lax.broadcasted_iota(jnp.int32, (K, tt), 0)
    blk = kio_i // K
    rr = kio_i % K
    pos = jnp.where(blk < n_lead, rr * K + blk, (blk - n_lead) * K + rr).astype(F32)
    dup = jnp.logical_and(blk >= n_lead, rr < n_lead)
    kio = kio_i.astype(F32)

    def top16(s):
        rank = jnp.full((PEER_NKEYS, tt), float(K), F32)
        arr = jnp.zeros((K, tt), F32)
        rows = []
        for r in range(K):
            m = jnp.max(s, axis=0, keepdims=True)
            first = jnp.min(jnp.where(s == m, kio, float(PEER_NKEYS)), axis=0, keepdims=True)
            hit = kio == first
            rank = jnp.where(hit, float(r), rank)
            s = jnp.where(hit, neg, s)
            arr = jnp.where(rio == r, m, arr)
            rows.append(m)
        return rank, arr, rows

    def head(h, carry):
        s1 = st_ref[2 * h]
        s2 = st_ref[2 * h + 1]
        rank1, a1, a1r = top16(s1)
        rank2, a2, a2r = top16(s2)
        blocks = [a1 + a2r[r2] for r2 in range(n_lead)] + [a1r[r1] + a2 for r1 in range(n_lead)]
        c0 = jnp.where(dup, neg, jnp.concatenate(blocks, axis=0))
        c = c0
        sel = jnp.zeros((PEER_NKEYS, tt), F32)
        for _ in range(K):
            m = jnp.max(c, axis=0, keepdims=True)
            first = jnp.min(jnp.where(c == m, pos, float(K * K)), axis=0, keepdims=True)
            hit = pos == first
            sel = jnp.where(hit, 1.0, sel)
            c = jnp.where(hit, neg, c)
        top = a1r[0] + a2r[0]
        z = jnp.sum(jnp.where(sel > 0.0, jnp.exp(c0 - top), 0.0), axis=0, keepdims=True)
        tail = sel[n_lead * K:(n_lead + 1) * K]
        for r1 in range(1, n_lead):
            tail = tail + sel[(n_lead + r1) * K:(n_lead + r1 + 1) * K]
        n2 = jnp.zeros((PEER_NKEYS, tt), F32)
        for r2 in range(K):
            if r2 < n_lead:
                cnt = jnp.sum(sel[r2 * K:(r2 + 1) * K], axis=0, keepdims=True)
            else:
                cnt = tail[r2:r2 + 1]
            n2 = jnp.where(rank2 == float(r2), cnt, n2)
        r1_ref[h] = rank1
        c1_ref[h] = jnp.exp(s1 - a1r[0]) / z
        n2_ref[h] = n2
        e2_ref[h] = jnp.exp(s2 - a2r[0])
        return carry

    lax.fori_loop(0, PEER_HEADS, head, 0)


def _router_call(st, tt):
    T = st.shape[-1]
    shape = (PEER_HEADS, PEER_NKEYS, T)
    outs = [jax.ShapeDtypeStruct(shape, F32)] * 4
    spec = pl.BlockSpec((PEER_HEADS, PEER_NKEYS, tt), lambda i: (0, 0, i))
    return pl.pallas_call(
        _router_kernel,
        grid=(T // tt,),
        in_specs=[pl.BlockSpec((2 * PEER_HEADS, PEER_NKEYS, tt), lambda i: (0, 0, i))],
        out_specs=[spec] * 4,
        out_shape=outs,
        compiler_params=_params("parallel"),
        name="peer_router",
    )(st)


PEER_CHUNK_ROWS = 256


def _peer_dense_kernel(h_ref, r1_ref, c1_ref, n2_ref, e2_ref, u_first_ref, u_a_ref, u_b_ref,
                       vt_a_ref, vt_b_ref, vt_last_ref, g_ref, b_ref, o_ref,
                       xt_ref, acc_ref, n2s_ref, e2s_ref, act_a_ref, act_b_ref, gt_a_ref, gt_b_ref,
                       *, alpha):
    j = pl.program_id(1)
    tt = h_ref.shape[0]
    te = vt_a_ref.shape[1]
    n_sub = te // PEER_NKEYS
    tile = (PEER_NKEYS, LANES)
    sqrt_half = float(np.sqrt(0.5))

    @pl.when(j == 0)
    def _():
        xt_ref[...] = h_ref[...].T.astype(BF16)
        acc_ref[...] = jnp.zeros_like(acc_ref)
        n2s_ref[...] = n2_ref[...].astype(BF16)
        e2s_ref[...] = e2_ref[...].astype(BF16)
        gt_b_ref[...] = jnp.zeros_like(gt_b_ref)
        act_a_ref[...] = jnp.dot(_unpack_bf16(u_first_ref[...]), xt_ref[...],
                                 preferred_element_type=F32)

    def phase(u_next_ref, act_next_ref, act_cur_ref, gt_cur_ref, vt_prev_ref, gt_prev_ref, key_off):
        def chunk(c, carry):
            rows = pl.ds(pl.multiple_of(c * PEER_CHUNK_ROWS, PEER_CHUNK_ROWS), PEER_CHUNK_ROWS)
            prows = pl.ds(pl.multiple_of(c * (PEER_CHUNK_ROWS // 2), PEER_CHUNK_ROWS // 2),
                          PEER_CHUNK_ROWS // 2)
            act_next_ref[rows, :] = jnp.dot(_unpack_bf16(u_next_ref[prows, :]), xt_ref[...],
                                            preferred_element_type=F32)
            n_ib = PEER_CHUNK_ROWS // PEER_NKEYS
            ibs = [c * n_ib + s for s in range(n_ib)]
            r1_rows = [[r1_ref[hd, pl.ds(key_off + ib, 1), :] for hd in range(PEER_HEADS)] for ib in ibs]
            c1_rows = [[c1_ref[hd, pl.ds(key_off + ib, 1), :] for hd in range(PEER_HEADS)] for ib in ibs]
            for tg in range(tt // LANES):
                lanes = slice(tg * LANES, (tg + 1) * LANES)
                ws = [None] * n_ib
                for hd in range(PEER_HEADS):
                    n2t = n2s_ref[hd, :, lanes]
                    e2t = e2s_ref[hd, :, lanes]
                    for s in range(n_ib):
                        r1 = jnp.broadcast_to(r1_rows[s][hd][:, lanes], tile).astype(BF16)
                        c1 = jnp.broadcast_to(c1_rows[s][hd][:, lanes], tile).astype(BF16)
                        term = jnp.where(r1 < n2t, e2t, jnp.zeros(tile, BF16)) * c1
                        ws[s] = term if ws[s] is None else ws[s] + term
                for s in range(n_ib):
                    erows = pl.ds(pl.multiple_of(ibs[s] * PEER_NKEYS, PEER_NKEYS), PEER_NKEYS)
                    a = act_cur_ref[erows, lanes]
                    ge = a * (0.5 + 0.5 * lax.erf(a * sqrt_half))
                    gt_cur_ref[erows, lanes] = ge.astype(BF16) * ws[s]
            acc_ref[rows, :] += jnp.dot(_unpack_bf16(vt_prev_ref[prows, :]), gt_prev_ref[...],
                                        preferred_element_type=F32)
            return carry

        lax.fori_loop(0, te // PEER_CHUNK_ROWS, chunk, 0)

    phase(u_a_ref, act_b_ref, act_a_ref, gt_a_ref, vt_a_ref, gt_b_ref, 0)
    phase(u_b_ref, act_a_ref, act_b_ref, gt_b_ref, vt_b_ref, gt_a_ref, n_sub)

    @pl.when(j == pl.num_programs(1) - 1)
    def _():
        acc = acc_ref[...] + jnp.dot(_unpack_bf16(vt_last_ref[...]), gt_b_ref[...],
                                     preferred_element_type=F32)
        z = alpha * h_ref[...] + acc.T
        o_ref[...] = _layer_norm(z, g_ref[...], b_ref[...])


def _peer_dense_call(h, r1, c1, n2, e2, u, vt, g, b, alpha, tt, te):
    T, D = h.shape
    n_sub = te // PEER_NKEYS
    n_blk = PEER_EXPERTS // te
    n_steps = n_blk // 2
    assert D == te and n_blk % 2 == 0
    spec1 = pl.BlockSpec((PEER_HEADS, 2 * n_sub, tt), lambda i, j: (0, j, i))
    spec2 = pl.BlockSpec((PEER_HEADS, PEER_NKEYS, tt), lambda i, j: (0, 0, i))
    last = n_blk - 1
    u_spec = lambda f: pl.BlockSpec((te // 2, D), lambda i, j: (f(j), 0))
    vt_spec = lambda f: pl.BlockSpec((D // 2, te), lambda i, j: (0, f(j)))
    return pl.pallas_call(
        functools.partial(_peer_dense_kernel, alpha=alpha),
        grid=(T // tt, n_steps),
        in_specs=[pl.BlockSpec((tt, D), lambda i, j: (i, 0)),
                  spec1, spec1, spec2, spec2,
                  u_spec(lambda j: 0),
                  u_spec(lambda j: 2 * j + 1),
                  u_spec(lambda j: jnp.minimum(2 * j + 2, last)),
                  vt_spec(lambda j: jnp.maximum(2 * j - 1, 0)),
                  vt_spec(lambda j: 2 * j),
                  vt_spec(lambda j: last),
                  pl.BlockSpec((1, D), lambda i, j: (0, 0)),
                  pl.BlockSpec((1, D), lambda i, j: (0, 0))],
        out_specs=pl.BlockSpec((tt, D), lambda i, j: (i, 0)),
        out_shape=jax.ShapeDtypeStruct((T, D), F32),
        scratch_shapes=[pltpu.VMEM((D, tt), BF16), pltpu.VMEM((D, tt), F32),
                        pltpu.VMEM((PEER_HEADS, PEER_NKEYS, tt), BF16),
                        pltpu.VMEM((PEER_HEADS, PEER_NKEYS, tt), BF16),
                        pltpu.VMEM((te, tt), F32), pltpu.VMEM((te, tt), F32),
                        pltpu.VMEM((te, tt), BF16), pltpu.VMEM((te, tt), BF16)],
        compiler_params=_params("parallel", "arbitrary"),
        name="peer_dense",
    )(h, r1, c1, n2, e2, u, u, u, vt, vt, vt, g, b)


def _pair_perm():
    idx = []
    for i in range(N_A_HEADS // 2):
        for grp in range(2):
            hd = i + (N_A_HEADS // 2) * grp
            idx.extend(range(hd * HEAD_DIM, (hd + 1) * HEAD_DIM))
    return np.asarray(idx, dtype=np.int32)


def _rope_tables(S):
    pos = np.arange(S)
    row = (pos // GRID_W).astype(np.float32)
    col = (pos % GRID_W).astype(np.float32)
    inv_freq = jnp.asarray(ROPE_THETA, F32) ** (-jnp.arange(ROPE_FREQS, dtype=F32) / ROPE_FREQS)
    ang_r = jnp.asarray(row)[:, None] * inv_freq
    ang_c = jnp.asarray(col)[:, None] * inv_freq
    cr, sr, cc, sc = jnp.cos(ang_r), jnp.sin(ang_r), jnp.cos(ang_c), jnp.sin(ang_c)
    zero = jnp.zeros_like(sr)
    cos = jnp.concatenate([cr, cr, cc, cc], axis=-1)
    sa = jnp.concatenate([-sr, zero, -sc, zero], axis=-1)
    sb = jnp.concatenate([zero, sr, zero, sc], axis=-1)
    tile2 = lambda t: jnp.concatenate([t, t], axis=-1)
    return tile2(cos), tile2(sa), tile2(sb)


def kernel(x, ln_in_g, ln_in_b, w_in, qn_g, kn_g, sink, gn_a_g, gn_b_g, w_o, ln1_g, ln1_b,
           peer_wq, peer_keys, peer_u, peer_v, ln2_g, ln2_b):
    B, S, D = x.shape
    depth = w_in.shape[0]
    T = B * S
    alpha = float((2.0 * depth) ** 0.25)
    tm = min(512, S)
    tq = min(256, S)
    tt_route = 128
    tt_dense = min(512, T)
    te = 1024

    perm = _pair_perm()
    col_perm = np.concatenate([perm, Q_W + np.arange(2 * KV_W, dtype=np.int32),
                               Q_W + 2 * KV_W + perm,
                               2 * Q_W + 2 * KV_W + np.arange(2 * KV_W, dtype=np.int32)])
    cos, sa, sb = _rope_tables(S)
    row2 = lambda v: v.reshape(1, -1)
    tile2 = lambda v: jnp.concatenate([v, v]).reshape(1, -1)

    h = _ln_call(x.reshape(T, D), row2(ln_in_g), row2(ln_in_b), tm)
    for l in range(depth):
        w = w_in[l][:, col_perm].astype(BF16)
        qa, ka, va, qb, kb, vb = _proj_call(h, w, tile2(qn_g[l]), tile2(kn_g[l]), cos, sa, sb, S, tm)
        oa = _attn_a_call(qa, ka, va, row2(gn_a_g[l][perm]), B, S, tq)
        ob = _attn_b_call(sink[l], qb, kb, vb, row2(gn_b_g[l][perm]), B, S)
        wa = w_o[l][:Q_W][perm].astype(BF16)
        wb = w_o[l][Q_W:][perm].astype(BF16)
        h = _oproj_call(oa, ob, wa, wb, h, row2(ln1_g[l]), row2(ln1_b[l]), alpha, tm)
        keys = peer_keys[l].reshape(2 * PEER_HEADS, PEER_NKEYS, PEER_DKEY // 2).astype(BF16)
        st = _peer_scores_call(h, peer_wq[l].astype(BF16), keys, tm)
        r1, c1, n2, e2 = _router_call(st, tt_route)
        h = _peer_dense_call(h, r1, c1, n2, e2, _pack_bf16_rows(peer_u[l]), _pack_bf16_rows(peer_v[l].T),
                             row2(ln2_g[l]), row2(ln2_b[l]), alpha, tt_dense, te)
    return h.reshape(B, S, D)
```

```python
import functools

import numpy as np
import jax
import jax.numpy as jnp
from jax import lax
from jax.experimental import pallas as pl
from jax.experimental.pallas import tpu as pltpu

D_MODEL = 1024
HEAD_DIM = 64
N_A_HEADS = 8
N_B_HEADS = 8
LANES = 128
PAIR_W = 2 * HEAD_DIM
Q_W = N_A_HEADS * HEAD_DIM
KV_W = 2 * HEAD_DIM
IN_WIDTH = 2 * (Q_W + 2 * KV_W)
BLOCK = 128
WINDOW = 128
GRID_W = 64
ROPE_THETA = 10000.0
ROPE_FREQS = HEAD_DIM // 4

PEER_HEADS = 8
PEER_NKEYS = 128
PEER_EXPERTS = PEER_NKEYS * PEER_NKEYS
PEER_DKEY = 256
PEER_TOPK = 16

LN_EPS = 1e-5
RMS_EPS = 1e-6
NEG_INF = -1e30

VMEM_LIMIT_BYTES = 56 * 1024 * 1024

BF16 = jnp.bfloat16
F32 = jnp.float32


def _alibi_slopes(n):
    return [float(2.0 ** (-((i + 1) * 8.0 / n))) for i in range(n)]


def _params(*sem):
    return pltpu.CompilerParams(dimension_semantics=sem,
                                vmem_limit_bytes=VMEM_LIMIT_BYTES)


def _pack_bf16_rows(w):
    r, c = w.shape
    pairs = w.astype(BF16).reshape(r // 2, 2, c).swapaxes(1, 2)
    return lax.bitcast_convert_type(pairs, jnp.uint32)


def _unpack_bf16(words):
    return pltpu.bitcast(words, BF16)


def _layer_norm(z, g, b):
    mu = jnp.mean(z, axis=-1, keepdims=True)
    zc = z - mu
    var = jnp.mean(zc * zc, axis=-1, keepdims=True)
    return zc * lax.rsqrt(var + LN_EPS) * g + b


def _half_rms(t, lo):
    sq = t * t
    s_lo = jnp.sum(jnp.where(lo, sq, 0.0), axis=-1, keepdims=True)
    s_hi = jnp.sum(jnp.where(lo, 0.0, sq), axis=-1, keepdims=True)
    return jnp.where(lo, lax.rsqrt(s_lo / HEAD_DIM + RMS_EPS),
                     lax.rsqrt(s_hi / HEAD_DIM + RMS_EPS))


def _ln_kernel(x_ref, g_ref, b_ref, o_ref):
    o_ref[...] = _layer_norm(x_ref[...], g_ref[...], b_ref[...])


def _ln_call(x, g, b, tm):
    T, D = x.shape
    return pl.pallas_call(
        _ln_kernel,
        grid=(T // tm,),
        in_specs=[pl.BlockSpec((tm, D), lambda i: (i, 0)),
                  pl.BlockSpec((1, D), lambda i: (0, 0)),
                  pl.BlockSpec((1, D), lambda i: (0, 0))],
        out_specs=pl.BlockSpec((tm, D), lambda i: (i, 0)),
        out_shape=jax.ShapeDtypeStruct((T, D), F32),
        compiler_params=_params("parallel"),
        name="ln_in",
    )(x, g, b)


def _proj_kernel(h_ref, w_ref, qg_ref, kg_ref, cos_ref, sa_ref, sb_ref,
                 qa_ref, ka_ref, va_ref, qb_ref, kb_ref, vb_ref):
    tm = h_ref.shape[0]
    y = jnp.dot(h_ref[...].astype(BF16), w_ref[...], preferred_element_type=F32)
    lo = lax.broadcasted_iota(jnp.int32, (tm, PAIR_W), 1) < HEAD_DIM
    cos, sa, sb = cos_ref[...], sa_ref[...], sb_ref[...]
    scale = HEAD_DIM ** -0.5

    def norm_rope(t, g):
        xn = t * _half_rms(t, lo) * g
        return xn * cos + pltpu.roll(xn, PAIR_W - ROPE_FREQS, 1) * sa + pltpu.roll(xn, ROPE_FREQS, 1) * sb

    for i in range(Q_W // PAIR_W):
        t = y[:, i * PAIR_W:(i + 1) * PAIR_W]
        qa_ref[:, i * PAIR_W:(i + 1) * PAIR_W] = (norm_rope(t, qg_ref[...]) * scale).astype(BF16)
    off = Q_W
    ka_ref[...] = norm_rope(y[:, off:off + KV_W], kg_ref[...]).astype(BF16)
    off += KV_W
    va_ref[...] = y[:, off:off + KV_W].astype(BF16)
    off += KV_W
    qb_ref[...] = (y[:, off:off + Q_W] * scale).astype(BF16)
    off += Q_W
    kb_ref[...] = y[:, off:off + KV_W].astype(BF16)
    off += KV_W
    vb_ref[...] = y[:, off:off + KV_W].astype(BF16)


def _proj_call(h, w, qg, kg, cos, sa, sb, S, tm):
    T, D = h.shape
    n_pos = S // tm
    row = lambda i: (i, 0)
    const = lambda i: (0, 0)
    pos = lambda i: (i % n_pos, 0)
    outs = [jax.ShapeDtypeStruct((T, w_), BF16) for w_ in (Q_W, KV_W, KV_W, Q_W, KV_W, KV_W)]
    return pl.pallas_call(
        _proj_kernel,
        grid=(T // tm,),
        in_specs=[pl.BlockSpec((tm, D), row),
                  pl.BlockSpec((D, IN_WIDTH), const),
                  pl.BlockSpec((1, PAIR_W), const),
                  pl.BlockSpec((1, PAIR_W), const),
                  pl.BlockSpec((tm, PAIR_W), pos),
                  pl.BlockSpec((tm, PAIR_W), pos),
                  pl.BlockSpec((tm, PAIR_W), pos)],
        out_specs=[pl.BlockSpec((tm, o.shape[1]), row) for o in outs],
        out_shape=outs,
        compiler_params=_params("parallel"),
        name="in_proj",
    )(h, w, qg, kg, cos, sa, sb)


def _attn_a_kernel(q_ref, k_ref, v_ref, g_ref, o_ref):
    tq = q_ref.shape[0]
    k = k_ref[...]
    v = v_ref[...]
    lo = lax.broadcasted_iota(jnp.int32, (tq, PAIR_W), 1) < HEAD_DIM
    nt = (((1,), (1,)), ((), ()))
    for i in range(Q_W // PAIR_W):
        qp = q_ref[:, i * PAIR_W:(i + 1) * PAIR_W]
        halves = []
        for grp in range(2):
            qz = jnp.where(lo if grp == 0 else jnp.logical_not(lo), qp, jnp.zeros_like(qp))
            s = lax.dot_general(qz, k, nt, preferred_element_type=F32)
            m = jnp.max(s, axis=-1, keepdims=True)
            p = jnp.exp(s - m)
            l = jnp.sum(p, axis=-1, keepdims=True)
            halves.append(jnp.dot(p.astype(BF16), v, preferred_element_type=F32) / l)
        o = jnp.where(lo, halves[0], halves[1])
        o = o * _half_rms(o, lo) * g_ref[:, i * PAIR_W:(i + 1) * PAIR_W]
        o_ref[:, i * PAIR_W:(i + 1) * PAIR_W] = o.astype(BF16)


def _attn_a_call(q, k, v, g, B, S, tq):
    T = q.shape[0]
    nq = S // tq
    return pl.pallas_call(
        _attn_a_kernel,
        grid=(B, nq),
        in_specs=[pl.BlockSpec((tq, Q_W), lambda b, i: (b * nq + i, 0)),
                  pl.BlockSpec((S, KV_W), lambda b, i: (b, 0)),
                  pl.BlockSpec((S, KV_W), lambda b, i: (b, 0)),
                  pl.BlockSpec((1, Q_W), lambda b, i: (0, 0))],
        out_specs=pl.BlockSpec((tq, Q_W), lambda b, i: (b * nq + i, 0)),
        out_shape=jax.ShapeDtypeStruct((T, Q_W), BF16),
        compiler_params=_params("parallel", "parallel"),
        name="attn_global",
    )(q, k, v, g)


def _attn_b_kernel(sink_ref, q_ref, k_ref, v_ref, g_ref, o_ref, *, slopes):
    j = pl.program_id(1)
    nb = k_ref.shape[0] // BLOCK
    span = BLOCK + 2 * WINDOW
    ks, vs = [], []
    for c in (-1, 0, 1):
        jc = j + c
        start = pl.multiple_of(jnp.clip(jc, 0, nb - 1) * BLOCK, BLOCK)
        ks.append(k_ref[pl.ds(start, BLOCK), :])
        vs.append(v_ref[pl.ds(start, BLOCK), :])
    kslab = jnp.concatenate(ks, axis=0)
    vslab = jnp.concatenate(vs, axis=0)
    qi = lax.broadcasted_iota(jnp.int32, (BLOCK, span), 0)
    ki = lax.broadcasted_iota(jnp.int32, (BLOCK, span), 1)
    dist = jnp.abs(qi + WINDOW - ki)
    s_pos = (j - 1) * BLOCK + ki
    in_seq = jnp.logical_and(s_pos >= 0, s_pos < nb * BLOCK)
    valid = jnp.logical_and(dist <= WINDOW, in_seq)
    distf = dist.astype(F32)

    lo = lax.broadcasted_iota(jnp.int32, (BLOCK, PAIR_W), 1) < HEAD_DIM
    hi = jnp.logical_not(lo)
    n_pairs = Q_W // PAIR_W
    heads = [(i, grp) for grp in range(2) for i in range(n_pairs)]
    qs = []
    for i, grp in heads:
        qp = q_ref[:, i * PAIR_W:(i + 1) * PAIR_W]
        qs.append(jnp.where(lo if grp == 0 else hi, qp, jnp.zeros_like(qp)))
    s_all = lax.dot_general(jnp.concatenate(qs, axis=0), kslab, (((1,), (1,)), ((), ())),
                            preferred_element_type=F32)
    ps, inv = [], []
    for n, (i, grp) in enumerate(heads):
        hd = i + n_pairs * grp
        s = s_all[n * BLOCK:(n + 1) * BLOCK] - slopes[hd] * distf
        s = jnp.where(valid, s, NEG_INF)
        sink = sink_ref[hd]
        m = jnp.maximum(jnp.max(s, axis=-1, keepdims=True), sink)
        p = jnp.exp(s - m)
        inv.append(1.0 / (jnp.sum(p, axis=-1, keepdims=True) + jnp.exp(sink - m)))
        ps.append(p.astype(BF16))
    o_all = jnp.dot(jnp.concatenate(ps, axis=0), vslab, preferred_element_type=F32)
    for i in range(n_pairs):
        o0 = o_all[i * BLOCK:(i + 1) * BLOCK] * inv[i]
        o1 = o_all[(i + n_pairs) * BLOCK:(i + n_pairs + 1) * BLOCK] * inv[i + n_pairs]
        o = jnp.where(lo, o0, o1)
        o = o * _half_rms(o, lo) * g_ref[:, i * PAIR_W:(i + 1) * PAIR_W]
        o_ref[:, i * PAIR_W:(i + 1) * PAIR_W] = o.astype(BF16)


def _attn_b_call(sink, q, k, v, g, B, S):
    T = q.shape[0]
    nb = S // BLOCK
    kern = functools.partial(_attn_b_kernel, slopes=_alibi_slopes(N_B_HEADS))
    return pl.pallas_call(
        kern,
        grid=(B, nb),
        in_specs=[pl.BlockSpec(memory_space=pltpu.SMEM),
                  pl.BlockSpec((BLOCK, Q_W), lambda b, i: (b * nb + i, 0)),
                  pl.BlockSpec((S, KV_W), lambda b, i: (b, 0)),
                  pl.BlockSpec((S, KV_W), lambda b, i: (b, 0)),
                  pl.BlockSpec((1, Q_W), lambda b, i: (0, 0))],
        out_specs=pl.BlockSpec((BLOCK, Q_W), lambda b, i: (b * nb + i, 0)),
        out_shape=jax.ShapeDtypeStruct((T, Q_W), BF16),
        compiler_params=_params("parallel", "parallel"),
        name="attn_window",
    )(sink, q, k, v, g)


def _oproj_kernel(oa_ref, ob_ref, wa_ref, wb_ref, h_ref, g_ref, b_ref, o_ref, *, alpha):
    mix = jnp.dot(oa_ref[...], wa_ref[...], preferred_element_type=F32)
    mix = mix + jnp.dot(ob_ref[...], wb_ref[...], preferred_element_type=F32)
    o_ref[...] = _layer_norm(alpha * h_ref[...] + mix, g_ref[...], b_ref[...])


def _oproj_call(oa, ob, wa, wb, h, g, b, alpha, tm):
    T, D = h.shape
    row = lambda i: (i, 0)
    const = lambda i: (0, 0)
    return pl.pallas_call(
        functools.partial(_oproj_kernel, alpha=alpha),
        grid=(T // tm,),
        in_specs=[pl.BlockSpec((tm, Q_W), row), pl.BlockSpec((tm, Q_W), row),
                  pl.BlockSpec((Q_W, D), const), pl.BlockSpec((Q_W, D), const),
                  pl.BlockSpec((tm, D), row),
                  pl.BlockSpec((1, D), const), pl.BlockSpec((1, D), const)],
        out_specs=pl.BlockSpec((tm, D), row),
        out_shape=jax.ShapeDtypeStruct((T, D), F32),
        compiler_params=_params("parallel"),
        name="out_proj_ln",
    )(oa, ob, wa, wb, h, g, b)


def _peer_scores_kernel(h_ref, wq_ref, keys_ref, st_ref):
    q = jnp.dot(h_ref[...].astype(BF16), wq_ref[...], preferred_element_type=F32).astype(BF16)
    half = PEER_DKEY // 2
    for n in range(2 * PEER_HEADS):
        qs = q[:, n * half:(n + 1) * half]
        st_ref[n] = lax.dot_general(keys_ref[n], qs, (((1,), (1,)), ((), ())),
                                    preferred_element_type=F32)


def _peer_scores_call(h, wq, keys, tm):
    T, D = h.shape
    n_sets = 2 * PEER_HEADS
    half = PEER_DKEY // 2
    return pl.pallas_call(
        _peer_scores_kernel,
        grid=(T // tm,),
        in_specs=[pl.BlockSpec((tm, D), lambda i: (i, 0)),
                  pl.BlockSpec((D, PEER_HEADS * PEER_DKEY), lambda i: (0, 0)),
                  pl.BlockSpec((n_sets, PEER_NKEYS, half), lambda i: (0, 0, 0))],
        out_specs=pl.BlockSpec((n_sets, PEER_NKEYS, tm), lambda i: (0, 0, i)),
        out_shape=jax.ShapeDtypeStruct((n_sets, PEER_NKEYS, T), F32),
        compiler_params=_params("parallel"),
        name="peer_scores",
    )(h, wq, keys)


def _router_kernel(st_ref, r1_ref, c1_ref, n2_ref, e2_ref):
    tt = st_ref.shape[-1]
    K = PEER_TOPK
    n_lead = 4
    neg = -jnp.inf
    kio_i = lax.broadcasted_iota(jnp.int32, (PEER_NKEYS, tt), 0)
    rio = lax.broadcasted_iota(jnp.int32, (K, tt), 0)
    blk = kio_i // K
    rr = kio_i % K
    pos = jnp.where(blk < n_lead, rr * K + blk, (blk - n_lead) * K + rr).astype(F32)
    dup = jnp.logical_and(blk >= n_lead, rr < n_lead)
    kio = kio_i.astype(F32)

    def top16(s, exact_ties):
        rank = jnp.full((PEER_NKEYS, tt), float(K), F32)
        arr = jnp.zeros((K, tt), F32)
        for r in range(K):
            m = jnp.max(s, axis=0, keepdims=True)
            hit = s == m
            if exact_ties:
                first = jnp.min(jnp.where(hit, kio, float(PEER_NKEYS)), axis=0, keepdims=True)
                hit = kio == first
            rank = jnp.where(hit, float(r), rank)
            s = jnp.where(hit, neg, s)
            arr = jnp.where(rio == r, m, arr)
        return rank, arr

    def route(h, exact_ties):
        rank1, a1 = top16(st_ref[2 * h], exact_ties)
        rank2, a2 = top16(st_ref[2 * h + 1], exact_ties)
        blocks = ([a1 + a2[r2:r2 + 1] for r2 in range(n_lead)]
                  + [a1[r1:r1 + 1] + a2 for r1 in range(n_lead)])
        c0 = jnp.where(dup, neg, jnp.concatenate(blocks, axis=0))
        c = c0
        sel = jnp.zeros((PEER_NKEYS, tt), F32)
        for _ in range(K):
            m = jnp.max(c, axis=0, keepdims=True)
            hit = c == m
            if exact_ties:
                first = jnp.min(jnp.where(hit, pos, float(K * K)), axis=0, keepdims=True)
                hit = pos == first
            sel = jnp.where(hit, 1.0, sel)
            c = jnp.where(hit, neg, c)
        return rank1, a1, rank2, a2, c0, sel

    def count_is_k(flags):
        return jnp.sum(flags, axis=0, keepdims=True) == float(K)

    def head(h, carry):
        fast = route(h, False)
        rank1, _, rank2, _, _, sel = fast
        one = jnp.ones((PEER_NKEYS, tt), F32)
        zero = jnp.zeros((PEER_NKEYS, tt), F32)
        ok = jnp.logical_and(
            jnp.logical_and(count_is_k(jnp.where(rank1 < float(K), one, zero)),
                            count_is_k(jnp.where(rank2 < float(K), one, zero))),
            count_is_k(sel))
        n_bad = jnp.sum(jnp.where(ok, 0.0, 1.0))
        rank1, a1, rank2, a2, c0, sel = lax.cond(n_bad > 0.0, lambda: route(h, True), lambda: fast)
        s1 = st_ref[2 * h]
        s2 = st_ref[2 * h + 1]
        a1r = [a1[r:r + 1] for r in range(1)]
        a2r = [a2[r:r + 1] for r in range(1)]
        top = a1r[0] + a2r[0]
        z = jnp.sum(jnp.where(sel > 0.0, jnp.exp(c0 - top), 0.0), axis=0, keepdims=True)
        tail = sel[n_lead * K:(n_lead + 1) * K]
        for r1 in range(1, n_lead):
            tail = tail + sel[(n_lead + r1) * K:(n_lead + r1 + 1) * K]
        n2 = jnp.zeros((PEER_NKEYS, tt), F32)
        for r2 in range(K):
            if r2 < n_lead:
                cnt = jnp.sum(sel[r2 * K:(r2 + 1) * K], axis=0, keepdims=True)
            else:
                cnt = tail[r2:r2 + 1]
            n2 = jnp.where(rank2 == float(r2), cnt, n2)
        r1_ref[h] = rank1
        c1_ref[h] = jnp.exp(s1 - a1r[0]) / z
        n2_ref[h] = n2
        e2_ref[h] = jnp.exp(s2 - a2r[0])
        return carry

    lax.fori_loop(0, PEER_HEADS, head, 0)


def _router_call(st, tt):
    T = st.shape[-1]
    shape = (PEER_HEADS, PEER_NKEYS, T)
    outs = [jax.ShapeDtypeStruct(shape, F32)] * 4
    spec = pl.BlockSpec((PEER_HEADS, PEER_NKEYS, tt), lambda i: (0, 0, i))
    return pl.pallas_call(
        _router_kernel,
        grid=(T // tt,),
        in_specs=[pl.BlockSpec((2 * PEER_HEADS, PEER_NKEYS, tt), lambda i: (0, 0, i))],
        out_specs=[spec] * 4,
        out_shape=outs,
        compiler_params=_params("parallel"),
        name="peer_router",
    )(st)


PEER_CHUNK_ROWS = 256


def _peer_dense_kernel(h_ref, r1_ref, c1_ref, n2_ref, e2_ref, u_first_ref, u_a_ref, u_b_ref,
                       vt_a_ref, vt_b_ref, vt_last_ref, g_ref, b_ref, o_ref,
                       xt_ref, acc_ref, n2s_ref, e2s_ref, act_a_ref, act_b_ref, gt_a_ref, gt_b_ref,
                       *, alpha):
    j = pl.program_id(1)
    tt = h_ref.shape[0]
    te = vt_a_ref.shape[1]
    n_sub = te // PEER_NKEYS
    tile = (PEER_NKEYS, LANES)
    sqrt_half = float(np.sqrt(0.5))

    @pl.when(j == 0)
    def _():
        xt_ref[:, :tt] = h_ref[...].T.astype(BF16)
        acc_ref[:, :tt] = jnp.zeros((acc_ref.shape[0], tt), F32)
        n2s_ref[:, :, :tt] = n2_ref[...].astype(BF16)
        e2s_ref[:, :, :tt] = e2_ref[...].astype(BF16)
        gt_b_ref[:, :tt] = jnp.zeros((te, tt), BF16)
        act_a_ref[:, :tt] = jnp.dot(_unpack_bf16(u_first_ref[...]), xt_ref[:, :tt],
                                 preferred_element_type=F32)

    def phase(u_next_ref, act_next_ref, act_cur_ref, gt_cur_ref, vt_prev_ref, gt_prev_ref, key_off):
        def chunk(c, carry):
            rows = pl.ds(pl.multiple_of(c * PEER_CHUNK_ROWS, PEER_CHUNK_ROWS), PEER_CHUNK_ROWS)
            prows = pl.ds(pl.multiple_of(c * (PEER_CHUNK_ROWS // 2), PEER_CHUNK_ROWS // 2),
                          PEER_CHUNK_ROWS // 2)
            act_next_ref[rows, :tt] = jnp.dot(_unpack_bf16(u_next_ref[prows, :]), xt_ref[:, :tt],
                                            preferred_element_type=F32)
            n_ib = PEER_CHUNK_ROWS // PEER_NKEYS
            ibs = [c * n_ib + s for s in range(n_ib)]
            r1_rows = [[r1_ref[hd, pl.ds(key_off + ib, 1), :] for hd in range(PEER_HEADS)] for ib in ibs]
            c1_rows = [[c1_ref[hd, pl.ds(key_off + ib, 1), :] for hd in range(PEER_HEADS)] for ib in ibs]
            for tg in range(tt // LANES):
                lanes = slice(tg * LANES, (tg + 1) * LANES)
                ws = [None] * n_ib
                for hd in range(PEER_HEADS):
                    n2t = n2s_ref[hd, :, lanes]
                    e2t = e2s_ref[hd, :, lanes]
                    for s in range(n_ib):
                        r1 = jnp.broadcast_to(r1_rows[s][hd][:, lanes], tile).astype(BF16)
                        c1 = jnp.broadcast_to(c1_rows[s][hd][:, lanes], tile).astype(BF16)
                        term = jnp.where(r1 < n2t, e2t, jnp.zeros(tile, BF16)) * c1
                        ws[s] = term if ws[s] is None else ws[s] + term
                for s in range(n_ib):
                    erows = pl.ds(pl.multiple_of(ibs[s] * PEER_NKEYS, PEER_NKEYS), PEER_NKEYS)
                    a = act_cur_ref[erows, lanes]
                    ge = a * (0.5 + 0.5 * lax.erf(a * sqrt_half))
                    gt_cur_ref[erows, lanes] = ge.astype(BF16) * ws[s]
            acc_ref[rows, :tt] += jnp.dot(_unpack_bf16(vt_prev_ref[prows, :]), gt_prev_ref[:, :tt],
                                        preferred_element_type=F32)
            return carry

        lax.fori_loop(0, te // PEER_CHUNK_ROWS, chunk, 0)

    phase(u_a_ref, act_b_ref, act_a_ref, gt_a_ref, vt_a_ref, gt_b_ref, 0)
    phase(u_b_ref, act_a_ref, act_b_ref, gt_b_ref, vt_b_ref, gt_a_ref, n_sub)

    @pl.when(j == pl.num_programs(1) - 1)
    def _():
        acc = acc_ref[:, :tt] + jnp.dot(_unpack_bf16(vt_last_ref[...]), gt_b_ref[:, :tt],
                                     preferred_element_type=F32)
        z = alpha * h_ref[...] + acc.T
        o_ref[...] = _layer_norm(z, g_ref[...], b_ref[...])


def _peer_dense_call(h, r1, c1, n2, e2, u, vt, g, b, alpha, tt, te):
    T, D = h.shape
    n_sub = te // PEER_NKEYS
    n_blk = PEER_EXPERTS // te
    n_steps = n_blk // 2
    assert D == te and n_blk % 2 == 0
    spec1 = pl.BlockSpec((PEER_HEADS, 2 * n_sub, tt), lambda i, j: (0, j, i))
    spec2 = pl.BlockSpec((PEER_HEADS, PEER_NKEYS, tt), lambda i, j: (0, 0, i))
    last = n_blk - 1
    ttp = tt + LANES
    u_spec = lambda f: pl.BlockSpec((te // 2, D), lambda i, j: (f(j), 0))
    vt_spec = lambda f: pl.BlockSpec((D // 2, te), lambda i, j: (0, f(j)))
    return pl.pallas_call(
        functools.partial(_peer_dense_kernel, alpha=alpha),
        grid=(T // tt, n_steps),
        in_specs=[pl.BlockSpec((tt, D), lambda i, j: (i, 0)),
                  spec1, spec1, spec2, spec2,
                  u_spec(lambda j: 0),
                  u_spec(lambda j: 2 * j + 1),
                  u_spec(lambda j: jnp.minimum(2 * j + 2, last)),
                  vt_spec(lambda j: jnp.maximum(2 * j - 1, 0)),
                  vt_spec(lambda j: 2 * j),
                  vt_spec(lambda j: last),
                  pl.BlockSpec((1, D), lambda i, j: (0, 0)),
                  pl.BlockSpec((1, D), lambda i, j: (0, 0))],
        out_specs=pl.BlockSpec((tt, D), lambda i, j: (i, 0)),
        out_shape=jax.ShapeDtypeStruct((T, D), F32),
        scratch_shapes=[pltpu.VMEM((D, ttp), BF16), pltpu.VMEM((D, ttp), F32),
                        pltpu.VMEM((PEER_HEADS, PEER_NKEYS, ttp), BF16),
                        pltpu.VMEM((PEER_HEADS, PEER_NKEYS, ttp), BF16),
                        pltpu.VMEM((te, ttp), F32), pltpu.VMEM((te, ttp), F32),
                        pltpu.VMEM((te, ttp), BF16), pltpu.VMEM((te, ttp), BF16)],
        compiler_params=_params("parallel", "arbitrary"),
        name="peer_dense",
    )(h, r1, c1, n2, e2, u, u, u, vt, vt, vt, g, b)


def _pair_perm():
    idx = []
    for i in range(N_A_HEADS // 2):
        for grp in range(2):
            hd = i + (N_A_HEADS // 2) * grp
            idx.extend(range(hd * HEAD_DIM, (hd + 1) * HEAD_DIM))
    return np.asarray(idx, dtype=np.int32)


def _rope_tables(S):
    pos = np.arange(S)
    row = (pos // GRID_W).astype(np.float32)
    col = (pos % GRID_W).astype(np.float32)
    inv_freq = jnp.asarray(ROPE_THETA, F32) ** (-jnp.arange(ROPE_FREQS, dtype=F32) / ROPE_FREQS)
    ang_r = jnp.asarray(row)[:, None] * inv_freq
    ang_c = jnp.asarray(col)[:, None] * inv_freq
    cr, sr, cc, sc = jnp.cos(ang_r), jnp.sin(ang_r), jnp.cos(ang_c), jnp.sin(ang_c)
    zero = jnp.zeros_like(sr)
    cos = jnp.concatenate([cr, cr, cc, cc], axis=-1)
    sa = jnp.concatenate([-sr, zero, -sc, zero], axis=-1)
    sb = jnp.concatenate([zero, sr, zero, sc], axis=-1)
    tile2 = lambda t: jnp.concatenate([t, t], axis=-1)
    return tile2(cos), tile2(sa), tile2(sb)


def kernel(x, ln_in_g, ln_in_b, w_in, qn_g, kn_g, sink, gn_a_g, gn_b_g, w_o, ln1_g, ln1_b,
           peer_wq, peer_keys, peer_u, peer_v, ln2_g, ln2_b):
    B, S, D = x.shape
    depth = w_in.shape[0]
    T = B * S
    alpha = float((2.0 * depth) ** 0.25)
    tm = min(512, S)
    tq = min(256, S)
    tt_route = 128
    tt_dense = min(512, T)
    te = 1024

    perm = _pair_perm()
    col_perm = np.concatenate([perm, Q_W + np.arange(2 * KV_W, dtype=np.int32),
                               Q_W + 2 * KV_W + perm,
                               2 * Q_W + 2 * KV_W + np.arange(2 * KV_W, dtype=np.int32)])
    cos, sa, sb = _rope_tables(S)
    row2 = lambda v: v.reshape(1, -1)
    tile2 = lambda v: jnp.concatenate([v, v]).reshape(1, -1)

    h = _ln_call(x.reshape(T, D), row2(ln_in_g), row2(ln_in_b), tm)
    for l in range(depth):
        w = w_in[l][:, col_perm].astype(BF16)
        qa, ka, va, qb, kb, vb = _proj_call(h, w, tile2(qn_g[l]), tile2(kn_g[l]), cos, sa, sb, S, tm)
        oa = _attn_a_call(qa, ka, va, row2(gn_a_g[l][perm]), B, S, tq)
        ob = _attn_b_call(sink[l], qb, kb, vb, row2(gn_b_g[l][perm]), B, S)
        wa = w_o[l][:Q_W][perm].astype(BF16)
        wb = w_o[l][Q_W:][perm].astype(BF16)
        h = _oproj_call(oa, ob, wa, wb, h, row2(ln1_g[l]), row2(ln1_b[l]), alpha, tm)
        keys = peer_keys[l].reshape(2 * PEER_HEADS, PEER_NKEYS, PEER_DKEY // 2).astype(BF16)
        st = _peer_scores_call(h, peer_wq[l].astype(BF16), keys, tm)
        r1, c1, n2, e2 = _router_call(st, tt_route)
        h = _peer_dense_call(h, r1, c1, n2, e2, _pack_bf16_rows(peer_u[l]), _pack_bf16_rows(peer_v[l].T),
                             row2(ln2_g[l]), row2(ln2_b[l]), alpha, tt_dense, te)
    return h.reshape(B, S, D)
```

```python
import functools

import numpy as np
import jax
import jax.numpy as jnp
from jax import lax
from jax.experimental import pallas as pl
from jax.experimental.pallas import tpu as pltpu

D_MODEL = 1024
HEAD_DIM = 64
N_A_HEADS = 8
N_B_HEADS = 8
LANES = 128
PAIR_W = 2 * HEAD_DIM
Q_W = N_A_HEADS * HEAD_DIM
KV_W = 2 * HEAD_DIM
IN_WIDTH = 2 * (Q_W + 2 * KV_W)
BLOCK = 128
WINDOW = 128
GRID_W = 64
ROPE_THETA = 10000.0
ROPE_FREQS = HEAD_DIM // 4

PEER_HEADS = 8
PEER_NKEYS = 128
PEER_EXPERTS = PEER_NKEYS * PEER_NKEYS
PEER_DKEY = 256
PEER_TOPK = 16

LN_EPS = 1e-5
RMS_EPS = 1e-6
NEG_INF = -1e30

VMEM_LIMIT_BYTES = 56 * 1024 * 1024

BF16 = jnp.bfloat16
F32 = jnp.float32


def _alibi_slopes(n):
    return [float(2.0 ** (-((i + 1) * 8.0 / n))) for i in range(n)]


def _params(*sem):
    return pltpu.CompilerParams(dimension_semantics=sem,
                                vmem_limit_bytes=VMEM_LIMIT_BYTES)


def _pack_bf16_rows(w):
    r, c = w.shape
    pairs = w.astype(BF16).reshape(r // 2, 2, c).swapaxes(1, 2)
    return lax.bitcast_convert_type(pairs, jnp.uint32)


def _unpack_bf16(words):
    return pltpu.bitcast(words, BF16)


def _layer_norm(z, g, b):
    mu = jnp.mean(z, axis=-1, keepdims=True)
    zc = z - mu
    var = jnp.mean(zc * zc, axis=-1, keepdims=True)
    return zc * lax.rsqrt(var + LN_EPS) * g + b


def _half_rms(t, lo):
    sq = t * t
    s_lo = jnp.sum(jnp.where(lo, sq, 0.0), axis=-1, keepdims=True)
    s_hi = jnp.sum(jnp.where(lo, 0.0, sq), axis=-1, keepdims=True)
    return jnp.where(lo, lax.rsqrt(s_lo / HEAD_DIM + RMS_EPS),
                     lax.rsqrt(s_hi / HEAD_DIM + RMS_EPS))


def _ln_kernel(x_ref, g_ref, b_ref, o_ref):
    o_ref[...] = _layer_norm(x_ref[...], g_ref[...], b_ref[...])


def _ln_call(x, g, b, tm):
    T, D = x.shape
    return pl.pallas_call(
        _ln_kernel,
        grid=(T // tm,),
        in_specs=[pl.BlockSpec((tm, D), lambda i: (i, 0)),
                  pl.BlockSpec((1, D), lambda i: (0, 0)),
                  pl.BlockSpec((1, D), lambda i: (0, 0))],
        out_specs=pl.BlockSpec((tm, D), lambda i: (i, 0)),
        out_shape=jax.ShapeDtypeStruct((T, D), F32),
        compiler_params=_params("parallel"),
        name="ln_in",
    )(x, g, b)


def _proj_kernel(h_ref, w_ref, qg_ref, kg_ref, cos_ref, sa_ref, sb_ref,
                 qa_ref, ka_ref, va_ref, qb_ref, kb_ref, vb_ref):
    tm = h_ref.shape[0]
    y = jnp.dot(h_ref[...].astype(BF16), w_ref[...], preferred_element_type=F32)
    lo = lax.broadcasted_iota(jnp.int32, (tm, PAIR_W), 1) < HEAD_DIM
    cos, sa, sb = cos_ref[...], sa_ref[...], sb_ref[...]
    scale = HEAD_DIM ** -0.5

    def norm_rope(t, g):
        xn = t * _half_rms(t, lo) * g
        return xn * cos + pltpu.roll(xn, PAIR_W - ROPE_FREQS, 1) * sa + pltpu.roll(xn, ROPE_FREQS, 1) * sb

    for i in range(Q_W // PAIR_W):
        t = y[:, i * PAIR_W:(i + 1) * PAIR_W]
        qa_ref[:, i * PAIR_W:(i + 1) * PAIR_W] = (norm_rope(t, qg_ref[...]) * scale).astype(BF16)
    off = Q_W
    ka_ref[...] = norm_rope(y[:, off:off + KV_W], kg_ref[...]).astype(BF16)
    off += KV_W
    va_ref[...] = y[:, off:off + KV_W].astype(BF16)
    off += KV_W
    qb_ref[...] = (y[:, off:off + Q_W] * scale).astype(BF16)
    off += Q_W
    kb_ref[...] = y[:, off:off + KV_W].astype(BF16)
    off += KV_W
    vb_ref[...] = y[:, off:off + KV_W].astype(BF16)


def _proj_call(h, w, qg, kg, cos, sa, sb, S, tm):
    T, D = h.shape
    n_pos = S // tm
    row = lambda i: (i, 0)
    const = lambda i: (0, 0)
    pos = lambda i: (i % n_pos, 0)
    outs = [jax.ShapeDtypeStruct((T, w_), BF16) for w_ in (Q_W, KV_W, KV_W, Q_W, KV_W, KV_W)]
    return pl.pallas_call(
        _proj_kernel,
        grid=(T // tm,),
        in_specs=[pl.BlockSpec((tm, D), row),
                  pl.BlockSpec((D, IN_WIDTH), const),
                  pl.BlockSpec((1, PAIR_W), const),
                  pl.BlockSpec((1, PAIR_W), const),
                  pl.BlockSpec((tm, PAIR_W), pos),
                  pl.BlockSpec((tm, PAIR_W), pos),
                  pl.BlockSpec((tm, PAIR_W), pos)],
        out_specs=[pl.BlockSpec((tm, o.shape[1]), row) for o in outs],
        out_shape=outs,
        compiler_params=_params("parallel"),
        name="in_proj",
    )(h, w, qg, kg, cos, sa, sb)


def _attn_a_kernel(q_ref, k_ref, v_ref, g_ref, o_ref):
    tq = q_ref.shape[0]
    k = k_ref[...]
    v = v_ref[...]
    lo = lax.broadcasted_iota(jnp.int32, (tq, PAIR_W), 1) < HEAD_DIM
    nt = (((1,), (1,)), ((), ()))
    for i in range(Q_W // PAIR_W):
        qp = q_ref[:, i * PAIR_W:(i + 1) * PAIR_W]
        halves = []
        for grp in range(2):
            qz = jnp.where(lo if grp == 0 else jnp.logical_not(lo), qp, jnp.zeros_like(qp))
            s = lax.dot_general(qz, k, nt, preferred_element_type=F32)
            m = jnp.max(s, axis=-1, keepdims=True)
            p = jnp.exp(s - m)
            l = jnp.sum(p, axis=-1, keepdims=True)
            halves.append(jnp.dot(p.astype(BF16), v, preferred_element_type=F32) / l)
        o = jnp.where(lo, halves[0], halves[1])
        o = o * _half_rms(o, lo) * g_ref[:, i * PAIR_W:(i + 1) * PAIR_W]
        o_ref[:, i * PAIR_W:(i + 1) * PAIR_W] = o.astype(BF16)


def _attn_a_call(q, k, v, g, B, S, tq):
    T = q.shape[0]
    nq = S // tq
    return pl.pallas_call(
        _attn_a_kernel,
        grid=(B, nq),
        in_specs=[pl.BlockSpec((tq, Q_W), lambda b, i: (b * nq + i, 0)),
                  pl.BlockSpec((S, KV_W), lambda b, i: (b, 0)),
                  pl.BlockSpec((S, KV_W), lambda b, i: (b, 0)),
                  pl.BlockSpec((1, Q_W), lambda b, i: (0, 0))],
        out_specs=pl.BlockSpec((tq, Q_W), lambda b, i: (b * nq + i, 0)),
        out_shape=jax.ShapeDtypeStruct((T, Q_W), BF16),
        compiler_params=_params("parallel", "parallel"),
        name="attn_global",
    )(q, k, v, g)


def _attn_b_kernel(sink_ref, q_ref, k_ref, v_ref, g_ref, o_ref, *, slopes):
    j = pl.program_id(1)
    nb = k_ref.shape[0] // BLOCK
    span = BLOCK + 2 * WINDOW
    ks, vs = [], []
    for c in (-1, 0, 1):
        jc = j + c
        start = pl.multiple_of(jnp.clip(jc, 0, nb - 1) * BLOCK, BLOCK)
        ks.append(k_ref[pl.ds(start, BLOCK), :])
        vs.append(v_ref[pl.ds(start, BLOCK), :])
    kslab = jnp.concatenate(ks, axis=0)
    vslab = jnp.concatenate(vs, axis=0)
    qi = lax.broadcasted_iota(jnp.int32, (BLOCK, span), 0)
    ki = lax.broadcasted_iota(jnp.int32, (BLOCK, span), 1)
    dist = jnp.abs(qi + WINDOW - ki)
    s_pos = (j - 1) * BLOCK + ki
    in_seq = jnp.logical_and(s_pos >= 0, s_pos < nb * BLOCK)
    valid = jnp.logical_and(dist <= WINDOW, in_seq)
    distf = dist.astype(F32)

    lo = lax.broadcasted_iota(jnp.int32, (BLOCK, PAIR_W), 1) < HEAD_DIM
    hi = jnp.logical_not(lo)
    n_pairs = Q_W // PAIR_W
    heads = [(i, grp) for grp in range(2) for i in range(n_pairs)]
    qs = []
    for i, grp in heads:
        qp = q_ref[:, i * PAIR_W:(i + 1) * PAIR_W]
        qs.append(jnp.where(lo if grp == 0 else hi, qp, jnp.zeros_like(qp)))
    s_all = lax.dot_general(jnp.concatenate(qs, axis=0), kslab, (((1,), (1,)), ((), ())),
                            preferred_element_type=F32)
    ps, inv = [], []
    for n, (i, grp) in enumerate(heads):
        hd = i + n_pairs * grp
        s = s_all[n * BLOCK:(n + 1) * BLOCK] - slopes[hd] * distf
        s = jnp.where(valid, s, NEG_INF)
        sink = sink_ref[hd]
        m = jnp.maximum(jnp.max(s, axis=-1, keepdims=True), sink)
        p = jnp.exp(s - m)
        inv.append(1.0 / (jnp.sum(p, axis=-1, keepdims=True) + jnp.exp(sink - m)))
        ps.append(p.astype(BF16))
    o_all = jnp.dot(jnp.concatenate(ps, axis=0), vslab, preferred_element_type=F32)
    for i in range(n_pairs):
        o0 = o_all[i * BLOCK:(i + 1) * BLOCK] * inv[i]
        o1 = o_all[(i + n_pairs) * BLOCK:(i + n_pairs + 1) * BLOCK] * inv[i + n_pairs]
        o = jnp.where(lo, o0, o1)
        o = o * _half_rms(o, lo) * g_ref[:, i * PAIR_W:(i + 1) * PAIR_W]
        o_ref[:, i * PAIR_W:(i + 1) * PAIR_W] = o.astype(BF16)


def _attn_b_call(sink, q, k, v, g, B, S):
    T = q.shape[0]
    nb = S // BLOCK
    kern = functools.partial(_attn_b_kernel, slopes=_alibi_slopes(N_B_HEADS))
    return pl.pallas_call(
        kern,
        grid=(B, nb),
        in_specs=[pl.BlockSpec(memory_space=pltpu.SMEM),
                  pl.BlockSpec((BLOCK, Q_W), lambda b, i: (b * nb + i, 0)),
                  pl.BlockSpec((S, KV_W), lambda b, i: (b, 0)),
                  pl.BlockSpec((S, KV_W), lambda b, i: (b, 0)),
                  pl.BlockSpec((1, Q_W), lambda b, i: (0, 0))],
        out_specs=pl.BlockSpec((BLOCK, Q_W), lambda b, i: (b * nb + i, 0)),
        out_shape=jax.ShapeDtypeStruct((T, Q_W), BF16),
        compiler_params=_params("parallel", "parallel"),
        name="attn_window",
    )(sink, q, k, v, g)


def _oproj_kernel(oa_ref, ob_ref, wa_ref, wb_ref, h_ref, g_ref, b_ref, o_ref, *, alpha):
    mix = jnp.dot(oa_ref[...], wa_ref[...], preferred_element_type=F32)
    mix = mix + jnp.dot(ob_ref[...], wb_ref[...], preferred_element_type=F32)
    o_ref[...] = _layer_norm(alpha * h_ref[...] + mix, g_ref[...], b_ref[...])


def _oproj_call(oa, ob, wa, wb, h, g, b, alpha, tm):
    T, D = h.shape
    row = lambda i: (i, 0)
    const = lambda i: (0, 0)
    return pl.pallas_call(
        functools.partial(_oproj_kernel, alpha=alpha),
        grid=(T // tm,),
        in_specs=[pl.BlockSpec((tm, Q_W), row), pl.BlockSpec((tm, Q_W), row),
                  pl.BlockSpec((Q_W, D), const), pl.BlockSpec((Q_W, D), const),
                  pl.BlockSpec((tm, D), row),
                  pl.BlockSpec((1, D), const), pl.BlockSpec((1, D), const)],
        out_specs=pl.BlockSpec((tm, D), row),
        out_shape=jax.ShapeDtypeStruct((T, D), F32),
        compiler_params=_params("parallel"),
        name="out_proj_ln",
    )(oa, ob, wa, wb, h, g, b)


def _peer_scores_kernel(h_ref, wq_ref, keys_ref, st_ref):
    q = jnp.dot(h_ref[...].astype(BF16), wq_ref[...], preferred_element_type=F32).astype(BF16)
    half = PEER_DKEY // 2
    for n in range(2 * PEER_HEADS):
        qs = q[:, n * half:(n + 1) * half]
        st_ref[n] = lax.dot_general(keys_ref[n], qs, (((1,), (1,)), ((), ())),
                                    preferred_element_type=F32)


def _peer_scores_call(h, wq, keys, tm):
    T, D = h.shape
    n_sets = 2 * PEER_HEADS
    half = PEER_DKEY // 2
    return pl.pallas_call(
        _peer_scores_kernel,
        grid=(T // tm,),
        in_specs=[pl.BlockSpec((tm, D), lambda i: (i, 0)),
                  pl.BlockSpec((D, PEER_HEADS * PEER_DKEY), lambda i: (0, 0)),
                  pl.BlockSpec((n_sets, PEER_NKEYS, half), lambda i: (0, 0, 0))],
        out_specs=pl.BlockSpec((n_sets, PEER_NKEYS, tm), lambda i: (0, 0, i)),
        out_shape=jax.ShapeDtypeStruct((n_sets, PEER_NKEYS, T), F32),
        compiler_params=_params("parallel"),
        name="peer_scores",
    )(h, wq, keys)


def _router_kernel(st_ref, r1_ref, c1_ref, n2_ref, e2_ref):
    tt = st_ref.shape[-1]
    K = PEER_TOPK
    n_lead = 4
    neg = -jnp.inf
    kio_i = lax.broadcasted_iota(jnp.int32, (PEER_NKEYS, tt), 0)
    rio = lax.broadcasted_iota(jnp.int32, (K, tt), 0)
    blk = kio_i // K
    rr = kio_i % K
    pos = jnp.where(blk < n_lead, rr * K + blk, (blk - n_lead) * K + rr).astype(F32)
    dup = jnp.logical_and(blk >= n_lead, rr < n_lead)
    kio = kio_i.astype(F32)

    def top16(s, exact_ties):
        rank = jnp.full((PEER_NKEYS, tt), float(K), F32)
        arr = jnp.zeros((K, tt), F32)
        for r in range(K):
            m = jnp.max(s, axis=0, keepdims=True)
            hit = s == m
            if exact_ties:
                first = jnp.min(jnp.where(hit, kio, float(PEER_NKEYS)), axis=0, keepdims=True)
                hit = kio == first
            rank = jnp.where(hit, float(r), rank)
            s = jnp.where(hit, neg, s)
            arr = jnp.where(rio == r, m, arr)
        return rank, arr

    def route(h, exact_ties):
        rank1, a1 = top16(st_ref[2 * h], exact_ties)
        rank2, a2 = top16(st_ref[2 * h + 1], exact_ties)
        blocks = ([a1 + a2[r2:r2 + 1] for r2 in range(n_lead)]
                  + [a1[r1:r1 + 1] + a2 for r1 in range(n_lead)])
        c0 = jnp.where(dup, neg, jnp.concatenate(blocks, axis=0))
        c = c0
        sel = jnp.zeros((PEER_NKEYS, tt), F32)
        for _ in range(K):
            m = jnp.max(c, axis=0, keepdims=True)
            hit = c == m
            if exact_ties:
                first = jnp.min(jnp.where(hit, pos, float(K * K)), axis=0, keepdims=True)
                hit = pos == first
            sel = jnp.where(hit, 1.0, sel)
            c = jnp.where(hit, neg, c)
        return rank1, a1, rank2, a2, c0, sel

    def count_is_k(flags):
        return jnp.sum(flags, axis=0, keepdims=True) == float(K)

    def head(h, carry):
        fast = route(h, False)
        rank1, _, rank2, _, _, sel = fast
        one = jnp.ones((PEER_NKEYS, tt), F32)
        zero = jnp.zeros((PEER_NKEYS, tt), F32)
        ok = jnp.logical_and(
            jnp.logical_and(count_is_k(jnp.where(rank1 < float(K), one, zero)),
                            count_is_k(jnp.where(rank2 < float(K), one, zero))),
            count_is_k(sel))
        n_bad = jnp.sum(jnp.where(ok, 0.0, 1.0))
        rank1, a1, rank2, a2, c0, sel = lax.cond(n_bad > 0.0, lambda: route(h, True), lambda: fast)
        s1 = st_ref[2 * h]
        s2 = st_ref[2 * h + 1]
        a1r = [a1[r:r + 1] for r in range(1)]
        a2r = [a2[r:r + 1] for r in range(1)]
        top = a1r[0] + a2r[0]
        z = jnp.sum(jnp.where(sel > 0.0, jnp.exp(c0 - top), 0.0), axis=0, keepdims=True)
        tail = sel[n_lead * K:(n_lead + 1) * K]
        for r1 in range(1, n_lead):
            tail = tail + sel[(n_lead + r1) * K:(n_lead + r1 + 1) * K]
        n2 = jnp.zeros((PEER_NKEYS, tt), F32)
        for r2 in range(K):
            if r2 < n_lead:
                cnt = jnp.sum(sel[r2 * K:(r2 + 1) * K], axis=0, keepdims=True)
            else:
                cnt = tail[r2:r2 + 1]
            n2 = jnp.where(rank2 == float(r2), cnt, n2)
        r1_ref[h] = rank1
        c1_ref[h] = jnp.exp(s1 - a1r[0]) / z
        n2_ref[h] = n2
        e2_ref[h] = jnp.exp(s2 - a2r[0])
        return carry

    lax.fori_loop(0, PEER_HEADS, head, 0)


def _router_call(st, tt):
    T = st.shape[-1]
    shape = (PEER_HEADS, PEER_NKEYS, T)
    outs = [jax.ShapeDtypeStruct(shape, F32)] * 4
    spec = pl.BlockSpec((PEER_HEADS, PEER_NKEYS, tt), lambda i: (0, 0, i))
    return pl.pallas_call(
        _router_kernel,
        grid=(T // tt,),
        in_specs=[pl.BlockSpec((2 * PEER_HEADS, PEER_NKEYS, tt), lambda i: (0, 0, i))],
        out_specs=[spec] * 4,
        out_shape=outs,
        compiler_params=_params("parallel"),
        name="peer_router",
    )(st)


PEER_CHUNK_ROWS = 256
PEER_IB_GROUP = 2
PEER_PART_ROWS = 64


def _peer_dense_kernel(h_ref, r1_ref, c1_ref, n2_ref, e2_ref, u_first_ref, u_a_ref, u_b_ref,
                       vt_a_ref, vt_b_ref, vt_last_ref, g_ref, b_ref, o_ref,
                       xt_ref, acc_ref, n2s_ref, e2s_ref, act_a_ref, act_b_ref, gt_a_ref, gt_b_ref,
                       *, alpha):
    j = pl.program_id(1)
    tt = h_ref.shape[0]
    te = vt_a_ref.shape[1]
    n_sub = te // PEER_NKEYS
    tile = (PEER_PART_ROWS, LANES)
    sqrt_half = float(np.sqrt(0.5))

    def gelu_bf16(a):
        return (a * (0.5 + 0.5 * lax.erf(a * sqrt_half))).astype(BF16)

    @pl.when(j == 0)
    def _():
        xt_ref[:, :tt] = h_ref[...].T.astype(BF16)
        acc_ref[:, :tt] = jnp.zeros((acc_ref.shape[0], tt), F32)
        n2s_ref[:, :, :tt] = n2_ref[...].astype(BF16)
        e2s_ref[:, :, :tt] = e2_ref[...].astype(BF16)
        gt_b_ref[:, :tt] = jnp.zeros((te, tt), BF16)
        act_a_ref[:, :tt] = gelu_bf16(jnp.dot(_unpack_bf16(u_first_ref[...]), xt_ref[:, :tt],
                                              preferred_element_type=F32))

    def phase(u_next_ref, act_next_ref, act_cur_ref, gt_cur_ref, vt_prev_ref, gt_prev_ref, key_off):
        def chunk(c, carry):
            rows = pl.ds(pl.multiple_of(c * PEER_CHUNK_ROWS, PEER_CHUNK_ROWS), PEER_CHUNK_ROWS)
            prows = pl.ds(pl.multiple_of(c * (PEER_CHUNK_ROWS // 2), PEER_CHUNK_ROWS // 2),
                          PEER_CHUNK_ROWS // 2)
            n_ib = PEER_CHUNK_ROWS // PEER_NKEYS
            for grp in range(n_ib // PEER_IB_GROUP):
                ibs = [c * n_ib + grp * PEER_IB_GROUP + s for s in range(PEER_IB_GROUP)]
                r1_rows = [[r1_ref[hd, pl.ds(key_off + ib, 1), :] for hd in range(PEER_HEADS)]
                           for ib in ibs]
                c1_rows = [[c1_ref[hd, pl.ds(key_off + ib, 1), :] for hd in range(PEER_HEADS)]
                           for ib in ibs]
                for tg in range(tt // LANES):
                    lanes = slice(tg * LANES, (tg + 1) * LANES)
                    for part in range(PEER_NKEYS // PEER_PART_ROWS):
                        krows = slice(part * PEER_PART_ROWS, (part + 1) * PEER_PART_ROWS)
                        ws = [None] * PEER_IB_GROUP
                        for hd in range(PEER_HEADS):
                            n2t = n2s_ref[hd, krows, lanes]
                            e2t = e2s_ref[hd, krows, lanes]
                            for s in range(PEER_IB_GROUP):
                                r1 = jnp.broadcast_to(r1_rows[s][hd][:, lanes], tile).astype(BF16)
                                c1 = jnp.broadcast_to(c1_rows[s][hd][:, lanes], tile).astype(BF16)
                                term = jnp.where(r1 < n2t, e2t, jnp.zeros(tile, BF16)) * c1
                                ws[s] = term if ws[s] is None else ws[s] + term
                        for s in range(PEER_IB_GROUP):
                            erows = pl.ds(pl.multiple_of(ibs[s] * PEER_NKEYS + part * PEER_PART_ROWS,
                                                         PEER_PART_ROWS), PEER_PART_ROWS)
                            gt_cur_ref[erows, lanes] = act_cur_ref[erows, lanes] * ws[s]
            act_next_ref[rows, :tt] = gelu_bf16(jnp.dot(_unpack_bf16(u_next_ref[prows, :]),
                                                        xt_ref[:, :tt], preferred_element_type=F32))
            acc_ref[rows, :tt] += jnp.dot(_unpack_bf16(vt_prev_ref[prows, :]), gt_prev_ref[:, :tt],
                                        preferred_element_type=F32)
            return carry

        lax.fori_loop(0, te // PEER_CHUNK_ROWS, chunk, 0)

    phase(u_a_ref, act_b_ref, act_a_ref, gt_a_ref, vt_a_ref, gt_b_ref, 0)
    phase(u_b_ref, act_a_ref, act_b_ref, gt_b_ref, vt_b_ref, gt_a_ref, n_sub)

    @pl.when(j == pl.num_programs(1) - 1)
    def _():
        acc = acc_ref[:, :tt] + jnp.dot(_unpack_bf16(vt_last_ref[...]), gt_b_ref[:, :tt],
                                     preferred_element_type=F32)
        z = alpha * h_ref[...] + acc.T
        o_ref[...] = _layer_norm(z, g_ref[...], b_ref[...])


def _peer_dense_call(h, r1, c1, n2, e2, u, vt, g, b, alpha, tt, te):
    T, D = h.shape
    n_sub = te // PEER_NKEYS
    n_blk = PEER_EXPERTS // te
    n_steps = n_blk // 2
    assert D == te and n_blk % 2 == 0
    spec1 = pl.BlockSpec((PEER_HEADS, 2 * n_sub, tt), lambda i, j: (0, j, i))
    spec2 = pl.BlockSpec((PEER_HEADS, PEER_NKEYS, tt), lambda i, j: (0, 0, i))
    last = n_blk - 1
    ttp = tt + LANES
    u_spec = lambda f: pl.BlockSpec((te // 2, D), lambda i, j: (f(j), 0))
    vt_spec = lambda f: pl.BlockSpec((D // 2, te), lambda i, j: (0, f(j)))
    return pl.pallas_call(
        functools.partial(_peer_dense_kernel, alpha=alpha),
        grid=(T // tt, n_steps),
        in_specs=[pl.BlockSpec((tt, D), lambda i, j: (i, 0)),
                  spec1, spec1, spec2, spec2,
                  u_spec(lambda j: 0),
                  u_spec(lambda j: 2 * j + 1),
                  u_spec(lambda j: jnp.minimum(2 * j + 2, last)),
                  vt_spec(lambda j: jnp.maximum(2 * j - 1, 0)),
                  vt_spec(lambda j: 2 * j),
                  vt_spec(lambda j: last),
                  pl.BlockSpec((1, D), lambda i, j: (0, 0)),
                  pl.BlockSpec((1, D), lambda i, j: (0, 0))],
        out_specs=pl.BlockSpec((tt, D), lambda i, j: (i, 0)),
        out_shape=jax.ShapeDtypeStruct((T, D), F32),
        scratch_shapes=[pltpu.VMEM((D, ttp), BF16), pltpu.VMEM((D, ttp), F32),
                        pltpu.VMEM((PEER_HEADS, PEER_NKEYS, ttp), BF16),
                        pltpu.VMEM((PEER_HEADS, PEER_NKEYS, ttp), BF16),
                        pltpu.VMEM((te, ttp), BF16), pltpu.VMEM((te, ttp), BF16),
                        pltpu.VMEM((te, ttp), BF16), pltpu.VMEM((te, ttp), BF16)],
        compiler_params=_params("parallel", "arbitrary"),
        name="peer_dense",
    )(h, r1, c1, n2, e2, u, u, u, vt, vt, vt, g, b)


def _pair_perm():
    idx = []
    for i in range(N_A_HEADS // 2):
        for grp in range(2):
            hd = i + (N_A_HEADS // 2) * grp
            idx.extend(range(hd * HEAD_DIM, (hd + 1) * HEAD_DIM))
    return np.asarray(idx, dtype=np.int32)


def _rope_tables(S):
    pos = np.arange(S)
    row = (pos // GRID_W).astype(np.float32)
    col = (pos % GRID_W).astype(np.float32)
    inv_freq = jnp.asarray(ROPE_THETA, F32) ** (-jnp.arange(ROPE_FREQS, dtype=F32) / ROPE_FREQS)
    ang_r = jnp.asarray(row)[:, None] * inv_freq
    ang_c = jnp.asarray(col)[:, None] * inv_freq
    cr, sr, cc, sc = jnp.cos(ang_r), jnp.sin(ang_r), jnp.cos(ang_c), jnp.sin(ang_c)
    zero = jnp.zeros_like(sr)
    cos = jnp.concatenate([cr, cr, cc, cc], axis=-1)
    sa = jnp.concatenate([-sr, zero, -sc, zero], axis=-1)
    sb = jnp.concatenate([zero, sr, zero, sc], axis=-1)
    tile2 = lambda t: jnp.concatenate([t, t], axis=-1)
    return tile2(cos), tile2(sa), tile2(sb)


def kernel(x, ln_in_g, ln_in_b, w_in, qn_g, kn_g, sink, gn_a_g, gn_b_g, w_o, ln1_g, ln1_b,
           peer_wq, peer_keys, peer_u, peer_v, ln2_g, ln2_b):
    B, S, D = x.shape
    depth = w_in.shape[0]
    T = B * S
    alpha = float((2.0 * depth) ** 0.25)
    tm = min(512, S)
    tq = min(256, S)
    tt_route = 128
    tt_dense = min(512, T)
    te = 1024

    perm = _pair_perm()
    col_perm = np.concatenate([perm, Q_W + np.arange(2 * KV_W, dtype=np.int32),
                               Q_W + 2 * KV_W + perm,
                               2 * Q_W + 2 * KV_W + np.arange(2 * KV_W, dtype=np.int32)])
    cos, sa, sb = _rope_tables(S)
    row2 = lambda v: v.reshape(1, -1)
    tile2 = lambda v: jnp.concatenate([v, v]).reshape(1, -1)

    h = _ln_call(x.reshape(T, D), row2(ln_in_g), row2(ln_in_b), tm)
    for l in range(depth):
        w = w_in[l][:, col_perm].astype(BF16)
        qa, ka, va, qb, kb, vb = _proj_call(h, w, tile2(qn_g[l]), tile2(kn_g[l]), cos, sa, sb, S, tm)
        oa = _attn_a_call(qa, ka, va, row2(gn_a_g[l][perm]), B, S, tq)
        ob = _attn_b_call(sink[l], qb, kb, vb, row2(gn_b_g[l][perm]), B, S)
        wa = w_o[l][:Q_W][perm].astype(BF16)
        wb = w_o[l][Q_W:][perm].astype(BF16)
        h = _oproj_call(oa, ob, wa, wb, h, row2(ln1_g[l]), row2(ln1_b[l]), alpha, tm)
        keys = peer_keys[l].reshape(2 * PEER_HEADS, PEER_NKEYS, PEER_DKEY // 2).astype(BF16)
        st = _peer_scores_call(h, peer_wq[l].astype(BF16), keys, tm)
        r1, c1, n2, e2 = _router_call(st, tt_route)
        h = _peer_dense_call(h, r1, c1, n2, e2, _pack_bf16_rows(peer_u[l]), _pack_bf16_rows(peer_v[l].T),
                             row2(ln2_g[l]), row2(ln2_b[l]), alpha, tt_dense, te)
    return h.reshape(B, S, D)
```

```python
import functools

import numpy as np
import jax
import jax.numpy as jnp
from jax import lax
from jax.experimental import pallas as pl
from jax.experimental.pallas import tpu as pltpu

D_MODEL = 1024
HEAD_DIM = 64
N_A_HEADS = 8
N_B_HEADS = 8
LANES = 128
SUBLANES = 8
PAIR_W = 2 * HEAD_DIM
Q_W = N_A_HEADS * HEAD_DIM
KV_W = 2 * HEAD_DIM
IN_WIDTH = 2 * (Q_W + 2 * KV_W)
BLOCK = 128
WINDOW = 128
GRID_W = 64
ROPE_THETA = 10000.0
ROPE_FREQS = HEAD_DIM // 4

PEER_HEADS = 8
PEER_NKEYS = 128
PEER_EXPERTS = PEER_NKEYS * PEER_NKEYS
PEER_DKEY = 256
PEER_TOPK = 16

LN_EPS = 1e-5
RMS_EPS = 1e-6
NEG_INF = -1e30

VMEM_LIMIT_BYTES = 56 * 1024 * 1024

BF16 = jnp.bfloat16
F32 = jnp.float32


def _alibi_slopes(n):
    return [float(2.0 ** (-((i + 1) * 8.0 / n))) for i in range(n)]


def _params(*sem):
    return pltpu.CompilerParams(dimension_semantics=sem,
                                vmem_limit_bytes=VMEM_LIMIT_BYTES)


def _pack_bf16_rows(w):
    r, c = w.shape
    pairs = w.astype(BF16).reshape(r // 2, 2, c).swapaxes(1, 2)
    return lax.bitcast_convert_type(pairs, jnp.uint32)


def _unpack_bf16(words):
    return pltpu.bitcast(words, BF16)


def _layer_norm(z, g, b):
    mu = jnp.mean(z, axis=-1, keepdims=True)
    zc = z - mu
    var = jnp.mean(zc * zc, axis=-1, keepdims=True)
    return zc * lax.rsqrt(var + LN_EPS) * g + b


def _half_rms(t, lo):
    sq = t * t
    s_lo = jnp.sum(jnp.where(lo, sq, 0.0), axis=-1, keepdims=True)
    s_hi = jnp.sum(jnp.where(lo, 0.0, sq), axis=-1, keepdims=True)
    return jnp.where(lo, lax.rsqrt(s_lo / HEAD_DIM + RMS_EPS),
                     lax.rsqrt(s_hi / HEAD_DIM + RMS_EPS))


def _ln_kernel(x_ref, g_ref, b_ref, o_ref):
    o_ref[...] = _layer_norm(x_ref[...], g_ref[...], b_ref[...])


def _ln_call(x, g, b, tm):
    T, D = x.shape
    return pl.pallas_call(
        _ln_kernel,
        grid=(T // tm,),
        in_specs=[pl.BlockSpec((tm, D), lambda i: (i, 0)),
                  pl.BlockSpec((1, D), lambda i: (0, 0)),
                  pl.BlockSpec((1, D), lambda i: (0, 0))],
        out_specs=pl.BlockSpec((tm, D), lambda i: (i, 0)),
        out_shape=jax.ShapeDtypeStruct((T, D), F32),
        compiler_params=_params("parallel"),
        name="ln_in",
    )(x, g, b)


def _proj_kernel(h_ref, w_ref, qg_ref, kg_ref, cos_ref, sa_ref, sb_ref,
                 qa_ref, ka_ref, va_ref, qb_ref, kb_ref, vb_ref):
    tm = h_ref.shape[0]
    y = jnp.dot(h_ref[...].astype(BF16), w_ref[...], preferred_element_type=F32)
    lo = lax.broadcasted_iota(jnp.int32, (tm, PAIR_W), 1) < HEAD_DIM
    cos, sa, sb = cos_ref[...], sa_ref[...], sb_ref[...]
    scale = HEAD_DIM ** -0.5

    def norm_rope(t, g):
        xn = t * _half_rms(t, lo) * g
        return xn * cos + pltpu.roll(xn, PAIR_W - ROPE_FREQS, 1) * sa + pltpu.roll(xn, ROPE_FREQS, 1) * sb

    for i in range(Q_W // PAIR_W):
        t = y[:, i * PAIR_W:(i + 1) * PAIR_W]
        qa_ref[:, i * PAIR_W:(i + 1) * PAIR_W] = (norm_rope(t, qg_ref[...]) * scale).astype(BF16)
    off = Q_W
    ka_ref[...] = norm_rope(y[:, off:off + KV_W], kg_ref[...]).astype(BF16)
    off += KV_W
    va_ref[...] = y[:, off:off + KV_W].astype(BF16)
    off += KV_W
    qb_ref[...] = (y[:, off:off + Q_W] * scale).astype(BF16)
    off += Q_W
    kb_ref[...] = y[:, off:off + KV_W].astype(BF16)
    off += KV_W
    vb_ref[...] = y[:, off:off + KV_W].astype(BF16)


def _proj_call(h, w, qg, kg, cos, sa, sb, S, tm):
    T, D = h.shape
    n_pos = S // tm
    row = lambda i: (i, 0)
    const = lambda i: (0, 0)
    pos = lambda i: (i % n_pos, 0)
    outs = [jax.ShapeDtypeStruct((T, w_), BF16) for w_ in (Q_W, KV_W, KV_W, Q_W, KV_W, KV_W)]
    return pl.pallas_call(
        _proj_kernel,
        grid=(T // tm,),
        in_specs=[pl.BlockSpec((tm, D), row),
                  pl.BlockSpec((D, IN_WIDTH), const),
                  pl.BlockSpec((1, PAIR_W), const),
                  pl.BlockSpec((1, PAIR_W), const),
                  pl.BlockSpec((tm, PAIR_W), pos),
                  pl.BlockSpec((tm, PAIR_W), pos),
                  pl.BlockSpec((tm, PAIR_W), pos)],
        out_specs=[pl.BlockSpec((tm, o.shape[1]), row) for o in outs],
        out_shape=outs,
        compiler_params=_params("parallel"),
        name="in_proj",
    )(h, w, qg, kg, cos, sa, sb)


def _attn_a_kernel(q_ref, k_ref, v_ref, g_ref, o_ref):
    tq = q_ref.shape[0]
    k = k_ref[...]
    v = v_ref[...]
    lo = lax.broadcasted_iota(jnp.int32, (tq, PAIR_W), 1) < HEAD_DIM
    nt = (((1,), (1,)), ((), ()))
    for i in range(Q_W // PAIR_W):
        qp = q_ref[:, i * PAIR_W:(i + 1) * PAIR_W]
        halves = []
        for grp in range(2):
            qz = jnp.where(lo if grp == 0 else jnp.logical_not(lo), qp, jnp.zeros_like(qp))
            s = lax.dot_general(qz, k, nt, preferred_element_type=F32)
            m = jnp.max(s, axis=-1, keepdims=True)
            p = jnp.exp(s - m)
            l = jnp.sum(p, axis=-1, keepdims=True)
            halves.append(jnp.dot(p.astype(BF16), v, preferred_element_type=F32) / l)
        o = jnp.where(lo, halves[0], halves[1])
        o = o * _half_rms(o, lo) * g_ref[:, i * PAIR_W:(i + 1) * PAIR_W]
        o_ref[:, i * PAIR_W:(i + 1) * PAIR_W] = o.astype(BF16)


def _attn_a_call(q, k, v, g, B, S, tq):
    T = q.shape[0]
    nq = S // tq
    return pl.pallas_call(
        _attn_a_kernel,
        grid=(B, nq),
        in_specs=[pl.BlockSpec((tq, Q_W), lambda b, i: (b * nq + i, 0)),
                  pl.BlockSpec((S, KV_W), lambda b, i: (b, 0)),
                  pl.BlockSpec((S, KV_W), lambda b, i: (b, 0)),
                  pl.BlockSpec((1, Q_W), lambda b, i: (0, 0))],
        out_specs=pl.BlockSpec((tq, Q_W), lambda b, i: (b * nq + i, 0)),
        out_shape=jax.ShapeDtypeStruct((T, Q_W), BF16),
        compiler_params=_params("parallel", "parallel"),
        name="attn_global",
    )(q, k, v, g)


def _attn_b_kernel(sink_ref, q_ref, k_ref, v_ref, g_ref, o_ref, *, slopes):
    j = pl.program_id(1)
    nb = k_ref.shape[0] // BLOCK
    span = BLOCK + 2 * WINDOW
    ks, vs = [], []
    for c in (-1, 0, 1):
        jc = j + c
        start = pl.multiple_of(jnp.clip(jc, 0, nb - 1) * BLOCK, BLOCK)
        ks.append(k_ref[pl.ds(start, BLOCK), :])
        vs.append(v_ref[pl.ds(start, BLOCK), :])
    kslab = jnp.concatenate(ks, axis=0)
    vslab = jnp.concatenate(vs, axis=0)
    qi = lax.broadcasted_iota(jnp.int32, (BLOCK, span), 0)
    ki = lax.broadcasted_iota(jnp.int32, (BLOCK, span), 1)
    dist = jnp.abs(qi + WINDOW - ki)
    s_pos = (j - 1) * BLOCK + ki
    in_seq = jnp.logical_and(s_pos >= 0, s_pos < nb * BLOCK)
    valid = jnp.logical_and(dist <= WINDOW, in_seq)
    distf = dist.astype(F32)

    lo = lax.broadcasted_iota(jnp.int32, (BLOCK, PAIR_W), 1) < HEAD_DIM
    hi = jnp.logical_not(lo)
    n_pairs = Q_W // PAIR_W
    heads = [(i, grp) for grp in range(2) for i in range(n_pairs)]
    qs = []
    for i, grp in heads:
        qp = q_ref[:, i * PAIR_W:(i + 1) * PAIR_W]
        qs.append(jnp.where(lo if grp == 0 else hi, qp, jnp.zeros_like(qp)))
    s_all = lax.dot_general(jnp.concatenate(qs, axis=0), kslab, (((1,), (1,)), ((), ())),
                            preferred_element_type=F32)
    ps, inv = [], []
    for n, (i, grp) in enumerate(heads):
        hd = i + n_pairs * grp
        s = s_all[n * BLOCK:(n + 1) * BLOCK] - slopes[hd] * distf
        s = jnp.where(valid, s, NEG_INF)
        sink = sink_ref[hd]
        m = jnp.maximum(jnp.max(s, axis=-1, keepdims=True), sink)
        p = jnp.exp(s - m)
        inv.append(1.0 / (jnp.sum(p, axis=-1, keepdims=True) + jnp.exp(sink - m)))
        ps.append(p.astype(BF16))
    o_all = jnp.dot(jnp.concatenate(ps, axis=0), vslab, preferred_element_type=F32)
    for i in range(n_pairs):
        o0 = o_all[i * BLOCK:(i + 1) * BLOCK] * inv[i]
        o1 = o_all[(i + n_pairs) * BLOCK:(i + n_pairs + 1) * BLOCK] * inv[i + n_pairs]
        o = jnp.where(lo, o0, o1)
        o = o * _half_rms(o, lo) * g_ref[:, i * PAIR_W:(i + 1) * PAIR_W]
        o_ref[:, i * PAIR_W:(i + 1) * PAIR_W] = o.astype(BF16)


def _attn_b_call(sink, q, k, v, g, B, S):
    T = q.shape[0]
    nb = S // BLOCK
    kern = functools.partial(_attn_b_kernel, slopes=_alibi_slopes(N_B_HEADS))
    return pl.pallas_call(
        kern,
        grid=(B, nb),
        in_specs=[pl.BlockSpec(memory_space=pltpu.SMEM),
                  pl.BlockSpec((BLOCK, Q_W), lambda b, i: (b * nb + i, 0)),
                  pl.BlockSpec((S, KV_W), lambda b, i: (b, 0)),
                  pl.BlockSpec((S, KV_W), lambda b, i: (b, 0)),
                  pl.BlockSpec((1, Q_W), lambda b, i: (0, 0))],
        out_specs=pl.BlockSpec((BLOCK, Q_W), lambda b, i: (b * nb + i, 0)),
        out_shape=jax.ShapeDtypeStruct((T, Q_W), BF16),
        compiler_params=_params("parallel", "parallel"),
        name="attn_window",
    )(sink, q, k, v, g)


def _oproj_kernel(oa_ref, ob_ref, wa_ref, wb_ref, h_ref, g_ref, b_ref, o_ref, *, alpha):
    mix = jnp.dot(oa_ref[...], wa_ref[...], preferred_element_type=F32)
    mix = mix + jnp.dot(ob_ref[...], wb_ref[...], preferred_element_type=F32)
    o_ref[...] = _layer_norm(alpha * h_ref[...] + mix, g_ref[...], b_ref[...])


def _oproj_call(oa, ob, wa, wb, h, g, b, alpha, tm):
    T, D = h.shape
    row = lambda i: (i, 0)
    const = lambda i: (0, 0)
    return pl.pallas_call(
        functools.partial(_oproj_kernel, alpha=alpha),
        grid=(T // tm,),
        in_specs=[pl.BlockSpec((tm, Q_W), row), pl.BlockSpec((tm, Q_W), row),
                  pl.BlockSpec((Q_W, D), const), pl.BlockSpec((Q_W, D), const),
                  pl.BlockSpec((tm, D), row),
                  pl.BlockSpec((1, D), const), pl.BlockSpec((1, D), const)],
        out_specs=pl.BlockSpec((tm, D), row),
        out_shape=jax.ShapeDtypeStruct((T, D), F32),
        compiler_params=_params("parallel"),
        name="out_proj_ln",
    )(oa, ob, wa, wb, h, g, b)


def _peer_scores_kernel(h_ref, wq_ref, keys_ref, st_ref):
    q = jnp.dot(h_ref[...].astype(BF16), wq_ref[...], preferred_element_type=F32).astype(BF16)
    half = PEER_DKEY // 2
    for n in range(2 * PEER_HEADS):
        qs = q[:, n * half:(n + 1) * half]
        st_ref[n] = lax.dot_general(keys_ref[n], qs, (((1,), (1,)), ((), ())),
                                    preferred_element_type=F32)


def _peer_scores_call(h, wq, keys, tm):
    T, D = h.shape
    n_sets = 2 * PEER_HEADS
    half = PEER_DKEY // 2
    return pl.pallas_call(
        _peer_scores_kernel,
        grid=(T // tm,),
        in_specs=[pl.BlockSpec((tm, D), lambda i: (i, 0)),
                  pl.BlockSpec((D, PEER_HEADS * PEER_DKEY), lambda i: (0, 0)),
                  pl.BlockSpec((n_sets, PEER_NKEYS, half), lambda i: (0, 0, 0))],
        out_specs=pl.BlockSpec((n_sets, PEER_NKEYS, tm), lambda i: (0, 0, i)),
        out_shape=jax.ShapeDtypeStruct((n_sets, PEER_NKEYS, T), F32),
        compiler_params=_params("parallel"),
        name="peer_scores",
    )(h, wq, keys)


ROUTE_TOKENS = SUBLANES * LANES
ROUTE_PASS = 4
ROUTE_PAIRS = [(r1, r2) for r1 in range(PEER_TOPK) for r2 in range(PEER_TOPK)
               if (r1 + 1) * (r2 + 1) <= PEER_TOPK]


def _slab(i):
    if isinstance(i, int):
        return pl.ds(i * SUBLANES, SUBLANES)
    return pl.ds(pl.multiple_of(i * SUBLANES, SUBLANES), SUBLANES)


def _insert_sorted(tops, x):
    out = []
    for t in tops[:-1]:
        out.append(jnp.maximum(t, x))
        x = jnp.minimum(t, x)
    out.append(jnp.maximum(tops[-1], x))
    return out


def _router_kernel(st_ref, r1_ref, c1_ref, n2_ref, e2_ref,
                   s1_ref, s2_ref, a1_ref, a2_ref, rk1_ref, rk2_ref, n2k_ref, z_ref,
                   cand_ref, sel_ref, n2r_ref, out_ref):
    K = PEER_TOPK
    neg = -jnp.inf
    shape = (SUBLANES, LANES)
    ones = jnp.ones(shape, F32)
    zeros = jnp.zeros(shape, F32)

    def load_keys_in(src_set, dst_ref):
        for g in range(SUBLANES):
            dst_ref[pl.ds(g, PEER_NKEYS, stride=SUBLANES), :] = st_ref[src_set, :, g * LANES:(g + 1) * LANES]

    def store_keys_out(dst_ref, h):
        for g in range(SUBLANES):
            dst_ref[h, :, g * LANES:(g + 1) * LANES] = out_ref[pl.ds(g, PEER_NKEYS, stride=SUBLANES), :]

    def top_values(value_of, n, a_ref):
        thr = None
        for p in range(K // ROUTE_PASS):
            groups = [[jnp.full(shape, neg, F32)] * ROUTE_PASS for _ in range(2)]
            for k in range(n):
                x = value_of(k)
                if thr is not None:
                    x = jnp.where(x < thr, x, neg)
                groups[k % 2] = _insert_sorted(groups[k % 2], x)
            tops = groups[0]
            for x in groups[1]:
                tops = _insert_sorted(tops, x)
            for i, t in enumerate(tops):
                a_ref[_slab(p * ROUTE_PASS + i), :] = t
            thr = tops[-1]

    def adjacent_equal(a_ref):
        bad = zeros
        prev = a_ref[_slab(0), :]
        for r in range(1, K):
            cur = a_ref[_slab(r), :]
            bad = jnp.where(cur == prev, ones, bad)
            prev = cur
        return bad

    def cand(r1, r2):
        return a1_ref[_slab(r1), :] + a2_ref[_slab(r2), :]

    def finish_candidates(selected):
        top = cand(0, 0)
        z = zeros
        counts = [zeros] * K
        for idx, (r1, r2) in enumerate(ROUTE_PAIRS):
            s = selected(idx)
            z = z + s * jnp.exp(cand(r1, r2) - top)
            counts[r2] = counts[r2] + s
        z_ref[...] = z
        for r2 in range(K):
            n2r_ref[_slab(r2), :] = counts[r2]

    def head(h, carry):
        load_keys_in(2 * h, s1_ref)
        load_keys_in(2 * h + 1, s2_ref)

        top_values(lambda k: s1_ref[_slab(k), :], PEER_NKEYS, a1_ref)
        top_values(lambda k: s2_ref[_slab(k), :], PEER_NKEYS, a2_ref)
        a1 = [a1_ref[_slab(r), :] for r in range(K)]
        n_ge1 = zeros
        for k in range(PEER_NKEYS):
            s = s1_ref[_slab(k), :]
            rank = jnp.full(shape, float(K), F32)
            for r in reversed(range(K)):
                rank = jnp.where(s >= a1[r], float(r), rank)
            rk1_ref[_slab(k), :] = rank
            n_ge1 = n_ge1 + jnp.where(rank < float(K), ones, zeros)
        top_values(lambda i: cand(*ROUTE_PAIRS[i]), len(ROUTE_PAIRS), cand_ref)
        tau = cand_ref[_slab(K - 1), :]
        finish_candidates(lambda i: jnp.where(cand(*ROUTE_PAIRS[i]) >= tau, ones, zeros))
        a2 = [a2_ref[_slab(r), :] for r in range(K)]
        counts = [n2r_ref[_slab(r), :] for r in range(K)]
        n_sel = zeros
        for r in range(K):
            n_sel = n_sel + counts[r]
        n_ge2 = zeros
        for k in range(PEER_NKEYS):
            s = s2_ref[_slab(k), :]
            v = zeros
            for r in reversed(range(K)):
                v = jnp.where(s >= a2[r], counts[r], v)
            n2k_ref[_slab(k), :] = v
            n_ge2 = n_ge2 + jnp.where(s >= a2[K - 1], ones, zeros)
        bad = adjacent_equal(a1_ref) + adjacent_equal(a2_ref) + adjacent_equal(cand_ref)
        bad = bad + jnp.where(n_ge1 == float(K), zeros, ones) + jnp.where(n_ge2 == float(K), zeros, ones)
        bad = bad + jnp.where(n_sel == float(K), zeros, ones)

        @pl.when(jnp.sum(bad) > 0.0)
        def _():
            def exact_ranks(s_ref, rk_ref):
                for kc in range(PEER_NKEYS // SUBLANES):
                    mine = [s_ref[_slab(kc * SUBLANES + i), :] for i in range(SUBLANES)]

                    def body(j, cnt):
                        sj = s_ref[_slab(j), :]
                        new = []
                        for i in range(SUBLANES):
                            tie = jnp.where(j < kc * SUBLANES + i, 1.0, 0.0)
                            new.append(cnt[i] + jnp.where(sj > mine[i], 1.0,
                                                          jnp.where(sj == mine[i], tie, 0.0)))
                        return new

                    cnt = lax.fori_loop(0, PEER_NKEYS, body, [zeros] * SUBLANES)
                    for i in range(SUBLANES):
                        rk_ref[_slab(kc * SUBLANES + i), :] = jnp.minimum(cnt[i], float(K))

            def values_by_rank(s_ref, rk_ref, a_ref):
                def body(r, carry):
                    m = jnp.full(shape, neg, F32)
                    rf = r.astype(F32)
                    for k in range(PEER_NKEYS):
                        m = jnp.maximum(m, jnp.where(rk_ref[_slab(k), :] == rf, s_ref[_slab(k), :], neg))
                    a_ref[_slab(r), :] = m
                    return carry

                lax.fori_loop(0, K, body, 0)

            exact_ranks(s1_ref, rk1_ref)
            exact_ranks(s2_ref, rk2_ref)
            values_by_rank(s1_ref, rk1_ref, a1_ref)
            values_by_rank(s2_ref, rk2_ref, a2_ref)
            n_pairs = len(ROUTE_PAIRS)
            for idx, (r1, r2) in enumerate(ROUTE_PAIRS):
                cand_ref[_slab(idx), :] = cand(r1, r2)
            for c0 in range(0, n_pairs, SUBLANES):
                ids = list(range(c0, min(c0 + SUBLANES, n_pairs)))
                mine = [cand_ref[_slab(i), :] for i in ids]

                def body(j, cnt, ids=ids, mine=mine):
                    vj = cand_ref[_slab(j), :]
                    new = []
                    for n, i in enumerate(ids):
                        tie = jnp.where(j < i, 1.0, 0.0)
                        new.append(cnt[n] + jnp.where(vj > mine[n], 1.0,
                                                      jnp.where(vj == mine[n], tie, 0.0)))
                    return new

                cnt = lax.fori_loop(0, n_pairs, body, [zeros] * len(ids))
                for n, i in enumerate(ids):
                    sel_ref[_slab(i), :] = jnp.where(cnt[n] < float(K), ones, zeros)
            finish_candidates(lambda i: sel_ref[_slab(i), :])
            counts_x = [n2r_ref[_slab(r), :] for r in range(K)]
            for k in range(PEER_NKEYS):
                rank = rk2_ref[_slab(k), :]
                v = zeros
                for r in range(K):
                    v = jnp.where(rank == float(r), counts_x[r], v)
                n2k_ref[_slab(k), :] = v

        top1 = a1_ref[_slab(0), :]
        top2 = a2_ref[_slab(0), :]
        inv_z = 1.0 / z_ref[...]
        for k in range(PEER_NKEYS):
            out_ref[_slab(k), :] = rk1_ref[_slab(k), :]
        store_keys_out(r1_ref, h)
        for k in range(PEER_NKEYS):
            out_ref[_slab(k), :] = jnp.exp(s1_ref[_slab(k), :] - top1) * inv_z
        store_keys_out(c1_ref, h)
        for k in range(PEER_NKEYS):
            out_ref[_slab(k), :] = n2k_ref[_slab(k), :]
        store_keys_out(n2_ref, h)
        for k in range(PEER_NKEYS):
            out_ref[_slab(k), :] = jnp.exp(s2_ref[_slab(k), :] - top2)
        store_keys_out(e2_ref, h)
        return carry

    lax.fori_loop(0, PEER_HEADS, head, 0)


def _router_call(st):
    T = st.shape[-1]
    tt = ROUTE_TOKENS
    shape = (PEER_HEADS, PEER_NKEYS, T)
    outs = [jax.ShapeDtypeStruct(shape, F32)] * 4
    spec = pl.BlockSpec((PEER_HEADS, PEER_NKEYS, tt), lambda i: (0, 0, i))
    keys = pltpu.VMEM((PEER_NKEYS * SUBLANES, LANES), F32)
    tops = pltpu.VMEM((PEER_TOPK * SUBLANES, LANES), F32)
    pairs = pltpu.VMEM((len(ROUTE_PAIRS) * SUBLANES, LANES), F32)
    return pl.pallas_call(
        _router_kernel,
        grid=(T // tt,),
        in_specs=[pl.BlockSpec((2 * PEER_HEADS, PEER_NKEYS, tt), lambda i: (0, 0, i))],
        out_specs=[spec] * 4,
        out_shape=outs,
        scratch_shapes=[keys, keys, tops, tops, keys, keys, keys,
                        pltpu.VMEM((SUBLANES, LANES), F32), pairs, pairs, tops, keys],
        compiler_params=_params("parallel"),
        name="peer_router",
    )(st)


PEER_CHUNK_ROWS = 256
PEER_IB_GROUP = 2
PEER_PART_ROWS = 64


def _peer_dense_kernel(h_ref, r1_ref, c1_ref, n2_ref, e2_ref, u_first_ref, u_a_ref, u_b_ref,
                       vt_a_ref, vt_b_ref, vt_last_ref, g_ref, b_ref, o_ref,
                       xt_ref, acc_ref, n2s_ref, e2s_ref, act_a_ref, act_b_ref, gt_a_ref, gt_b_ref,
                       *, alpha):
    j = pl.program_id(1)
    tt = h_ref.shape[0]
    te = vt_a_ref.shape[1]
    n_sub = te // PEER_NKEYS
    tile = (PEER_PART_ROWS, LANES)
    sqrt_half = float(np.sqrt(0.5))

    def gelu_bf16(a):
        return (a * (0.5 + 0.5 * lax.erf(a * sqrt_half))).astype(BF16)

    @pl.when(j == 0)
    def _():
        xt_ref[:, :tt] = h_ref[...].T.astype(BF16)
        acc_ref[:, :tt] = jnp.zeros((acc_ref.shape[0], tt), F32)
        n2s_ref[:, :, :tt] = n2_ref[...].astype(BF16)
        e2s_ref[:, :, :tt] = e2_ref[...].astype(BF16)
        gt_b_ref[:, :tt] = jnp.zeros((te, tt), BF16)
        act_a_ref[:, :tt] = gelu_bf16(jnp.dot(_unpack_bf16(u_first_ref[...]), xt_ref[:, :tt],
                                              preferred_element_type=F32))

    def phase(u_next_ref, act_next_ref, act_cur_ref, gt_cur_ref, vt_prev_ref, gt_prev_ref, key_off):
        def chunk(c, carry):
            rows = pl.ds(pl.multiple_of(c * PEER_CHUNK_ROWS, PEER_CHUNK_ROWS), PEER_CHUNK_ROWS)
            prows = pl.ds(pl.multiple_of(c * (PEER_CHUNK_ROWS // 2), PEER_CHUNK_ROWS // 2),
                          PEER_CHUNK_ROWS // 2)
            n_ib = PEER_CHUNK_ROWS // PEER_NKEYS
            for grp in range(n_ib // PEER_IB_GROUP):
                ibs = [c * n_ib + grp * PEER_IB_GROUP + s for s in range(PEER_IB_GROUP)]
                r1_rows = [[r1_ref[hd, pl.ds(key_off + ib, 1), :] for hd in range(PEER_HEADS)]
                           for ib in ibs]
                c1_rows = [[c1_ref[hd, pl.ds(key_off + ib, 1), :] for hd in range(PEER_HEADS)]
                           for ib in ibs]
                for tg in range(tt // LANES):
                    lanes = slice(tg * LANES, (tg + 1) * LANES)
                    for part in range(PEER_NKEYS // PEER_PART_ROWS):
                        krows = slice(part * PEER_PART_ROWS, (part + 1) * PEER_PART_ROWS)
                        ws = [None] * PEER_IB_GROUP
                        for hd in range(PEER_HEADS):
                            n2t = n2s_ref[hd, krows, lanes]
                            e2t = e2s_ref[hd, krows, lanes]
                            for s in range(PEER_IB_GROUP):
                                r1 = jnp.broadcast_to(r1_rows[s][hd][:, lanes], tile).astype(BF16)
                                c1 = jnp.broadcast_to(c1_rows[s][hd][:, lanes], tile).astype(BF16)
                                term = jnp.where(r1 < n2t, e2t, jnp.zeros(tile, BF16)) * c1
                                ws[s] = term if ws[s] is None else ws[s] + term
                        for s in range(PEER_IB_GROUP):
                            erows = pl.ds(pl.multiple_of(ibs[s] * PEER_NKEYS + part * PEER_PART_ROWS,
                                                         PEER_PART_ROWS), PEER_PART_ROWS)
                            gt_cur_ref[erows, lanes] = act_cur_ref[erows, lanes] * ws[s]
            act_next_ref[rows, :tt] = gelu_bf16(jnp.dot(_unpack_bf16(u_next_ref[prows, :]),
                                                        xt_ref[:, :tt], preferred_element_type=F32))
            acc_ref[rows, :tt] += jnp.dot(_unpack_bf16(vt_prev_ref[prows, :]), gt_prev_ref[:, :tt],
                                        preferred_element_type=F32)
            return carry

        lax.fori_loop(0, te // PEER_CHUNK_ROWS, chunk, 0)

    phase(u_a_ref, act_b_ref, act_a_ref, gt_a_ref, vt_a_ref, gt_b_ref, 0)
    phase(u_b_ref, act_a_ref, act_b_ref, gt_b_ref, vt_b_ref, gt_a_ref, n_sub)

    @pl.when(j == pl.num_programs(1) - 1)
    def _():
        acc = acc_ref[:, :tt] + jnp.dot(_unpack_bf16(vt_last_ref[...]), gt_b_ref[:, :tt],
                                     preferred_element_type=F32)
        z = alpha * h_ref[...] + acc.T
        o_ref[...] = _layer_norm(z, g_ref[...], b_ref[...])


def _peer_dense_call(h, r1, c1, n2, e2, u, vt, g, b, alpha, tt, te):
    T, D = h.shape
    n_sub = te // PEER_NKEYS
    n_blk = PEER_EXPERTS // te
    n_steps = n_blk // 2
    assert D == te and n_blk % 2 == 0
    spec1 = pl.BlockSpec((PEER_HEADS, 2 * n_sub, tt), lambda i, j: (0, j, i))
    spec2 = pl.BlockSpec((PEER_HEADS, PEER_NKEYS, tt), lambda i, j: (0, 0, i))
    last = n_blk - 1
    ttp = tt + LANES
    u_spec = lambda f: pl.BlockSpec((te // 2, D), lambda i, j: (f(j), 0))
    vt_spec = lambda f: pl.BlockSpec((D // 2, te), lambda i, j: (0, f(j)))
    return pl.pallas_call(
        functools.partial(_peer_dense_kernel, alpha=alpha),
        grid=(T // tt, n_steps),
        in_specs=[pl.BlockSpec((tt, D), lambda i, j: (i, 0)),
                  spec1, spec1, spec2, spec2,
                  u_spec(lambda j: 0),
                  u_spec(lambda j: 2 * j + 1),
                  u_spec(lambda j: jnp.minimum(2 * j + 2, last)),
                  vt_spec(lambda j: jnp.maximum(2 * j - 1, 0)),
                  vt_spec(lambda j: 2 * j),
                  vt_spec(lambda j: last),
                  pl.BlockSpec((1, D), lambda i, j: (0, 0)),
                  pl.BlockSpec((1, D), lambda i, j: (0, 0))],
        out_specs=pl.BlockSpec((tt, D), lambda i, j: (i, 0)),
        out_shape=jax.ShapeDtypeStruct((T, D), F32),
        scratch_shapes=[pltpu.VMEM((D, ttp), BF16), pltpu.VMEM((D, ttp), F32),
                        pltpu.VMEM((PEER_HEADS, PEER_NKEYS, ttp), BF16),
                        pltpu.VMEM((PEER_HEADS, PEER_NKEYS, ttp), BF16),
                        pltpu.VMEM((te, ttp), BF16), pltpu.VMEM((te, ttp), BF16),
                        pltpu.VMEM((te, ttp), BF16), pltpu.VMEM((te, ttp), BF16)],
        compiler_params=_params("parallel", "arbitrary"),
        name="peer_dense",
    )(h, r1, c1, n2, e2, u, u, u, vt, vt, vt, g, b)


def _pair_perm():
    idx = []
    for i in range(N_A_HEADS // 2):
        for grp in range(2):
            hd = i + (N_A_HEADS // 2) * grp
            idx.extend(range(hd * HEAD_DIM, (hd + 1) * HEAD_DIM))
    return np.asarray(idx, dtype=np.int32)


def _rope_tables(S):
    pos = np.arange(S)
    row = (pos // GRID_W).astype(np.float32)
    col = (pos % GRID_W).astype(np.float32)
    inv_freq = jnp.asarray(ROPE_THETA, F32) ** (-jnp.arange(ROPE_FREQS, dtype=F32) / ROPE_FREQS)
    ang_r = jnp.asarray(row)[:, None] * inv_freq
    ang_c = jnp.asarray(col)[:, None] * inv_freq
    cr, sr, cc, sc = jnp.cos(ang_r), jnp.sin(ang_r), jnp.cos(ang_c), jnp.sin(ang_c)
    zero = jnp.zeros_like(sr)
    cos = jnp.concatenate([cr, cr, cc, cc], axis=-1)
    sa = jnp.concatenate([-sr, zero, -sc, zero], axis=-1)
    sb = jnp.concatenate([zero, sr, zero, sc], axis=-1)
    tile2 = lambda t: jnp.concatenate([t, t], axis=-1)
    return tile2(cos), tile2(sa), tile2(sb)


def kernel(x, ln_in_g, ln_in_b, w_in, qn_g, kn_g, sink, gn_a_g, gn_b_g, w_o, ln1_g, ln1_b,
           peer_wq, peer_keys, peer_u, peer_v, ln2_g, ln2_b):
    B, S, D = x.shape
    depth = w_in.shape[0]
    T = B * S
    alpha = float((2.0 * depth) ** 0.25)
    tm = min(512, S)
    tq = min(256, S)
    tt_dense = min(512, T)
    te = 1024

    perm = _pair_perm()
    col_perm = np.concatenate([perm, Q_W + np.arange(2 * KV_W, dtype=np.int32),
                               Q_W + 2 * KV_W + perm,
                               2 * Q_W + 2 * KV_W + np.arange(2 * KV_W, dtype=np.int32)])
    cos, sa, sb = _rope_tables(S)
    row2 = lambda v: v.reshape(1, -1)
    tile2 = lambda v: jnp.concatenate([v, v]).reshape(1, -1)

    h = _ln_call(x.reshape(T, D), row2(ln_in_g), row2(ln_in_b), tm)
    for l in range(depth):
        w = w_in[l][:, col_perm].astype(BF16)
        qa, ka, va, qb, kb, vb = _proj_call(h, w, tile2(qn_g[l]), tile2(kn_g[l]), cos, sa, sb, S, tm)
        oa = _attn_a_call(qa, ka, va, row2(gn_a_g[l][perm]), B, S, tq)
        ob = _attn_b_call(sink[l], qb, kb, vb, row2(gn_b_g[l][perm]), B, S)
        wa = w_o[l][:Q_W][perm].astype(BF16)
        wb = w_o[l][Q_W:][perm].astype(BF16)
        h = _oproj_call(oa, ob, wa, wb, h, row2(ln1_g[l]), row2(ln1_b[l]), alpha, tm)
        keys = peer_keys[l].reshape(2 * PEER_HEADS, PEER_NKEYS, PEER_DKEY // 2).astype(BF16)
        st = _peer_scores_call(h, peer_wq[l].astype(BF16), keys, tm)
        r1, c1, n2, e2 = _router_call(st)
        h = _peer_dense_call(h, r1, c1, n2, e2, _pack_bf16_rows(peer_u[l]), _pack_bf16_rows(peer_v[l].T),
                             row2(ln2_g[l]), row2(ln2_b[l]), alpha, tt_dense, te)
    return h.reshape(B, S, D)
```

```python
import functools

import numpy as np
import jax
import jax.numpy as jnp
from jax import lax
from jax.experimental import pallas as pl
from jax.experimental.pallas import tpu as pltpu

D_MODEL = 1024
HEAD_DIM = 64
N_A_HEADS = 8
N_B_HEADS = 8
LANES = 128
SUBLANES = 8
PAIR_W = 2 * HEAD_DIM
Q_W = N_A_HEADS * HEAD_DIM
KV_W = 2 * HEAD_DIM
IN_WIDTH = 2 * (Q_W + 2 * KV_W)
BLOCK = 128
WINDOW = 128
GRID_W = 64
ROPE_THETA = 10000.0
ROPE_FREQS = HEAD_DIM // 4

PEER_HEADS = 8
PEER_NKEYS = 128
PEER_EXPERTS = PEER_NKEYS * PEER_NKEYS
PEER_DKEY = 256
PEER_TOPK = 16

LN_EPS = 1e-5
RMS_EPS = 1e-6
NEG_INF = -1e30

VMEM_LIMIT_BYTES = 56 * 1024 * 1024

BF16 = jnp.bfloat16
F32 = jnp.float32


def _alibi_slopes(n):
    return [float(2.0 ** (-((i + 1) * 8.0 / n))) for i in range(n)]


def _params(*sem):
    return pltpu.CompilerParams(dimension_semantics=sem,
                                vmem_limit_bytes=VMEM_LIMIT_BYTES)


def _pack_bf16_rows(w):
    r, c = w.shape
    pairs = w.astype(BF16).reshape(r // 2, 2, c).swapaxes(1, 2)
    return lax.bitcast_convert_type(pairs, jnp.uint32)


def _unpack_bf16(words):
    return pltpu.bitcast(words, BF16)


def _layer_norm(z, g, b):
    mu = jnp.mean(z, axis=-1, keepdims=True)
    zc = z - mu
    var = jnp.mean(zc * zc, axis=-1, keepdims=True)
    return zc * lax.rsqrt(var + LN_EPS) * g + b


def _half_rms(t, lo):
    sq = t * t
    s_lo = jnp.sum(jnp.where(lo, sq, 0.0), axis=-1, keepdims=True)
    s_hi = jnp.sum(jnp.where(lo, 0.0, sq), axis=-1, keepdims=True)
    return jnp.where(lo, lax.rsqrt(s_lo / HEAD_DIM + RMS_EPS),
                     lax.rsqrt(s_hi / HEAD_DIM + RMS_EPS))


def _ln_kernel(x_ref, g_ref, b_ref, o_ref):
    o_ref[...] = _layer_norm(x_ref[...], g_ref[...], b_ref[...])


def _ln_call(x, g, b, tm):
    T, D = x.shape
    return pl.pallas_call(
        _ln_kernel,
        grid=(T // tm,),
        in_specs=[pl.BlockSpec((tm, D), lambda i: (i, 0)),
                  pl.BlockSpec((1, D), lambda i: (0, 0)),
                  pl.BlockSpec((1, D), lambda i: (0, 0))],
        out_specs=pl.BlockSpec((tm, D), lambda i: (i, 0)),
        out_shape=jax.ShapeDtypeStruct((T, D), F32),
        compiler_params=_params("parallel"),
        name="ln_in",
    )(x, g, b)


def _proj_kernel(h_ref, w_ref, qg_ref, kg_ref, cos_ref, sa_ref, sb_ref,
                 qa_ref, ka_ref, va_ref, qb_ref, kb_ref, vb_ref):
    tm = h_ref.shape[0]
    y = jnp.dot(h_ref[...].astype(BF16), w_ref[...], preferred_element_type=F32)
    lo = lax.broadcasted_iota(jnp.int32, (tm, PAIR_W), 1) < HEAD_DIM
    cos, sa, sb = cos_ref[...], sa_ref[...], sb_ref[...]
    scale = HEAD_DIM ** -0.5

    def norm_rope(t, g):
        xn = t * _half_rms(t, lo) * g
        return xn * cos + pltpu.roll(xn, PAIR_W - ROPE_FREQS, 1) * sa + pltpu.roll(xn, ROPE_FREQS, 1) * sb

    for i in range(Q_W // PAIR_W):
        t = y[:, i * PAIR_W:(i + 1) * PAIR_W]
        qa_ref[:, i * PAIR_W:(i + 1) * PAIR_W] = (norm_rope(t, qg_ref[...]) * scale).astype(BF16)
    off = Q_W
    ka_ref[...] = norm_rope(y[:, off:off + KV_W], kg_ref[...]).astype(BF16)
    off += KV_W
    va_ref[...] = y[:, off:off + KV_W].astype(BF16)
    off += KV_W
    qb_ref[...] = (y[:, off:off + Q_W] * scale).astype(BF16)
    off += Q_W
    kb_ref[...] = y[:, off:off + KV_W].astype(BF16)
    off += KV_W
    vb_ref[...] = y[:, off:off + KV_W].astype(BF16)


def _proj_call(h, w, qg, kg, cos, sa, sb, S, tm):
    T, D = h.shape
    n_pos = S // tm
    row = lambda i: (i, 0)
    const = lambda i: (0, 0)
    pos = lambda i: (i % n_pos, 0)
    outs = [jax.ShapeDtypeStruct((T, w_), BF16) for w_ in (Q_W, KV_W, KV_W, Q_W, KV_W, KV_W)]
    return pl.pallas_call(
        _proj_kernel,
        grid=(T // tm,),
        in_specs=[pl.BlockSpec((tm, D), row),
                  pl.BlockSpec((D, IN_WIDTH), const),
                  pl.BlockSpec((1, PAIR_W), const),
                  pl.BlockSpec((1, PAIR_W), const),
                  pl.BlockSpec((tm, PAIR_W), pos),
                  pl.BlockSpec((tm, PAIR_W), pos),
                  pl.BlockSpec((tm, PAIR_W), pos)],
        out_specs=[pl.BlockSpec((tm, o.shape[1]), row) for o in outs],
        out_shape=outs,
        compiler_params=_params("parallel"),
        name="in_proj",
    )(h, w, qg, kg, cos, sa, sb)


def _attn_a_kernel(q_ref, k_ref, v_ref, g_ref, o_ref):
    tq = q_ref.shape[0]
    k = k_ref[...]
    v = v_ref[...]
    lo = lax.broadcasted_iota(jnp.int32, (tq, PAIR_W), 1) < HEAD_DIM
    nt = (((1,), (1,)), ((), ()))
    for i in range(Q_W // PAIR_W):
        qp = q_ref[:, i * PAIR_W:(i + 1) * PAIR_W]
        halves = []
        for grp in range(2):
            qz = jnp.where(lo if grp == 0 else jnp.logical_not(lo), qp, jnp.zeros_like(qp))
            s = lax.dot_general(qz, k, nt, preferred_element_type=F32)
            m = jnp.max(s, axis=-1, keepdims=True)
            p = jnp.exp(s - m)
            l = jnp.sum(p, axis=-1, keepdims=True)
            halves.append(jnp.dot(p.astype(BF16), v, preferred_element_type=F32) / l)
        o = jnp.where(lo, halves[0], halves[1])
        o = o * _half_rms(o, lo) * g_ref[:, i * PAIR_W:(i + 1) * PAIR_W]
        o_ref[:, i * PAIR_W:(i + 1) * PAIR_W] = o.astype(BF16)


def _attn_a_call(q, k, v, g, B, S, tq):
    T = q.shape[0]
    nq = S // tq
    return pl.pallas_call(
        _attn_a_kernel,
        grid=(B, nq),
        in_specs=[pl.BlockSpec((tq, Q_W), lambda b, i: (b * nq + i, 0)),
                  pl.BlockSpec((S, KV_W), lambda b, i: (b, 0)),
                  pl.BlockSpec((S, KV_W), lambda b, i: (b, 0)),
                  pl.BlockSpec((1, Q_W), lambda b, i: (0, 0))],
        out_specs=pl.BlockSpec((tq, Q_W), lambda b, i: (b * nq + i, 0)),
        out_shape=jax.ShapeDtypeStruct((T, Q_W), BF16),
        compiler_params=_params("parallel", "parallel"),
        name="attn_global",
    )(q, k, v, g)


def _attn_b_kernel(sink_ref, q_ref, k_ref, v_ref, g_ref, o_ref, *, slopes):
    j = pl.program_id(1)
    nb = k_ref.shape[0] // BLOCK
    span = BLOCK + 2 * WINDOW
    ks, vs = [], []
    for c in (-1, 0, 1):
        jc = j + c
        start = pl.multiple_of(jnp.clip(jc, 0, nb - 1) * BLOCK, BLOCK)
        ks.append(k_ref[pl.ds(start, BLOCK), :])
        vs.append(v_ref[pl.ds(start, BLOCK), :])
    kslab = jnp.concatenate(ks, axis=0)
    vslab = jnp.concatenate(vs, axis=0)
    qi = lax.broadcasted_iota(jnp.int32, (BLOCK, span), 0)
    ki = lax.broadcasted_iota(jnp.int32, (BLOCK, span), 1)
    dist = jnp.abs(qi + WINDOW - ki)
    s_pos = (j - 1) * BLOCK + ki
    in_seq = jnp.logical_and(s_pos >= 0, s_pos < nb * BLOCK)
    valid = jnp.logical_and(dist <= WINDOW, in_seq)
    distf = dist.astype(F32)

    lo = lax.broadcasted_iota(jnp.int32, (BLOCK, PAIR_W), 1) < HEAD_DIM
    hi = jnp.logical_not(lo)
    n_pairs = Q_W // PAIR_W
    heads = [(i, grp) for grp in range(2) for i in range(n_pairs)]
    qs = []
    for i, grp in heads:
        qp = q_ref[:, i * PAIR_W:(i + 1) * PAIR_W]
        qs.append(jnp.where(lo if grp == 0 else hi, qp, jnp.zeros_like(qp)))
    s_all = lax.dot_general(jnp.concatenate(qs, axis=0), kslab, (((1,), (1,)), ((), ())),
                            preferred_element_type=F32)
    ps, inv = [], []
    for n, (i, grp) in enumerate(heads):
        hd = i + n_pairs * grp
        s = s_all[n * BLOCK:(n + 1) * BLOCK] - slopes[hd] * distf
        s = jnp.where(valid, s, NEG_INF)
        sink = sink_ref[hd]
        m = jnp.maximum(jnp.max(s, axis=-1, keepdims=True), sink)
        p = jnp.exp(s - m)
        inv.append(1.0 / (jnp.sum(p, axis=-1, keepdims=True) + jnp.exp(sink - m)))
        ps.append(p.astype(BF16))
    o_all = jnp.dot(jnp.concatenate(ps, axis=0), vslab, preferred_element_type=F32)
    for i in range(n_pairs):
        o0 = o_all[i * BLOCK:(i + 1) * BLOCK] * inv[i]
        o1 = o_all[(i + n_pairs) * BLOCK:(i + n_pairs + 1) * BLOCK] * inv[i + n_pairs]
        o = jnp.where(lo, o0, o1)
        o = o * _half_rms(o, lo) * g_ref[:, i * PAIR_W:(i + 1) * PAIR_W]
        o_ref[:, i * PAIR_W:(i + 1) * PAIR_W] = o.astype(BF16)


def _attn_b_call(sink, q, k, v, g, B, S):
    T = q.shape[0]
    nb = S // BLOCK
    kern = functools.partial(_attn_b_kernel, slopes=_alibi_slopes(N_B_HEADS))
    return pl.pallas_call(
        kern,
        grid=(B, nb),
        in_specs=[pl.BlockSpec(memory_space=pltpu.SMEM),
                  pl.BlockSpec((BLOCK, Q_W), lambda b, i: (b * nb + i, 0)),
                  pl.BlockSpec((S, KV_W), lambda b, i: (b, 0)),
                  pl.BlockSpec((S, KV_W), lambda b, i: (b, 0)),
                  pl.BlockSpec((1, Q_W), lambda b, i: (0, 0))],
        out_specs=pl.BlockSpec((BLOCK, Q_W), lambda b, i: (b * nb + i, 0)),
        out_shape=jax.ShapeDtypeStruct((T, Q_W), BF16),
        compiler_params=_params("parallel", "parallel"),
        name="attn_window",
    )(sink, q, k, v, g)


def _oproj_kernel(oa_ref, ob_ref, wa_ref, wb_ref, h_ref, g_ref, b_ref, o_ref, *, alpha):
    mix = jnp.dot(oa_ref[...], wa_ref[...], preferred_element_type=F32)
    mix = mix + jnp.dot(ob_ref[...], wb_ref[...], preferred_element_type=F32)
    o_ref[...] = _layer_norm(alpha * h_ref[...] + mix, g_ref[...], b_ref[...])


def _oproj_call(oa, ob, wa, wb, h, g, b, alpha, tm):
    T, D = h.shape
    row = lambda i: (i, 0)
    const = lambda i: (0, 0)
    return pl.pallas_call(
        functools.partial(_oproj_kernel, alpha=alpha),
        grid=(T // tm,),
        in_specs=[pl.BlockSpec((tm, Q_W), row), pl.BlockSpec((tm, Q_W), row),
                  pl.BlockSpec((Q_W, D), const), pl.BlockSpec((Q_W, D), const),
                  pl.BlockSpec((tm, D), row),
                  pl.BlockSpec((1, D), const), pl.BlockSpec((1, D), const)],
        out_specs=pl.BlockSpec((tm, D), row),
        out_shape=jax.ShapeDtypeStruct((T, D), F32),
        compiler_params=_params("parallel"),
        name="out_proj_ln",
    )(oa, ob, wa, wb, h, g, b)


def _peer_scores_kernel(h_ref, wq_ref, keys_ref, st_ref):
    q = jnp.dot(h_ref[...].astype(BF16), wq_ref[...], preferred_element_type=F32).astype(BF16)
    half = PEER_DKEY // 2
    for n in range(2 * PEER_HEADS):
        qs = q[:, n * half:(n + 1) * half]
        st_ref[n] = lax.dot_general(keys_ref[n], qs, (((1,), (1,)), ((), ())),
                                    preferred_element_type=F32)


def _peer_scores_call(h, wq, keys, tm):
    T, D = h.shape
    n_sets = 2 * PEER_HEADS
    half = PEER_DKEY // 2
    return pl.pallas_call(
        _peer_scores_kernel,
        grid=(T // tm,),
        in_specs=[pl.BlockSpec((tm, D), lambda i: (i, 0)),
                  pl.BlockSpec((D, PEER_HEADS * PEER_DKEY), lambda i: (0, 0)),
                  pl.BlockSpec((n_sets, PEER_NKEYS, half), lambda i: (0, 0, 0))],
        out_specs=pl.BlockSpec((n_sets, PEER_NKEYS, tm), lambda i: (0, 0, i)),
        out_shape=jax.ShapeDtypeStruct((n_sets, PEER_NKEYS, T), F32),
        compiler_params=_params("parallel"),
        name="peer_scores",
    )(h, wq, keys)


ROUTE_TOKENS = SUBLANES * LANES
ROUTE_PASS = 4
ROUTE_PAIRS = [(r1, r2) for r1 in range(PEER_TOPK) for r2 in range(PEER_TOPK)
               if (r1 + 1) * (r2 + 1) <= PEER_TOPK]


def _slab(i):
    if isinstance(i, int):
        return pl.ds(i * SUBLANES, SUBLANES)
    return pl.ds(pl.multiple_of(i * SUBLANES, SUBLANES), SUBLANES)


def _insert_sorted(tops, x):
    out = []
    for t in tops[:-1]:
        out.append(jnp.maximum(t, x))
        x = jnp.minimum(t, x)
    out.append(jnp.maximum(tops[-1], x))
    return out


def _router_kernel(st_ref, r1_ref, c1_ref, n2_ref, e2_ref,
                   s1_ref, s2_ref, a1_ref, a2_ref, rk1_ref, n2k_ref, z_ref, bad_ref,
                   cand_ref, n2r_ref, out_ref):
    K = PEER_TOPK
    neg = -jnp.inf
    shape = (SUBLANES, LANES)
    ones = jnp.ones(shape, F32)
    zeros = jnp.zeros(shape, F32)

    def load_keys_in(src_set, dst_ref):
        for g in range(SUBLANES):
            dst_ref[pl.ds(g, PEER_NKEYS, stride=SUBLANES), :] = st_ref[src_set, :, g * LANES:(g + 1) * LANES]

    def store_keys_out(dst_ref, h):
        for g in range(SUBLANES):
            dst_ref[h, :, g * LANES:(g + 1) * LANES] = out_ref[pl.ds(g, PEER_NKEYS, stride=SUBLANES), :]

    def top_values(value_of, n, a_ref):
        thr = None
        for p in range(K // ROUTE_PASS):
            groups = [[jnp.full(shape, neg, F32)] * ROUTE_PASS for _ in range(2)]
            for k in range(n):
                x = value_of(k)
                if thr is not None:
                    x = jnp.where(x < thr, x, neg)
                groups[k % 2] = _insert_sorted(groups[k % 2], x)
            tops = groups[0]
            for x in groups[1]:
                tops = _insert_sorted(tops, x)
            for i, t in enumerate(tops):
                a_ref[_slab(p * ROUTE_PASS + i), :] = t
            thr = tops[-1]

    def adjacent_equal(a_ref):
        bad = zeros
        prev = a_ref[_slab(0), :]
        for r in range(1, K):
            cur = a_ref[_slab(r), :]
            bad = jnp.where(cur == prev, ones, bad)
            prev = cur
        return bad

    def cand(r1, r2):
        return a1_ref[_slab(r1), :] + a2_ref[_slab(r2), :]

    def finish_candidates(selected):
        top = cand(0, 0)
        z = zeros
        counts = [zeros] * K
        for idx, (r1, r2) in enumerate(ROUTE_PAIRS):
            s = selected(idx)
            z = z + s * jnp.exp(cand(r1, r2) - top)
            counts[r2] = counts[r2] + s
        z_ref[...] = z
        for r2 in range(K):
            n2r_ref[_slab(r2), :] = counts[r2]

    def head(h, carry):
        load_keys_in(2 * h, s1_ref)
        load_keys_in(2 * h + 1, s2_ref)

        top_values(lambda k: s1_ref[_slab(k), :], PEER_NKEYS, a1_ref)
        top_values(lambda k: s2_ref[_slab(k), :], PEER_NKEYS, a2_ref)
        a1 = [a1_ref[_slab(r), :] for r in range(K)]
        n_ge1 = zeros
        for k in range(PEER_NKEYS):
            s = s1_ref[_slab(k), :]
            rank = jnp.full(shape, float(K), F32)
            for r in reversed(range(K)):
                rank = jnp.where(s >= a1[r], float(r), rank)
            rk1_ref[_slab(k), :] = rank
            n_ge1 = n_ge1 + jnp.where(rank < float(K), ones, zeros)
        top_values(lambda i: cand(*ROUTE_PAIRS[i]), len(ROUTE_PAIRS), cand_ref)
        tau = cand_ref[_slab(K - 1), :]
        finish_candidates(lambda i: jnp.where(cand(*ROUTE_PAIRS[i]) >= tau, ones, zeros))
        a2 = [a2_ref[_slab(r), :] for r in range(K)]
        counts = [n2r_ref[_slab(r), :] for r in range(K)]
        n_sel = zeros
        for r in range(K):
            n_sel = n_sel + counts[r]
        n_ge2 = zeros
        for k in range(PEER_NKEYS):
            s = s2_ref[_slab(k), :]
            v = zeros
            for r in reversed(range(K)):
                v = jnp.where(s >= a2[r], counts[r], v)
            n2k_ref[_slab(k), :] = v
            n_ge2 = n_ge2 + jnp.where(s >= a2[K - 1], ones, zeros)
        bad = adjacent_equal(a1_ref) + adjacent_equal(a2_ref) + adjacent_equal(cand_ref)
        bad = bad + jnp.where(n_ge1 == float(K), zeros, ones) + jnp.where(n_ge2 == float(K), zeros, ones)
        bad = bad + jnp.where(n_sel == float(K), zeros, ones)

        bad_ref[...] = bad

        top1 = a1_ref[_slab(0), :]
        top2 = a2_ref[_slab(0), :]
        inv_z = 1.0 / z_ref[...]
        for k in range(PEER_NKEYS):
            out_ref[_slab(k), :] = rk1_ref[_slab(k), :]
        store_keys_out(r1_ref, h)
        for k in range(PEER_NKEYS):
            out_ref[_slab(k), :] = jnp.exp(s1_ref[_slab(k), :] - top1) * inv_z
        store_keys_out(c1_ref, h)
        for k in range(PEER_NKEYS):
            out_ref[_slab(k), :] = n2k_ref[_slab(k), :]
        store_keys_out(n2_ref, h)
        for k in range(PEER_NKEYS):
            out_ref[_slab(k), :] = jnp.exp(s2_ref[_slab(k), :] - top2)
        store_keys_out(e2_ref, h)

        def fix_group(g, carry2):
            @pl.when(jnp.sum(bad_ref[pl.ds(g, 1), :]) > 0.0)
            def _():
                _route_group_exact(st_ref, r1_ref, c1_ref, n2_ref, e2_ref, h,
                                   pl.ds(pl.multiple_of(g * LANES, LANES), LANES))
            return carry2

        @pl.when(jnp.sum(bad) > 0.0)
        def _():
            lax.fori_loop(0, SUBLANES, fix_group, 0)

        return carry

    lax.fori_loop(0, PEER_HEADS, head, 0)


def _route_group_exact(st_ref, r1_ref, c1_ref, n2_ref, e2_ref, h, lanes):
    tt = LANES
    K = PEER_TOPK
    n_lead = 4
    neg = -jnp.inf
    kio_i = lax.broadcasted_iota(jnp.int32, (PEER_NKEYS, tt), 0)
    rio = lax.broadcasted_iota(jnp.int32, (K, tt), 0)
    blk = kio_i // K
    rr = kio_i % K
    pos = jnp.where(blk < n_lead, rr * K + blk, (blk - n_lead) * K + rr).astype(F32)
    dup = jnp.logical_and(blk >= n_lead, rr < n_lead)
    kio = kio_i.astype(F32)

    def top16(s):
        rank = jnp.full((PEER_NKEYS, tt), float(K), F32)
        arr = jnp.zeros((K, tt), F32)
        for r in range(K):
            m = jnp.max(s, axis=0, keepdims=True)
            first = jnp.min(jnp.where(s == m, kio, float(PEER_NKEYS)), axis=0, keepdims=True)
            hit = kio == first
            rank = jnp.where(hit, float(r), rank)
            s = jnp.where(hit, neg, s)
            arr = jnp.where(rio == r, m, arr)
        return rank, arr

    s1 = st_ref[2 * h, :, lanes]
    s2 = st_ref[2 * h + 1, :, lanes]
    rank1, a1 = top16(s1)
    rank2, a2 = top16(s2)
    blocks = ([a1 + a2[r2:r2 + 1] for r2 in range(n_lead)]
              + [a1[r1:r1 + 1] + a2 for r1 in range(n_lead)])
    c0 = jnp.where(dup, neg, jnp.concatenate(blocks, axis=0))
    c = c0
    sel = jnp.zeros((PEER_NKEYS, tt), F32)
    for _ in range(K):
        m = jnp.max(c, axis=0, keepdims=True)
        first = jnp.min(jnp.where(c == m, pos, float(K * K)), axis=0, keepdims=True)
        hit = pos == first
        sel = jnp.where(hit, 1.0, sel)
        c = jnp.where(hit, neg, c)
    top = a1[0:1] + a2[0:1]
    z = jnp.sum(jnp.where(sel > 0.0, jnp.exp(c0 - top), 0.0), axis=0, keepdims=True)
    tail = sel[n_lead * K:(n_lead + 1) * K]
    for r1 in range(1, n_lead):
        tail = tail + sel[(n_lead + r1) * K:(n_lead + r1 + 1) * K]
    n2 = jnp.zeros((PEER_NKEYS, tt), F32)
    for r2 in range(K):
        if r2 < n_lead:
            cnt = jnp.sum(sel[r2 * K:(r2 + 1) * K], axis=0, keepdims=True)
        else:
            cnt = tail[r2:r2 + 1]
        n2 = jnp.where(rank2 == float(r2), cnt, n2)
    r1_ref[h, :, lanes] = rank1
    c1_ref[h, :, lanes] = jnp.exp(s1 - a1[0:1]) / z
    n2_ref[h, :, lanes] = n2
    e2_ref[h, :, lanes] = jnp.exp(s2 - a2[0:1])


def _router_call(st):
    T = st.shape[-1]
    tt = ROUTE_TOKENS
    shape = (PEER_HEADS, PEER_NKEYS, T)
    outs = [jax.ShapeDtypeStruct(shape, F32)] * 4
    spec = pl.BlockSpec((PEER_HEADS, PEER_NKEYS, tt), lambda i: (0, 0, i))
    keys = pltpu.VMEM((PEER_NKEYS * SUBLANES, LANES), F32)
    tops = pltpu.VMEM((PEER_TOPK * SUBLANES, LANES), F32)
    one = pltpu.VMEM((SUBLANES, LANES), F32)
    return pl.pallas_call(
        _router_kernel,
        grid=(T // tt,),
        in_specs=[pl.BlockSpec((2 * PEER_HEADS, PEER_NKEYS, tt), lambda i: (0, 0, i))],
        out_specs=[spec] * 4,
        out_shape=outs,
        scratch_shapes=[keys, keys, tops, tops, keys, keys, one, one, tops, tops, keys],
        compiler_params=_params("parallel"),
        name="peer_router",
    )(st)


PEER_CHUNK_ROWS = 256
PEER_IB_GROUP = 2
PEER_PART_ROWS = 64


def _peer_dense_kernel(h_ref, r1_ref, c1_ref, n2_ref, e2_ref, u_first_ref, u_a_ref, u_b_ref,
                       vt_a_ref, vt_b_ref, vt_last_ref, g_ref, b_ref, o_ref,
                       xt_ref, acc_ref, n2s_ref, e2s_ref, act_a_ref, act_b_ref, gt_a_ref, gt_b_ref,
                       *, alpha):
    j = pl.program_id(1)
    tt = h_ref.shape[0]
    te = vt_a_ref.shape[1]
    n_sub = te // PEER_NKEYS
    tile = (PEER_PART_ROWS, LANES)
    sqrt_half = float(np.sqrt(0.5))

    def gelu_bf16(a):
        return (a * (0.5 + 0.5 * lax.erf(a * sqrt_half))).astype(BF16)

    @pl.when(j == 0)
    def _():
        xt_ref[:, :tt] = h_ref[...].T.astype(BF16)
        acc_ref[:, :tt] = jnp.zeros((acc_ref.shape[0], tt), F32)
        n2s_ref[:, :, :tt] = n2_ref[...].astype(BF16)
        e2s_ref[:, :, :tt] = e2_ref[...].astype(BF16)
        gt_b_ref[:, :tt] = jnp.zeros((te, tt), BF16)
        act_a_ref[:, :tt] = gelu_bf16(jnp.dot(_unpack_bf16(u_first_ref[...]), xt_ref[:, :tt],
                                              preferred_element_type=F32))

    def phase(u_next_ref, act_next_ref, act_cur_ref, gt_cur_ref, vt_prev_ref, gt_prev_ref, key_off):
        def chunk(c, carry):
            rows = pl.ds(pl.multiple_of(c * PEER_CHUNK_ROWS, PEER_CHUNK_ROWS), PEER_CHUNK_ROWS)
            prows = pl.ds(pl.multiple_of(c * (PEER_CHUNK_ROWS // 2), PEER_CHUNK_ROWS // 2),
                          PEER_CHUNK_ROWS // 2)
            n_ib = PEER_CHUNK_ROWS // PEER_NKEYS
            for grp in range(n_ib // PEER_IB_GROUP):
                ibs = [c * n_ib + grp * PEER_IB_GROUP + s for s in range(PEER_IB_GROUP)]
                r1_rows = [[r1_ref[hd, pl.ds(key_off + ib, 1), :] for hd in range(PEER_HEADS)]
                           for ib in ibs]
                c1_rows = [[c1_ref[hd, pl.ds(key_off + ib, 1), :] for hd in range(PEER_HEADS)]
                           for ib in ibs]
                for tg in range(tt // LANES):
                    lanes = slice(tg * LANES, (tg + 1) * LANES)
                    for part in range(PEER_NKEYS // PEER_PART_ROWS):
                        krows = slice(part * PEER_PART_ROWS, (part + 1) * PEER_PART_ROWS)
                        ws = [None] * PEER_IB_GROUP
                        for hd in range(PEER_HEADS):
                            n2t = n2s_ref[hd, krows, lanes]
                            e2t = e2s_ref[hd, krows, lanes]
                            for s in range(PEER_IB_GROUP):
                                r1 = jnp.broadcast_to(r1_rows[s][hd][:, lanes], tile).astype(BF16)
                                c1 = jnp.broadcast_to(c1_rows[s][hd][:, lanes], tile).astype(BF16)
                                term = jnp.where(r1 < n2t, e2t, jnp.zeros(tile, BF16)) * c1
                                ws[s] = term if ws[s] is None else ws[s] + term
                        for s in range(PEER_IB_GROUP):
                            erows = pl.ds(pl.multiple_of(ibs[s] * PEER_NKEYS + part * PEER_PART_ROWS,
                                                         PEER_PART_ROWS), PEER_PART_ROWS)
                            gt_cur_ref[erows, lanes] = act_cur_ref[erows, lanes] * ws[s]
            act_next_ref[rows, :tt] = gelu_bf16(jnp.dot(_unpack_bf16(u_next_ref[prows, :]),
                                                        xt_ref[:, :tt], preferred_element_type=F32))
            acc_ref[rows, :tt] += jnp.dot(_unpack_bf16(vt_prev_ref[prows, :]), gt_prev_ref[:, :tt],
                                        preferred_element_type=F32)
            return carry

        lax.fori_loop(0, te // PEER_CHUNK_ROWS, chunk, 0)

    phase(u_a_ref, act_b_ref, act_a_ref, gt_a_ref, vt_a_ref, gt_b_ref, 0)
    phase(u_b_ref, act_a_ref, act_b_ref, gt_b_ref, vt_b_ref, gt_a_ref, n_sub)

    @pl.when(j == pl.num_programs(1) - 1)
    def _():
        acc = acc_ref[:, :tt] + jnp.dot(_unpack_bf16(vt_last_ref[...]), gt_b_ref[:, :tt],
                                     preferred_element_type=F32)
        z = alpha * h_ref[...] + acc.T
        o_ref[...] = _layer_norm(z, g_ref[...], b_ref[...])


def _peer_dense_call(h, r1, c1, n2, e2, u, vt, g, b, alpha, tt, te):
    T, D = h.shape
    n_sub = te // PEER_NKEYS
    n_blk = PEER_EXPERTS // te
    n_steps = n_blk // 2
    assert D == te and n_blk % 2 == 0
    spec1 = pl.BlockSpec((PEER_HEADS, 2 * n_sub, tt), lambda i, j: (0, j, i))
    spec2 = pl.BlockSpec((PEER_HEADS, PEER_NKEYS, tt), lambda i, j: (0, 0, i))
    last = n_blk - 1
    ttp = tt + LANES
    u_spec = lambda f: pl.BlockSpec((te // 2, D), lambda i, j: (f(j), 0))
    vt_spec = lambda f: pl.BlockSpec((D // 2, te), lambda i, j: (0, f(j)))
    return pl.pallas_call(
        functools.partial(_peer_dense_kernel, alpha=alpha),
        grid=(T // tt, n_steps),
        in_specs=[pl.BlockSpec((tt, D), lambda i, j: (i, 0)),
                  spec1, spec1, spec2, spec2,
                  u_spec(lambda j: 0),
                  u_spec(lambda j: 2 * j + 1),
                  u_spec(lambda j: jnp.minimum(2 * j + 2, last)),
                  vt_spec(lambda j: jnp.maximum(2 * j - 1, 0)),
                  vt_spec(lambda j: 2 * j),
                  vt_spec(lambda j: last),
                  pl.BlockSpec((1, D), lambda i, j: (0, 0)),
                  pl.BlockSpec((1, D), lambda i, j: (0, 0))],
        out_specs=pl.BlockSpec((tt, D), lambda i, j: (i, 0)),
        out_shape=jax.ShapeDtypeStruct((T, D), F32),
        scratch_shapes=[pltpu.VMEM((D, ttp), BF16), pltpu.VMEM((D, ttp), F32),
                        pltpu.VMEM((PEER_HEADS, PEER_NKEYS, ttp), BF16),
                        pltpu.VMEM((PEER_HEADS, PEER_NKEYS, ttp), BF16),
                        pltpu.VMEM((te, ttp), BF16), pltpu.VMEM((te, ttp), BF16),
                        pltpu.VMEM((te, ttp), BF16), pltpu.VMEM((te, ttp), BF16)],
        compiler_params=_params("parallel", "arbitrary"),
        name="peer_dense",
    )(h, r1, c1, n2, e2, u, u, u, vt, vt, vt, g, b)


def _pair_perm():
    idx = []
    for i in range(N_A_HEADS // 2):
        for grp in range(2):
            hd = i + (N_A_HEADS // 2) * grp
            idx.extend(range(hd * HEAD_DIM, (hd + 1) * HEAD_DIM))
    return np.asarray(idx, dtype=np.int32)


def _rope_tables(S):
    pos = np.arange(S)
    row = (pos // GRID_W).astype(np.float32)
    col = (pos % GRID_W).astype(np.float32)
    inv_freq = jnp.asarray(ROPE_THETA, F32) ** (-jnp.arange(ROPE_FREQS, dtype=F32) / ROPE_FREQS)
    ang_r = jnp.asarray(row)[:, None] * inv_freq
    ang_c = jnp.asarray(col)[:, None] * inv_freq
    cr, sr, cc, sc = jnp.cos(ang_r), jnp.sin(ang_r), jnp.cos(ang_c), jnp.sin(ang_c)
    zero = jnp.zeros_like(sr)
    cos = jnp.concatenate([cr, cr, cc, cc], axis=-1)
    sa = jnp.concatenate([-sr, zero, -sc, zero], axis=-1)
    sb = jnp.concatenate([zero, sr, zero, sc], axis=-1)
    tile2 = lambda t: jnp.concatenate([t, t], axis=-1)
    return tile2(cos), tile2(sa), tile2(sb)


def kernel(x, ln_in_g, ln_in_b, w_in, qn_g, kn_g, sink, gn_a_g, gn_b_g, w_o, ln1_g, ln1_b,
           peer_wq, peer_keys, peer_u, peer_v, ln2_g, ln2_b):
    B, S, D = x.shape
    depth = w_in.shape[0]
    T = B * S
    alpha = float((2.0 * depth) ** 0.25)
    tm = min(512, S)
    tq = min(256, S)
    tt_dense = min(512, T)
    te = 1024

    perm = _pair_perm()
    col_perm = np.concatenate([perm, Q_W + np.arange(2 * KV_W, dtype=np.int32),
                               Q_W + 2 * KV_W + perm,
                               2 * Q_W + 2 * KV_W + np.arange(2 * KV_W, dtype=np.int32)])
    cos, sa, sb = _rope_tables(S)
    row2 = lambda v: v.reshape(1, -1)
    tile2 = lambda v: jnp.concatenate([v, v]).reshape(1, -1)

    h = _ln_call(x.reshape(T, D), row2(ln_in_g), row2(ln_in_b), tm)
    for l in range(depth):
        w = w_in[l][:, col_perm].astype(BF16)
        qa, ka, va, qb, kb, vb = _proj_call(h, w, tile2(qn_g[l]), tile2(kn_g[l]), cos, sa, sb, S, tm)
        oa = _attn_a_call(qa, ka, va, row2(gn_a_g[l][perm]), B, S, tq)
        ob = _attn_b_call(sink[l], qb, kb, vb, row2(gn_b_g[l][perm]), B, S)
        wa = w_o[l][:Q_W][perm].astype(BF16)
        wb = w_o[l][Q_W:][perm].astype(BF16)
        h = _oproj_call(oa, ob, wa, wb, h, row2(ln1_g[l]), row2(ln1_b[l]), alpha, tm)
        keys = peer_keys[l].reshape(2 * PEER_HEADS, PEER_NKEYS, PEER_DKEY // 2).astype(BF16)
        st = _peer_scores_call(h, peer_wq[l].astype(BF16), keys, tm)
        r1, c1, n2, e2 = _router_call(st)
        h = _peer_dense_call(h, r1, c1, n2, e2, _pack_bf16_rows(peer_u[l]), _pack_bf16_rows(peer_v[l].T),
                             row2(ln2_g[l]), row2(ln2_b[l]), alpha, tt_dense, te)
    return h.reshape(B, S, D)
```

```python
import functools

import numpy as np
import jax
import jax.numpy as jnp
from jax import lax
from jax.experimental import pallas as pl
from jax.experimental.pallas import tpu as pltpu

D_MODEL = 1024
HEAD_DIM = 64
N_A_HEADS = 8
N_B_HEADS = 8
LANES = 128
SUBLANES = 8
PAIR_W = 2 * HEAD_DIM
Q_W = N_A_HEADS * HEAD_DIM
KV_W = 2 * HEAD_DIM
IN_WIDTH = 2 * (Q_W + 2 * KV_W)
BLOCK = 128
WINDOW = 128
GRID_W = 64
ROPE_THETA = 10000.0
ROPE_FREQS = HEAD_DIM // 4

PEER_HEADS = 8
PEER_NKEYS = 128
PEER_EXPERTS = PEER_NKEYS * PEER_NKEYS
PEER_DKEY = 256
PEER_TOPK = 16

LN_EPS = 1e-5
RMS_EPS = 1e-6
NEG_INF = -1e30

VMEM_LIMIT_BYTES = 56 * 1024 * 1024

BF16 = jnp.bfloat16
F32 = jnp.float32


def _alibi_slopes(n):
    return [float(2.0 ** (-((i + 1) * 8.0 / n))) for i in range(n)]


def _params(*sem):
    return pltpu.CompilerParams(dimension_semantics=sem,
                                vmem_limit_bytes=VMEM_LIMIT_BYTES)


def _pack_kernel(w_ref, o_ref, *, transpose):
    w = w_ref[...]
    if transpose:
        w = w.T
    o_ref[...] = pltpu.bitcast(w.astype(BF16), jnp.uint32)


def _pack_bf16_rows(w, transpose=False):
    n, c = w.shape
    blk = 1024
    assert n % blk == 0 and c % blk == 0
    if transpose:
        grid, out_map = (n // blk, c // blk), (lambda i, j: (j, i))
        out_shape = (c // 2, n)
    else:
        grid, out_map = (n // blk, c // blk), (lambda i, j: (i, j))
        out_shape = (n // 2, c)
    return pl.pallas_call(
        functools.partial(_pack_kernel, transpose=transpose),
        grid=grid,
        in_specs=[pl.BlockSpec((blk, blk), lambda i, j: (i, j))],
        out_specs=pl.BlockSpec((blk // 2, blk), out_map),
        out_shape=jax.ShapeDtypeStruct(out_shape, jnp.uint32),
        compiler_params=_params("parallel", "parallel"),
        name="pack_bf16",
    )(w)


def _unpack_bf16(words):
    return pltpu.bitcast(words, BF16)


def _layer_norm(z, g, b):
    mu = jnp.mean(z, axis=-1, keepdims=True)
    zc = z - mu
    var = jnp.mean(zc * zc, axis=-1, keepdims=True)
    return zc * lax.rsqrt(var + LN_EPS) * g + b


def _half_rms(t, lo):
    sq = t * t
    s_lo = jnp.sum(jnp.where(lo, sq, 0.0), axis=-1, keepdims=True)
    s_hi = jnp.sum(jnp.where(lo, 0.0, sq), axis=-1, keepdims=True)
    return jnp.where(lo, lax.rsqrt(s_lo / HEAD_DIM + RMS_EPS),
                     lax.rsqrt(s_hi / HEAD_DIM + RMS_EPS))


def _ln_kernel(x_ref, g_ref, b_ref, o_ref):
    o_ref[...] = _layer_norm(x_ref[...], g_ref[...], b_ref[...])


def _ln_call(x, g, b, tm):
    T, D = x.shape
    return pl.pallas_call(
        _ln_kernel,
        grid=(T // tm,),
        in_specs=[pl.BlockSpec((tm, D), lambda i: (i, 0)),
                  pl.BlockSpec((1, D), lambda i: (0, 0)),
                  pl.BlockSpec((1, D), lambda i: (0, 0))],
        out_specs=pl.BlockSpec((tm, D), lambda i: (i, 0)),
        out_shape=jax.ShapeDtypeStruct((T, D), F32),
        compiler_params=_params("parallel"),
        name="ln_in",
    )(x, g, b)


def _proj_kernel(h_ref, w_ref, qg_ref, kg_ref, cos_ref, sa_ref, sb_ref,
                 qa_ref, ka_ref, va_ref, qb_ref, kb_ref, vb_ref):
    tm = h_ref.shape[0]
    y = jnp.dot(h_ref[...].astype(BF16), w_ref[...], preferred_element_type=F32)
    lo = lax.broadcasted_iota(jnp.int32, (tm, PAIR_W), 1) < HEAD_DIM
    cos, sa, sb = cos_ref[...], sa_ref[...], sb_ref[...]
    scale = HEAD_DIM ** -0.5

    def norm_rope(t, g):
        xn = t * _half_rms(t, lo) * g
        return xn * cos + pltpu.roll(xn, PAIR_W - ROPE_FREQS, 1) * sa + pltpu.roll(xn, ROPE_FREQS, 1) * sb

    for i in range(Q_W // PAIR_W):
        t = y[:, i * PAIR_W:(i + 1) * PAIR_W]
        qa_ref[:, i * PAIR_W:(i + 1) * PAIR_W] = (norm_rope(t, qg_ref[...]) * scale).astype(BF16)
    off = Q_W
    ka_ref[...] = norm_rope(y[:, off:off + KV_W], kg_ref[...]).astype(BF16)
    off += KV_W
    va_ref[...] = y[:, off:off + KV_W].astype(BF16)
    off += KV_W
    qb_ref[...] = (y[:, off:off + Q_W] * scale).astype(BF16)
    off += Q_W
    kb_ref[...] = y[:, off:off + KV_W].astype(BF16)
    off += KV_W
    vb_ref[...] = y[:, off:off + KV_W].astype(BF16)


def _proj_call(h, w, qg, kg, cos, sa, sb, S, tm):
    T, D = h.shape
    n_pos = S // tm
    row = lambda i: (i, 0)
    const = lambda i: (0, 0)
    pos = lambda i: (i % n_pos, 0)
    outs = [jax.ShapeDtypeStruct((T, w_), BF16) for w_ in (Q_W, KV_W, KV_W, Q_W, KV_W, KV_W)]
    return pl.pallas_call(
        _proj_kernel,
        grid=(T // tm,),
        in_specs=[pl.BlockSpec((tm, D), row),
                  pl.BlockSpec((D, IN_WIDTH), const),
                  pl.BlockSpec((1, PAIR_W), const),
                  pl.BlockSpec((1, PAIR_W), const),
                  pl.BlockSpec((tm, PAIR_W), pos),
                  pl.BlockSpec((tm, PAIR_W), pos),
                  pl.BlockSpec((tm, PAIR_W), pos)],
        out_specs=[pl.BlockSpec((tm, o.shape[1]), row) for o in outs],
        out_shape=outs,
        compiler_params=_params("parallel"),
        name="in_proj",
    )(h, w, qg, kg, cos, sa, sb)


def _attn_a_kernel(q_ref, k_ref, v_ref, g_ref, o_ref):
    tq = q_ref.shape[0]
    k = k_ref[...]
    v = v_ref[...]
    lo = lax.broadcasted_iota(jnp.int32, (tq, PAIR_W), 1) < HEAD_DIM
    nt = (((1,), (1,)), ((), ()))
    for i in range(Q_W // PAIR_W):
        qp = q_ref[:, i * PAIR_W:(i + 1) * PAIR_W]
        halves = []
        for grp in range(2):
            qz = jnp.where(lo if grp == 0 else jnp.logical_not(lo), qp, jnp.zeros_like(qp))
            s = lax.dot_general(qz, k, nt, preferred_element_type=F32)
            m = jnp.max(s, axis=-1, keepdims=True)
            p = jnp.exp(s - m)
            l = jnp.sum(p, axis=-1, keepdims=True)
            halves.append(jnp.dot(p.astype(BF16), v, preferred_element_type=F32) / l)
        o = jnp.where(lo, halves[0], halves[1])
        o = o * _half_rms(o, lo) * g_ref[:, i * PAIR_W:(i + 1) * PAIR_W]
        o_ref[:, i * PAIR_W:(i + 1) * PAIR_W] = o.astype(BF16)


def _attn_a_call(q, k, v, g, B, S, tq):
    T = q.shape[0]
    nq = S // tq
    return pl.pallas_call(
        _attn_a_kernel,
        grid=(B, nq),
        in_specs=[pl.BlockSpec((tq, Q_W), lambda b, i: (b * nq + i, 0)),
                  pl.BlockSpec((S, KV_W), lambda b, i: (b, 0)),
                  pl.BlockSpec((S, KV_W), lambda b, i: (b, 0)),
                  pl.BlockSpec((1, Q_W), lambda b, i: (0, 0))],
        out_specs=pl.BlockSpec((tq, Q_W), lambda b, i: (b * nq + i, 0)),
        out_shape=jax.ShapeDtypeStruct((T, Q_W), BF16),
        compiler_params=_params("parallel", "parallel"),
        name="attn_global",
    )(q, k, v, g)


def _attn_b_kernel(sink_ref, q_ref, k_ref, v_ref, g_ref, o_ref, *, slopes):
    j = pl.program_id(1)
    nb = k_ref.shape[0] // BLOCK
    span = BLOCK + 2 * WINDOW
    ks, vs = [], []
    for c in (-1, 0, 1):
        jc = j + c
        start = pl.multiple_of(jnp.clip(jc, 0, nb - 1) * BLOCK, BLOCK)
        ks.append(k_ref[pl.ds(start, BLOCK), :])
        vs.append(v_ref[pl.ds(start, BLOCK), :])
    kslab = jnp.concatenate(ks, axis=0)
    vslab = jnp.concatenate(vs, axis=0)
    qi = lax.broadcasted_iota(jnp.int32, (BLOCK, span), 0)
    ki = lax.broadcasted_iota(jnp.int32, (BLOCK, span), 1)
    dist = jnp.abs(qi + WINDOW - ki)
    s_pos = (j - 1) * BLOCK + ki
    in_seq = jnp.logical_and(s_pos >= 0, s_pos < nb * BLOCK)
    valid = jnp.logical_and(dist <= WINDOW, in_seq)
    distf = dist.astype(F32)

    lo = lax.broadcasted_iota(jnp.int32, (BLOCK, PAIR_W), 1) < HEAD_DIM
    hi = jnp.logical_not(lo)
    n_pairs = Q_W // PAIR_W
    heads = [(i, grp) for grp in range(2) for i in range(n_pairs)]
    qs = []
    for i, grp in heads:
        qp = q_ref[:, i * PAIR_W:(i + 1) * PAIR_W]
        qs.append(jnp.where(lo if grp == 0 else hi, qp, jnp.zeros_like(qp)))
    s_all = lax.dot_general(jnp.concatenate(qs, axis=0), kslab, (((1,), (1,)), ((), ())),
                            preferred_element_type=F32)
    ps, inv = [], []
    for n, (i, grp) in enumerate(heads):
        hd = i + n_pairs * grp
        s = s_all[n * BLOCK:(n + 1) * BLOCK] - slopes[hd] * distf
        s = jnp.where(valid, s, NEG_INF)
        sink = sink_ref[hd]
        m = jnp.maximum(jnp.max(s, axis=-1, keepdims=True), sink)
        p = jnp.exp(s - m)
        inv.append(1.0 / (jnp.sum(p, axis=-1, keepdims=True) + jnp.exp(sink - m)))
        ps.append(p.astype(BF16))
    o_all = jnp.dot(jnp.concatenate(ps, axis=0), vslab, preferred_element_type=F32)
    for i in range(n_pairs):
        o0 = o_all[i * BLOCK:(i + 1) * BLOCK] * inv[i]
        o1 = o_all[(i + n_pairs) * BLOCK:(i + n_pairs + 1) * BLOCK] * inv[i + n_pairs]
        o = jnp.where(lo, o0, o1)
        o = o * _half_rms(o, lo) * g_ref[:, i * PAIR_W:(i + 1) * PAIR_W]
        o_ref[:, i * PAIR_W:(i + 1) * PAIR_W] = o.astype(BF16)


def _attn_b_call(sink, q, k, v, g, B, S):
    T = q.shape[0]
    nb = S // BLOCK
    kern = functools.partial(_attn_b_kernel, slopes=_alibi_slopes(N_B_HEADS))
    return pl.pallas_call(
        kern,
        grid=(B, nb),
        in_specs=[pl.BlockSpec(memory_space=pltpu.SMEM),
                  pl.BlockSpec((BLOCK, Q_W), lambda b, i: (b * nb + i, 0)),
                  pl.BlockSpec((S, KV_W), lambda b, i: (b, 0)),
                  pl.BlockSpec((S, KV_W), lambda b, i: (b, 0)),
                  pl.BlockSpec((1, Q_W), lambda b, i: (0, 0))],
        out_specs=pl.BlockSpec((BLOCK, Q_W), lambda b, i: (b * nb + i, 0)),
        out_shape=jax.ShapeDtypeStruct((T, Q_W), BF16),
        compiler_params=_params("parallel", "parallel"),
        name="attn_window",
    )(sink, q, k, v, g)


def _oproj_kernel(oa_ref, ob_ref, wa_ref, wb_ref, h_ref, g_ref, b_ref, o_ref, *, alpha):
    mix = jnp.dot(oa_ref[...], wa_ref[...], preferred_element_type=F32)
    mix = mix + jnp.dot(ob_ref[...], wb_ref[...], preferred_element_type=F32)
    o_ref[...] = _layer_norm(alpha * h_ref[...] + mix, g_ref[...], b_ref[...])


def _oproj_call(oa, ob, wa, wb, h, g, b, alpha, tm):
    T, D = h.shape
    row = lambda i: (i, 0)
    const = lambda i: (0, 0)
    return pl.pallas_call(
        functools.partial(_oproj_kernel, alpha=alpha),
        grid=(T // tm,),
        in_specs=[pl.BlockSpec((tm, Q_W), row), pl.BlockSpec((tm, Q_W), row),
                  pl.BlockSpec((Q_W, D), const), pl.BlockSpec((Q_W, D), const),
                  pl.BlockSpec((tm, D), row),
                  pl.BlockSpec((1, D), const), pl.BlockSpec((1, D), const)],
        out_specs=pl.BlockSpec((tm, D), row),
        out_shape=jax.ShapeDtypeStruct((T, D), F32),
        compiler_params=_params("parallel"),
        name="out_proj_ln",
    )(oa, ob, wa, wb, h, g, b)


def _peer_scores_kernel(h_ref, wq_ref, keys_ref, st_ref):
    q = jnp.dot(h_ref[...].astype(BF16), wq_ref[...], preferred_element_type=F32).astype(BF16)
    half = PEER_DKEY // 2
    for n in range(2 * PEER_HEADS):
        qs = q[:, n * half:(n + 1) * half]
        st_ref[n] = lax.dot_general(keys_ref[n], qs, (((1,), (1,)), ((), ())),
                                    preferred_element_type=F32)


def _peer_scores_call(h, wq, keys, tm):
    T, D = h.shape
    n_sets = 2 * PEER_HEADS
    half = PEER_DKEY // 2
    return pl.pallas_call(
        _peer_scores_kernel,
        grid=(T // tm,),
        in_specs=[pl.BlockSpec((tm, D), lambda i: (i, 0)),
                  pl.BlockSpec((D, PEER_HEADS * PEER_DKEY), lambda i: (0, 0)),
                  pl.BlockSpec((n_sets, PEER_NKEYS, half), lambda i: (0, 0, 0))],
        out_specs=pl.BlockSpec((n_sets, PEER_NKEYS, tm), lambda i: (0, 0, i)),
        out_shape=jax.ShapeDtypeStruct((n_sets, PEER_NKEYS, T), F32),
        compiler_params=_params("parallel"),
        name="peer_scores",
    )(h, wq, keys)


ROUTE_TOKENS = SUBLANES * LANES
ROUTE_PASS = 4
ROUTE_PAIRS = [(r1, r2) for r1 in range(PEER_TOPK) for r2 in range(PEER_TOPK)
               if (r1 + 1) * (r2 + 1) <= PEER_TOPK]


def _slab(i):
    if isinstance(i, int):
        return pl.ds(i * SUBLANES, SUBLANES)
    return pl.ds(pl.multiple_of(i * SUBLANES, SUBLANES), SUBLANES)


def _insert_sorted(tops, x):
    out = []
    for t in tops[:-1]:
        out.append(jnp.maximum(t, x))
        x = jnp.minimum(t, x)
    out.append(jnp.maximum(tops[-1], x))
    return out


def _router_kernel(st_ref, r1_ref, c1_ref, n2_ref, e2_ref,
                   s1_ref, s2_ref, a1_ref, a2_ref, rk1_ref, n2k_ref, z_ref, bad_ref,
                   cand_ref, n2r_ref, out_ref):
    K = PEER_TOPK
    neg = -jnp.inf
    shape = (SUBLANES, LANES)
    ones = jnp.ones(shape, F32)
    zeros = jnp.zeros(shape, F32)

    def load_keys_in(src_set, dst_ref):
        for g in range(SUBLANES):
            dst_ref[pl.ds(g, PEER_NKEYS, stride=SUBLANES), :] = st_ref[src_set, :, g * LANES:(g + 1) * LANES]

    def store_keys_out(dst_ref, h):
        for g in range(SUBLANES):
            dst_ref[h, :, g * LANES:(g + 1) * LANES] = out_ref[pl.ds(g, PEER_NKEYS, stride=SUBLANES), :]

    def top_values(value_of, n, a_ref):
        thr = None
        for p in range(K // ROUTE_PASS):
            groups = [[jnp.full(shape, neg, F32)] * ROUTE_PASS for _ in range(2)]
            for k in range(n):
                x = value_of(k)
                if thr is not None:
                    x = jnp.where(x < thr, x, neg)
                groups[k % 2] = _insert_sorted(groups[k % 2], x)
            tops = groups[0]
            for x in groups[1]:
                tops = _insert_sorted(tops, x)
            for i, t in enumerate(tops):
                a_ref[_slab(p * ROUTE_PASS + i), :] = t
            thr = tops[-1]

    def adjacent_equal(a_ref):
        bad = zeros
        prev = a_ref[_slab(0), :]
        for r in range(1, K):
            cur = a_ref[_slab(r), :]
            bad = jnp.where(cur == prev, ones, bad)
            prev = cur
        return bad

    def cand(r1, r2):
        return a1_ref[_slab(r1), :] + a2_ref[_slab(r2), :]

    def finish_candidates(selected):
        top = cand(0, 0)
        z = zeros
        counts = [zeros] * K
        for idx, (r1, r2) in enumerate(ROUTE_PAIRS):
            s = selected(idx)
            z = z + s * jnp.exp(cand(r1, r2) - top)
            counts[r2] = counts[r2] + s
        z_ref[...] = z
        for r2 in range(K):
            n2r_ref[_slab(r2), :] = counts[r2]

    def head(h, carry):
        load_keys_in(2 * h, s1_ref)
        load_keys_in(2 * h + 1, s2_ref)

        top_values(lambda k: s1_ref[_slab(k), :], PEER_NKEYS, a1_ref)
        top_values(lambda k: s2_ref[_slab(k), :], PEER_NKEYS, a2_ref)
        a1 = [a1_ref[_slab(r), :] for r in range(K)]
        n_ge1 = zeros
        for k in range(PEER_NKEYS):
            s = s1_ref[_slab(k), :]
            rank = jnp.full(shape, float(K), F32)
            for r in reversed(range(K)):
                rank = jnp.where(s >= a1[r], float(r), rank)
            rk1_ref[_slab(k), :] = rank
            n_ge1 = n_ge1 + jnp.where(rank < float(K), ones, zeros)
        top_values(lambda i: cand(*ROUTE_PAIRS[i]), len(ROUTE_PAIRS), cand_ref)
        tau = cand_ref[_slab(K - 1), :]
        finish_candidates(lambda i: jnp.where(cand(*ROUTE_PAIRS[i]) >= tau, ones, zeros))
        a2 = [a2_ref[_slab(r), :] for r in range(K)]
        counts = [n2r_ref[_slab(r), :] for r in range(K)]
        n_sel = zeros
        for r in range(K):
            n_sel = n_sel + counts[r]
        n_ge2 = zeros
        for k in range(PEER_NKEYS):
            s = s2_ref[_slab(k), :]
            v = zeros
            for r in reversed(range(K)):
                v = jnp.where(s >= a2[r], counts[r], v)
            n2k_ref[_slab(k), :] = v
            n_ge2 = n_ge2 + jnp.where(s >= a2[K - 1], ones, zeros)
        bad = adjacent_equal(a1_ref) + adjacent_equal(a2_ref) + adjacent_equal(cand_ref)
        bad = bad + jnp.where(n_ge1 == float(K), zeros, ones) + jnp.where(n_ge2 == float(K), zeros, ones)
        bad = bad + jnp.where(n_sel == float(K), zeros, ones)

        bad_ref[...] = bad

        top1 = a1_ref[_slab(0), :]
        top2 = a2_ref[_slab(0), :]
        inv_z = 1.0 / z_ref[...]
        for k in range(PEER_NKEYS):
            out_ref[_slab(k), :] = rk1_ref[_slab(k), :]
        store_keys_out(r1_ref, h)
        for k in range(PEER_NKEYS):
            out_ref[_slab(k), :] = jnp.exp(s1_ref[_slab(k), :] - top1) * inv_z
        store_keys_out(c1_ref, h)
        for k in range(PEER_NKEYS):
            out_ref[_slab(k), :] = n2k_ref[_slab(k), :]
        store_keys_out(n2_ref, h)
        for k in range(PEER_NKEYS):
            out_ref[_slab(k), :] = jnp.exp(s2_ref[_slab(k), :] - top2)
        store_keys_out(e2_ref, h)

        def fix_group(g, carry2):
            @pl.when(jnp.sum(bad_ref[pl.ds(g, 1), :]) > 0.0)
            def _():
                _route_group_exact(st_ref, r1_ref, c1_ref, n2_ref, e2_ref, h,
                                   pl.ds(pl.multiple_of(g * LANES, LANES), LANES))
            return carry2

        @pl.when(jnp.sum(bad) > 0.0)
        def _():
            lax.fori_loop(0, SUBLANES, fix_group, 0)

        return carry

    lax.fori_loop(0, PEER_HEADS, head, 0)


def _route_group_exact(st_ref, r1_ref, c1_ref, n2_ref, e2_ref, h, lanes):
    tt = LANES
    K = PEER_TOPK
    n_lead = 4
    neg = -jnp.inf
    kio_i = lax.broadcasted_iota(jnp.int32, (PEER_NKEYS, tt), 0)
    rio = lax.broadcasted_iota(jnp.int32, (K, tt), 0)
    blk = kio_i // K
    rr = kio_i % K
    pos = jnp.where(blk < n_lead, rr * K + blk, (blk - n_lead) * K + rr).astype(F32)
    dup = jnp.logical_and(blk >= n_lead, rr < n_lead)
    kio = kio_i.astype(F32)

    def top16(s):
        rank = jnp.full((PEER_NKEYS, tt), float(K), F32)
        arr = jnp.zeros((K, tt), F32)
        for r in range(K):
            m = jnp.max(s, axis=0, keepdims=True)
            first = jnp.min(jnp.where(s == m, kio, float(PEER_NKEYS)), axis=0, keepdims=True)
            hit = kio == first
            rank = jnp.where(hit, float(r), rank)
            s = jnp.where(hit, neg, s)
            arr = jnp.where(rio == r, m, arr)
        return rank, arr

    s1 = st_ref[2 * h, :, lanes]
    s2 = st_ref[2 * h + 1, :, lanes]
    rank1, a1 = top16(s1)
    rank2, a2 = top16(s2)
    blocks = ([a1 + a2[r2:r2 + 1] for r2 in range(n_lead)]
              + [a1[r1:r1 + 1] + a2 for r1 in range(n_lead)])
    c0 = jnp.where(dup, neg, jnp.concatenate(blocks, axis=0))
    c = c0
    sel = jnp.zeros((PEER_NKEYS, tt), F32)
    for _ in range(K):
        m = jnp.max(c, axis=0, keepdims=True)
        first = jnp.min(jnp.where(c == m, pos, float(K * K)), axis=0, keepdims=True)
        hit = pos == first
        sel = jnp.where(hit, 1.0, sel)
        c = jnp.where(hit, neg, c)
    top = a1[0:1] + a2[0:1]
    z = jnp.sum(jnp.where(sel > 0.0, jnp.exp(c0 - top), 0.0), axis=0, keepdims=True)
    tail = sel[n_lead * K:(n_lead + 1) * K]
    for r1 in range(1, n_lead):
        tail = tail + sel[(n_lead + r1) * K:(n_lead + r1 + 1) * K]
    n2 = jnp.zeros((PEER_NKEYS, tt), F32)
    for r2 in range(K):
        if r2 < n_lead:
            cnt = jnp.sum(sel[r2 * K:(r2 + 1) * K], axis=0, keepdims=True)
        else:
            cnt = tail[r2:r2 + 1]
        n2 = jnp.where(rank2 == float(r2), cnt, n2)
    r1_ref[h, :, lanes] = rank1
    c1_ref[h, :, lanes] = jnp.exp(s1 - a1[0:1]) / z
    n2_ref[h, :, lanes] = n2
    e2_ref[h, :, lanes] = jnp.exp(s2 - a2[0:1])


def _router_call(st):
    T = st.shape[-1]
    tt = ROUTE_TOKENS
    shape = (PEER_HEADS, PEER_NKEYS, T)
    outs = [jax.ShapeDtypeStruct(shape, F32)] * 4
    spec = pl.BlockSpec((PEER_HEADS, PEER_NKEYS, tt), lambda i: (0, 0, i))
    keys = pltpu.VMEM((PEER_NKEYS * SUBLANES, LANES), F32)
    tops = pltpu.VMEM((PEER_TOPK * SUBLANES, LANES), F32)
    one = pltpu.VMEM((SUBLANES, LANES), F32)
    return pl.pallas_call(
        _router_kernel,
        grid=(T // tt,),
        in_specs=[pl.BlockSpec((2 * PEER_HEADS, PEER_NKEYS, tt), lambda i: (0, 0, i))],
        out_specs=[spec] * 4,
        out_shape=outs,
        scratch_shapes=[keys, keys, tops, tops, keys, keys, one, one, tops, tops, keys],
        compiler_params=_params("parallel"),
        name="peer_router",
    )(st)


PEER_CHUNK_ROWS = 256
PEER_IB_GROUP = 2
PEER_PART_ROWS = 64


def _peer_dense_kernel(h_ref, r1_ref, c1_ref, n2_ref, e2_ref, u_first_ref, u_a_ref, u_b_ref,
                       vt_a_ref, vt_b_ref, vt_last_ref, g_ref, b_ref, o_ref,
                       xt_ref, acc_ref, n2s_ref, e2s_ref, act_a_ref, act_b_ref, gt_a_ref, gt_b_ref,
                       *, alpha):
    j = pl.program_id(1)
    tt = h_ref.shape[0]
    te = vt_a_ref.shape[1]
    n_sub = te // PEER_NKEYS
    tile = (PEER_PART_ROWS, LANES)
    sqrt_half = float(np.sqrt(0.5))

    def gelu_bf16(a):
        return (a * (0.5 + 0.5 * lax.erf(a * sqrt_half))).astype(BF16)

    @pl.when(j == 0)
    def _():
        xt_ref[:, :tt] = h_ref[...].T.astype(BF16)
        acc_ref[:, :tt] = jnp.zeros((acc_ref.shape[0], tt), F32)
        n2s_ref[:, :, :tt] = n2_ref[...].astype(BF16)
        e2s_ref[:, :, :tt] = e2_ref[...].astype(BF16)
        gt_b_ref[:, :tt] = jnp.zeros((te, tt), BF16)
        act_a_ref[:, :tt] = gelu_bf16(jnp.dot(_unpack_bf16(u_first_ref[...]), xt_ref[:, :tt],
                                              preferred_element_type=F32))

    def phase(u_next_ref, act_next_ref, act_cur_ref, gt_cur_ref, vt_prev_ref, gt_prev_ref, key_off):
        def chunk(c, carry):
            rows = pl.ds(pl.multiple_of(c * PEER_CHUNK_ROWS, PEER_CHUNK_ROWS), PEER_CHUNK_ROWS)
            prows = pl.ds(pl.multiple_of(c * (PEER_CHUNK_ROWS // 2), PEER_CHUNK_ROWS // 2),
                          PEER_CHUNK_ROWS // 2)
            n_ib = PEER_CHUNK_ROWS // PEER_NKEYS
            for grp in range(n_ib // PEER_IB_GROUP):
                ibs = [c * n_ib + grp * PEER_IB_GROUP + s for s in range(PEER_IB_GROUP)]
                r1_rows = [[r1_ref[hd, pl.ds(key_off + ib, 1), :] for hd in range(PEER_HEADS)]
                           for ib in ibs]
                c1_rows = [[c1_ref[hd, pl.ds(key_off + ib, 1), :] for hd in range(PEER_HEADS)]
                           for ib in ibs]
                for tg in range(tt // LANES):
                    lanes = slice(tg * LANES, (tg + 1) * LANES)
                    for part in range(PEER_NKEYS // PEER_PART_ROWS):
                        krows = slice(part * PEER_PART_ROWS, (part + 1) * PEER_PART_ROWS)
                        ws = [None] * PEER_IB_GROUP
                        for hd in range(PEER_HEADS):
                            n2t = n2s_ref[hd, krows, lanes]
                            e2t = e2s_ref[hd, krows, lanes]
                            for s in range(PEER_IB_GROUP):
                                r1 = jnp.broadcast_to(r1_rows[s][hd][:, lanes], tile).astype(BF16)
                                c1 = jnp.broadcast_to(c1_rows[s][hd][:, lanes], tile).astype(BF16)
                                term = jnp.where(r1 < n2t, e2t, jnp.zeros(tile, BF16)) * c1
                                ws[s] = term if ws[s] is None else ws[s] + term
                        for s in range(PEER_IB_GROUP):
                            erows = pl.ds(pl.multiple_of(ibs[s] * PEER_NKEYS + part * PEER_PART_ROWS,
                                                         PEER_PART_ROWS), PEER_PART_ROWS)
                            gt_cur_ref[erows, lanes] = act_cur_ref[erows, lanes] * ws[s]
            act_next_ref[rows, :tt] = gelu_bf16(jnp.dot(_unpack_bf16(u_next_ref[prows, :]),
                                                        xt_ref[:, :tt], preferred_element_type=F32))
            acc_ref[rows, :tt] += jnp.dot(_unpack_bf16(vt_prev_ref[prows, :]), gt_prev_ref[:, :tt],
                                        preferred_element_type=F32)
            return carry

        lax.fori_loop(0, te // PEER_CHUNK_ROWS, chunk, 0)

    phase(u_a_ref, act_b_ref, act_a_ref, gt_a_ref, vt_a_ref, gt_b_ref, 0)
    phase(u_b_ref, act_a_ref, act_b_ref, gt_b_ref, vt_b_ref, gt_a_ref, n_sub)

    @pl.when(j == pl.num_programs(1) - 1)
    def _():
        acc = acc_ref[:, :tt] + jnp.dot(_unpack_bf16(vt_last_ref[...]), gt_b_ref[:, :tt],
                                     preferred_element_type=F32)
        z = alpha * h_ref[...] + acc.T
        o_ref[...] = _layer_norm(z, g_ref[...], b_ref[...])


def _peer_dense_call(h, r1, c1, n2, e2, u, vt, g, b, alpha, tt, te):
    T, D = h.shape
    n_sub = te // PEER_NKEYS
    n_blk = PEER_EXPERTS // te
    n_steps = n_blk // 2
    assert D == te and n_blk % 2 == 0
    spec1 = pl.BlockSpec((PEER_HEADS, 2 * n_sub, tt), lambda i, j: (0, j, i))
    spec2 = pl.BlockSpec((PEER_HEADS, PEER_NKEYS, tt), lambda i, j: (0, 0, i))
    last = n_blk - 1
    ttp = tt + LANES
    u_spec = lambda f: pl.BlockSpec((te // 2, D), lambda i, j: (f(j), 0))
    vt_spec = lambda f: pl.BlockSpec((D // 2, te), lambda i, j: (0, f(j)))
    return pl.pallas_call(
        functools.partial(_peer_dense_kernel, alpha=alpha),
        grid=(T // tt, n_steps),
        in_specs=[pl.BlockSpec((tt, D), lambda i, j: (i, 0)),
                  spec1, spec1, spec2, spec2,
                  u_spec(lambda j: 0),
                  u_spec(lambda j: 2 * j + 1),
                  u_spec(lambda j: jnp.minimum(2 * j + 2, last)),
                  vt_spec(lambda j: jnp.maximum(2 * j - 1, 0)),
                  vt_spec(lambda j: 2 * j),
                  vt_spec(lambda j: last),
                  pl.BlockSpec((1, D), lambda i, j: (0, 0)),
                  pl.BlockSpec((1, D), lambda i, j: (0, 0))],
        out_specs=pl.BlockSpec((tt, D), lambda i, j: (i, 0)),
        out_shape=jax.ShapeDtypeStruct((T, D), F32),
        scratch_shapes=[pltpu.VMEM((D, ttp), BF16), pltpu.VMEM((D, ttp), F32),
                        pltpu.VMEM((PEER_HEADS, PEER_NKEYS, ttp), BF16),
                        pltpu.VMEM((PEER_HEADS, PEER_NKEYS, ttp), BF16),
                        pltpu.VMEM((te, ttp), BF16), pltpu.VMEM((te, ttp), BF16),
                        pltpu.VMEM((te, ttp), BF16), pltpu.VMEM((te, ttp), BF16)],
        compiler_params=_params("parallel", "arbitrary"),
        name="peer_dense",
    )(h, r1, c1, n2, e2, u, u, u, vt, vt, vt, g, b)


def _pair_perm():
    idx = []
    for i in range(N_A_HEADS // 2):
        for grp in range(2):
            hd = i + (N_A_HEADS // 2) * grp
            idx.extend(range(hd * HEAD_DIM, (hd + 1) * HEAD_DIM))
    return np.asarray(idx, dtype=np.int32)


def _rope_tables(S):
    pos = np.arange(S)
    row = (pos // GRID_W).astype(np.float32)
    col = (pos % GRID_W).astype(np.float32)
    inv_freq = jnp.asarray(ROPE_THETA, F32) ** (-jnp.arange(ROPE_FREQS, dtype=F32) / ROPE_FREQS)
    ang_r = jnp.asarray(row)[:, None] * inv_freq
    ang_c = jnp.asarray(col)[:, None] * inv_freq
    cr, sr, cc, sc = jnp.cos(ang_r), jnp.sin(ang_r), jnp.cos(ang_c), jnp.sin(ang_c)
    zero = jnp.zeros_like(sr)
    cos = jnp.concatenate([cr, cr, cc, cc], axis=-1)
    sa = jnp.concatenate([-sr, zero, -sc, zero], axis=-1)
    sb = jnp.concatenate([zero, sr, zero, sc], axis=-1)
    tile2 = lambda t: jnp.concatenate([t, t], axis=-1)
    return tile2(cos), tile2(sa), tile2(sb)


def kernel(x, ln_in_g, ln_in_b, w_in, qn_g, kn_g, sink, gn_a_g, gn_b_g, w_o, ln1_g, ln1_b,
           peer_wq, peer_keys, peer_u, peer_v, ln2_g, ln2_b):
    B, S, D = x.shape
    depth = w_in.shape[0]
    T = B * S
    alpha = float((2.0 * depth) ** 0.25)
    tm = min(512, S)
    tq = min(256, S)
    tt_dense = min(512, T)
    te = 1024

    perm = _pair_perm()
    col_perm = np.concatenate([perm, Q_W + np.arange(2 * KV_W, dtype=np.int32),
                               Q_W + 2 * KV_W + perm,
                               2 * Q_W + 2 * KV_W + np.arange(2 * KV_W, dtype=np.int32)])
    cos, sa, sb = _rope_tables(S)
    row2 = lambda v: v.reshape(1, -1)
    tile2 = lambda v: jnp.concatenate([v, v]).reshape(1, -1)

    h = _ln_call(x.reshape(T, D), row2(ln_in_g), row2(ln_in_b), tm)
    for l in range(depth):
        w = w_in[l][:, col_perm].astype(BF16)
        qa, ka, va, qb, kb, vb = _proj_call(h, w, tile2(qn_g[l]), tile2(kn_g[l]), cos, sa, sb, S, tm)
        oa = _attn_a_call(qa, ka, va, row2(gn_a_g[l][perm]), B, S, tq)
        ob = _attn_b_call(sink[l], qb, kb, vb, row2(gn_b_g[l][perm]), B, S)
        wa = w_o[l][:Q_W][perm].astype(BF16)
        wb = w_o[l][Q_W:][perm].astype(BF16)
        h = _oproj_call(oa, ob, wa, wb, h, row2(ln1_g[l]), row2(ln1_b[l]), alpha, tm)
        keys = peer_keys[l].reshape(2 * PEER_HEADS, PEER_NKEYS, PEER_DKEY // 2).astype(BF16)
        st = _peer_scores_call(h, peer_wq[l].astype(BF16), keys, tm)
        r1, c1, n2, e2 = _router_call(st)
        h = _peer_dense_call(h, r1, c1, n2, e2, _pack_bf16_rows(peer_u[l]),
                             _pack_bf16_rows(peer_v[l], transpose=True),
                             row2(ln2_g[l]), row2(ln2_b[l]), alpha, tt_dense, te)
    return h.reshape(B, S, D)
```

```python
import functools

import numpy as np
import jax
import jax.numpy as jnp
from jax import lax
from jax.experimental import pallas as pl
from jax.experimental.pallas import tpu as pltpu

D_MODEL = 1024
HEAD_DIM = 64
N_A_HEADS = 8
N_B_HEADS = 8
LANES = 128
SUBLANES = 8
PAIR_W = 2 * HEAD_DIM
Q_W = N_A_HEADS * HEAD_DIM
KV_W = 2 * HEAD_DIM
IN_WIDTH = 2 * (Q_W + 2 * KV_W)
BLOCK = 128
WINDOW = 128
GRID_W = 64
ROPE_THETA = 10000.0
ROPE_FREQS = HEAD_DIM // 4

PEER_HEADS = 8
PEER_NKEYS = 128
PEER_EXPERTS = PEER_NKEYS * PEER_NKEYS
PEER_DKEY = 256
PEER_TOPK = 16

LN_EPS = 1e-5
RMS_EPS = 1e-6
NEG_INF = -1e30

VMEM_LIMIT_BYTES = 56 * 1024 * 1024

BF16 = jnp.bfloat16
F32 = jnp.float32


def _alibi_slopes(n):
    return [float(2.0 ** (-((i + 1) * 8.0 / n))) for i in range(n)]


def _params(*sem):
    return pltpu.CompilerParams(dimension_semantics=sem,
                                vmem_limit_bytes=VMEM_LIMIT_BYTES)


def _pack_kernel(w_ref, o_ref, *, transpose):
    w = w_ref[...]
    if transpose:
        w = w.T
    o_ref[...] = pltpu.bitcast(w.astype(BF16), jnp.uint32)


def _pack_bf16_rows(stacked, layer, transpose=False):
    _, n, c = stacked.shape
    blk = 1024
    assert n % blk == 0 and c % blk == 0
    if transpose:
        grid, out_map = (n // blk, c // blk), (lambda i, j: (j, i))
        out_shape = (c // 2, n)
    else:
        grid, out_map = (n // blk, c // blk), (lambda i, j: (i, j))
        out_shape = (n // 2, c)
    return pl.pallas_call(
        functools.partial(_pack_kernel, transpose=transpose),
        grid=grid,
        in_specs=[pl.BlockSpec((None, blk, blk), lambda i, j: (layer, i, j))],
        out_specs=pl.BlockSpec((blk // 2, blk), out_map),
        out_shape=jax.ShapeDtypeStruct(out_shape, jnp.uint32),
        compiler_params=_params("parallel", "parallel"),
        name="pack_bf16",
    )(stacked)


def _unpack_bf16(words):
    return pltpu.bitcast(words, BF16)


def _layer_norm(z, g, b):
    mu = jnp.mean(z, axis=-1, keepdims=True)
    zc = z - mu
    var = jnp.mean(zc * zc, axis=-1, keepdims=True)
    return zc * lax.rsqrt(var + LN_EPS) * g + b


def _half_rms(t, lo):
    sq = t * t
    s_lo = jnp.sum(jnp.where(lo, sq, 0.0), axis=-1, keepdims=True)
    s_hi = jnp.sum(jnp.where(lo, 0.0, sq), axis=-1, keepdims=True)
    return jnp.where(lo, lax.rsqrt(s_lo / HEAD_DIM + RMS_EPS),
                     lax.rsqrt(s_hi / HEAD_DIM + RMS_EPS))


def _ln_kernel(x_ref, g_ref, b_ref, o_ref):
    o_ref[...] = _layer_norm(x_ref[...], g_ref[...], b_ref[...])


def _ln_call(x, g, b, tm):
    T, D = x.shape
    return pl.pallas_call(
        _ln_kernel,
        grid=(T // tm,),
        in_specs=[pl.BlockSpec((tm, D), lambda i: (i, 0)),
                  pl.BlockSpec((1, D), lambda i: (0, 0)),
                  pl.BlockSpec((1, D), lambda i: (0, 0))],
        out_specs=pl.BlockSpec((tm, D), lambda i: (i, 0)),
        out_shape=jax.ShapeDtypeStruct((T, D), F32),
        compiler_params=_params("parallel"),
        name="ln_in",
    )(x, g, b)


def _proj_kernel(h_ref, w_ref, qg_ref, kg_ref, cos_ref, sa_ref, sb_ref,
                 qa_ref, ka_ref, va_ref, qb_ref, kb_ref, vb_ref):
    tm = h_ref.shape[0]
    y = jnp.dot(h_ref[...].astype(BF16), w_ref[...], preferred_element_type=F32)
    lo = lax.broadcasted_iota(jnp.int32, (tm, PAIR_W), 1) < HEAD_DIM
    cos, sa, sb = cos_ref[...], sa_ref[...], sb_ref[...]
    scale = HEAD_DIM ** -0.5

    def norm_rope(t, g):
        xn = t * _half_rms(t, lo) * g
        return xn * cos + pltpu.roll(xn, PAIR_W - ROPE_FREQS, 1) * sa + pltpu.roll(xn, ROPE_FREQS, 1) * sb

    for i in range(Q_W // PAIR_W):
        t = y[:, i * PAIR_W:(i + 1) * PAIR_W]
        qa_ref[:, i * PAIR_W:(i + 1) * PAIR_W] = (norm_rope(t, qg_ref[...]) * scale).astype(BF16)
    off = Q_W
    ka_ref[...] = norm_rope(y[:, off:off + KV_W], kg_ref[...]).astype(BF16)
    off += KV_W
    va_ref[...] = y[:, off:off + KV_W].astype(BF16)
    off += KV_W
    qb_ref[...] = (y[:, off:off + Q_W] * scale).astype(BF16)
    off += Q_W
    kb_ref[...] = y[:, off:off + KV_W].astype(BF16)
    off += KV_W
    vb_ref[...] = y[:, off:off + KV_W].astype(BF16)


def _proj_call(h, w, qg, kg, cos, sa, sb, S, tm):
    T, D = h.shape
    n_pos = S // tm
    row = lambda i: (i, 0)
    const = lambda i: (0, 0)
    pos = lambda i: (i % n_pos, 0)
    outs = [jax.ShapeDtypeStruct((T, w_), BF16) for w_ in (Q_W, KV_W, KV_W, Q_W, KV_W, KV_W)]
    return pl.pallas_call(
        _proj_kernel,
        grid=(T // tm,),
        in_specs=[pl.BlockSpec((tm, D), row),
                  pl.BlockSpec((D, IN_WIDTH), const),
                  pl.BlockSpec((1, PAIR_W), const),
                  pl.BlockSpec((1, PAIR_W), const),
                  pl.BlockSpec((tm, PAIR_W), pos),
                  pl.BlockSpec((tm, PAIR_W), pos),
                  pl.BlockSpec((tm, PAIR_W), pos)],
        out_specs=[pl.BlockSpec((tm, o.shape[1]), row) for o in outs],
        out_shape=outs,
        compiler_params=_params("parallel"),
        name="in_proj",
    )(h, w, qg, kg, cos, sa, sb)


def _attn_a_kernel(q_ref, k_ref, v_ref, g_ref, o_ref):
    tq = q_ref.shape[0]
    k = k_ref[...]
    v = v_ref[...]
    lo = lax.broadcasted_iota(jnp.int32, (tq, PAIR_W), 1) < HEAD_DIM
    nt = (((1,), (1,)), ((), ()))
    for i in range(Q_W // PAIR_W):
        qp = q_ref[:, i * PAIR_W:(i + 1) * PAIR_W]
        halves = []
        for grp in range(2):
            qz = jnp.where(lo if grp == 0 else jnp.logical_not(lo), qp, jnp.zeros_like(qp))
            s = lax.dot_general(qz, k, nt, preferred_element_type=F32)
            m = jnp.max(s, axis=-1, keepdims=True)
            p = jnp.exp(s - m)
            l = jnp.sum(p, axis=-1, keepdims=True)
            halves.append(jnp.dot(p.astype(BF16), v, preferred_element_type=F32) / l)
        o = jnp.where(lo, halves[0], halves[1])
        o = o * _half_rms(o, lo) * g_ref[:, i * PAIR_W:(i + 1) * PAIR_W]
        o_ref[:, i * PAIR_W:(i + 1) * PAIR_W] = o.astype(BF16)


def _attn_a_call(q, k, v, g, B, S, tq):
    T = q.shape[0]
    nq = S // tq
    return pl.pallas_call(
        _attn_a_kernel,
        grid=(B, nq),
        in_specs=[pl.BlockSpec((tq, Q_W), lambda b, i: (b * nq + i, 0)),
                  pl.BlockSpec((S, KV_W), lambda b, i: (b, 0)),
                  pl.BlockSpec((S, KV_W), lambda b, i: (b, 0)),
                  pl.BlockSpec((1, Q_W), lambda b, i: (0, 0))],
        out_specs=pl.BlockSpec((tq, Q_W), lambda b, i: (b * nq + i, 0)),
        out_shape=jax.ShapeDtypeStruct((T, Q_W), BF16),
        compiler_params=_params("parallel", "parallel"),
        name="attn_global",
    )(q, k, v, g)


def _attn_b_kernel(sink_ref, q_ref, k_ref, v_ref, g_ref, o_ref, *, slopes):
    j = pl.program_id(1)
    nb = k_ref.shape[0] // BLOCK
    span = BLOCK + 2 * WINDOW
    ks, vs = [], []
    for c in (-1, 0, 1):
        jc = j + c
        start = pl.multiple_of(jnp.clip(jc, 0, nb - 1) * BLOCK, BLOCK)
        ks.append(k_ref[pl.ds(start, BLOCK), :])
        vs.append(v_ref[pl.ds(start, BLOCK), :])
    kslab = jnp.concatenate(ks, axis=0)
    vslab = jnp.concatenate(vs, axis=0)
    qi = lax.broadcasted_iota(jnp.int32, (BLOCK, span), 0)
    ki = lax.broadcasted_iota(jnp.int32, (BLOCK, span), 1)
    dist = jnp.abs(qi + WINDOW - ki)
    s_pos = (j - 1) * BLOCK + ki
    in_seq = jnp.logical_and(s_pos >= 0, s_pos < nb * BLOCK)
    valid = jnp.logical_and(dist <= WINDOW, in_seq)
    distf = dist.astype(F32)

    lo = lax.broadcasted_iota(jnp.int32, (BLOCK, PAIR_W), 1) < HEAD_DIM
    hi = jnp.logical_not(lo)
    n_pairs = Q_W // PAIR_W
    heads = [(i, grp) for grp in range(2) for i in range(n_pairs)]
    qs = []
    for i, grp in heads:
        qp = q_ref[:, i * PAIR_W:(i + 1) * PAIR_W]
        qs.append(jnp.where(lo if grp == 0 else hi, qp, jnp.zeros_like(qp)))
    s_all = lax.dot_general(jnp.concatenate(qs, axis=0), kslab, (((1,), (1,)), ((), ())),
                            preferred_element_type=F32)
    ps, inv = [], []
    for n, (i, grp) in enumerate(heads):
        hd = i + n_pairs * grp
        s = s_all[n * BLOCK:(n + 1) * BLOCK] - slopes[hd] * distf
        s = jnp.where(valid, s, NEG_INF)
        sink = sink_ref[hd]
        m = jnp.maximum(jnp.max(s, axis=-1, keepdims=True), sink)
        p = jnp.exp(s - m)
        inv.append(1.0 / (jnp.sum(p, axis=-1, keepdims=True) + jnp.exp(sink - m)))
        ps.append(p.astype(BF16))
    o_all = jnp.dot(jnp.concatenate(ps, axis=0), vslab, preferred_element_type=F32)
    for i in range(n_pairs):
        o0 = o_all[i * BLOCK:(i + 1) * BLOCK] * inv[i]
        o1 = o_all[(i + n_pairs) * BLOCK:(i + n_pairs + 1) * BLOCK] * inv[i + n_pairs]
        o = jnp.where(lo, o0, o1)
        o = o * _half_rms(o, lo) * g_ref[:, i * PAIR_W:(i + 1) * PAIR_W]
        o_ref[:, i * PAIR_W:(i + 1) * PAIR_W] = o.astype(BF16)


def _attn_b_call(sink, q, k, v, g, B, S):
    T = q.shape[0]
    nb = S // BLOCK
    kern = functools.partial(_attn_b_kernel, slopes=_alibi_slopes(N_B_HEADS))
    return pl.pallas_call(
        kern,
        grid=(B, nb),
        in_specs=[pl.BlockSpec(memory_space=pltpu.SMEM),
                  pl.BlockSpec((BLOCK, Q_W), lambda b, i: (b * nb + i, 0)),
                  pl.BlockSpec((S, KV_W), lambda b, i: (b, 0)),
                  pl.BlockSpec((S, KV_W), lambda b, i: (b, 0)),
                  pl.BlockSpec((1, Q_W), lambda b, i: (0, 0))],
        out_specs=pl.BlockSpec((BLOCK, Q_W), lambda b, i: (b * nb + i, 0)),
        out_shape=jax.ShapeDtypeStruct((T, Q_W), BF16),
        compiler_params=_params("parallel", "parallel"),
        name="attn_window",
    )(sink, q, k, v, g)


def _oproj_kernel(oa_ref, ob_ref, wa_ref, wb_ref, h_ref, g_ref, b_ref, o_ref, *, alpha):
    mix = jnp.dot(oa_ref[...], wa_ref[...], preferred_element_type=F32)
    mix = mix + jnp.dot(ob_ref[...], wb_ref[...], preferred_element_type=F32)
    o_ref[...] = _layer_norm(alpha * h_ref[...] + mix, g_ref[...], b_ref[...])


def _oproj_call(oa, ob, wa, wb, h, g, b, alpha, tm):
    T, D = h.shape
    row = lambda i: (i, 0)
    const = lambda i: (0, 0)
    return pl.pallas_call(
        functools.partial(_oproj_kernel, alpha=alpha),
        grid=(T // tm,),
        in_specs=[pl.BlockSpec((tm, Q_W), row), pl.BlockSpec((tm, Q_W), row),
                  pl.BlockSpec((Q_W, D), const), pl.BlockSpec((Q_W, D), const),
                  pl.BlockSpec((tm, D), row),
                  pl.BlockSpec((1, D), const), pl.BlockSpec((1, D), const)],
        out_specs=pl.BlockSpec((tm, D), row),
        out_shape=jax.ShapeDtypeStruct((T, D), F32),
        compiler_params=_params("parallel"),
        name="out_proj_ln",
    )(oa, ob, wa, wb, h, g, b)


def _peer_scores_kernel(h_ref, wq_ref, keys_ref, st_ref):
    q = jnp.dot(h_ref[...].astype(BF16), wq_ref[...], preferred_element_type=F32).astype(BF16)
    half = PEER_DKEY // 2
    for n in range(2 * PEER_HEADS):
        qs = q[:, n * half:(n + 1) * half]
        st_ref[n] = lax.dot_general(keys_ref[n], qs, (((1,), (1,)), ((), ())),
                                    preferred_element_type=F32)


def _peer_scores_call(h, wq, keys, tm):
    T, D = h.shape
    n_sets = 2 * PEER_HEADS
    half = PEER_DKEY // 2
    return pl.pallas_call(
        _peer_scores_kernel,
        grid=(T // tm,),
        in_specs=[pl.BlockSpec((tm, D), lambda i: (i, 0)),
                  pl.BlockSpec((D, PEER_HEADS * PEER_DKEY), lambda i: (0, 0)),
                  pl.BlockSpec((n_sets, PEER_NKEYS, half), lambda i: (0, 0, 0))],
        out_specs=pl.BlockSpec((n_sets, PEER_NKEYS, tm), lambda i: (0, 0, i)),
        out_shape=jax.ShapeDtypeStruct((n_sets, PEER_NKEYS, T), F32),
        compiler_params=_params("parallel"),
        name="peer_scores",
    )(h, wq, keys)


ROUTE_TOKENS = SUBLANES * LANES
ROUTE_PASS = 4
ROUTE_PAIRS = [(r1, r2) for r1 in range(PEER_TOPK) for r2 in range(PEER_TOPK)
               if (r1 + 1) * (r2 + 1) <= PEER_TOPK]


def _slab(i):
    if isinstance(i, int):
        return pl.ds(i * SUBLANES, SUBLANES)
    return pl.ds(pl.multiple_of(i * SUBLANES, SUBLANES), SUBLANES)


def _insert_sorted(tops, x):
    out = []
    for t in tops[:-1]:
        out.append(jnp.maximum(t, x))
        x = jnp.minimum(t, x)
    out.append(jnp.maximum(tops[-1], x))
    return out


def _router_kernel(st_ref, r1_ref, c1_ref, n2_ref, e2_ref,
                   s1_ref, s2_ref, a1_ref, a2_ref, rk1_ref, n2k_ref, z_ref, bad_ref,
                   cand_ref, n2r_ref, out_ref):
    K = PEER_TOPK
    neg = -jnp.inf
    shape = (SUBLANES, LANES)
    ones = jnp.ones(shape, F32)
    zeros = jnp.zeros(shape, F32)

    def load_keys_in(src_set, dst_ref):
        for g in range(SUBLANES):
            dst_ref[pl.ds(g, PEER_NKEYS, stride=SUBLANES), :] = st_ref[src_set, :, g * LANES:(g + 1) * LANES]

    def store_keys_out(dst_ref, h):
        for g in range(SUBLANES):
            dst_ref[h, :, g * LANES:(g + 1) * LANES] = out_ref[pl.ds(g, PEER_NKEYS, stride=SUBLANES), :]

    def top_values(value_of, n, a_ref):
        thr = None
        for p in range(K // ROUTE_PASS):
            groups = [[jnp.full(shape, neg, F32)] * ROUTE_PASS for _ in range(2)]
            for k in range(n):
                x = value_of(k)
                if thr is not None:
                    x = jnp.where(x < thr, x, neg)
                groups[k % 2] = _insert_sorted(groups[k % 2], x)
            tops = groups[0]
            for x in groups[1]:
                tops = _insert_sorted(tops, x)
            for i, t in enumerate(tops):
                a_ref[_slab(p * ROUTE_PASS + i), :] = t
            thr = tops[-1]

    def adjacent_equal(a_ref):
        bad = zeros
        prev = a_ref[_slab(0), :]
        for r in range(1, K):
            cur = a_ref[_slab(r), :]
            bad = jnp.where(cur == prev, ones, bad)
            prev = cur
        return bad

    def cand(r1, r2):
        return a1_ref[_slab(r1), :] + a2_ref[_slab(r2), :]

    def finish_candidates(selected):
        top = cand(0, 0)
        z = zeros
        counts = [zeros] * K
        for idx, (r1, r2) in enumerate(ROUTE_PAIRS):
            s = selected(idx)
            z = z + s * jnp.exp(cand(r1, r2) - top)
            counts[r2] = counts[r2] + s
        z_ref[...] = z
        for r2 in range(K):
            n2r_ref[_slab(r2), :] = counts[r2]

    def head(h, carry):
        load_keys_in(2 * h, s1_ref)
        load_keys_in(2 * h + 1, s2_ref)

        top_values(lambda k: s1_ref[_slab(k), :], PEER_NKEYS, a1_ref)
        top_values(lambda k: s2_ref[_slab(k), :], PEER_NKEYS, a2_ref)
        a1 = [a1_ref[_slab(r), :] for r in range(K)]
        n_ge1 = zeros
        for k in range(PEER_NKEYS):
            s = s1_ref[_slab(k), :]
            rank = jnp.full(shape, float(K), F32)
            for r in reversed(range(K)):
                rank = jnp.where(s >= a1[r], float(r), rank)
            rk1_ref[_slab(k), :] = rank
            n_ge1 = n_ge1 + jnp.where(rank < float(K), ones, zeros)
        top_values(lambda i: cand(*ROUTE_PAIRS[i]), len(ROUTE_PAIRS), cand_ref)
        tau = cand_ref[_slab(K - 1), :]
        finish_candidates(lambda i: jnp.where(cand(*ROUTE_PAIRS[i]) >= tau, ones, zeros))
        a2 = [a2_ref[_slab(r), :] for r in range(K)]
        counts = [n2r_ref[_slab(r), :] for r in range(K)]
        n_sel = zeros
        for r in range(K):
            n_sel = n_sel + counts[r]
        n_ge2 = zeros
        for k in range(PEER_NKEYS):
            s = s2_ref[_slab(k), :]
            v = zeros
            for r in reversed(range(K)):
                v = jnp.where(s >= a2[r], counts[r], v)
            n2k_ref[_slab(k), :] = v
            n_ge2 = n_ge2 + jnp.where(s >= a2[K - 1], ones, zeros)
        bad = adjacent_equal(a1_ref) + adjacent_equal(a2_ref) + adjacent_equal(cand_ref)
        bad = bad + jnp.where(n_ge1 == float(K), zeros, ones) + jnp.where(n_ge2 == float(K), zeros, ones)
        bad = bad + jnp.where(n_sel == float(K), zeros, ones)

        bad_ref[...] = bad

        top1 = a1_ref[_slab(0), :]
        top2 = a2_ref[_slab(0), :]
        inv_z = 1.0 / z_ref[...]
        for k in range(PEER_NKEYS):
            out_ref[_slab(k), :] = rk1_ref[_slab(k), :]
        store_keys_out(r1_ref, h)
        for k in range(PEER_NKEYS):
            out_ref[_slab(k), :] = jnp.exp(s1_ref[_slab(k), :] - top1) * inv_z
        store_keys_out(c1_ref, h)
        for k in range(PEER_NKEYS):
            out_ref[_slab(k), :] = n2k_ref[_slab(k), :]
        store_keys_out(n2_ref, h)
        for k in range(PEER_NKEYS):
            out_ref[_slab(k), :] = jnp.exp(s2_ref[_slab(k), :] - top2)
        store_keys_out(e2_ref, h)

        def fix_group(g, carry2):
            @pl.when(jnp.sum(bad_ref[pl.ds(g, 1), :]) > 0.0)
            def _():
                _route_group_exact(st_ref, r1_ref, c1_ref, n2_ref, e2_ref, h,
                                   pl.ds(pl.multiple_of(g * LANES, LANES), LANES))
            return carry2

        @pl.when(jnp.sum(bad) > 0.0)
        def _():
            lax.fori_loop(0, SUBLANES, fix_group, 0)

        return carry

    lax.fori_loop(0, PEER_HEADS, head, 0)


def _route_group_exact(st_ref, r1_ref, c1_ref, n2_ref, e2_ref, h, lanes):
    tt = LANES
    K = PEER_TOPK
    n_lead = 4
    neg = -jnp.inf
    kio_i = lax.broadcasted_iota(jnp.int32, (PEER_NKEYS, tt), 0)
    rio = lax.broadcasted_iota(jnp.int32, (K, tt), 0)
    blk = kio_i // K
    rr = kio_i % K
    pos = jnp.where(blk < n_lead, rr * K + blk, (blk - n_lead) * K + rr).astype(F32)
    dup = jnp.logical_and(blk >= n_lead, rr < n_lead)
    kio = kio_i.astype(F32)

    def top16(s):
        rank = jnp.full((PEER_NKEYS, tt), float(K), F32)
        arr = jnp.zeros((K, tt), F32)
        for r in range(K):
            m = jnp.max(s, axis=0, keepdims=True)
            first = jnp.min(jnp.where(s == m, kio, float(PEER_NKEYS)), axis=0, keepdims=True)
            hit = kio == first
            rank = jnp.where(hit, float(r), rank)
            s = jnp.where(hit, neg, s)
            arr = jnp.where(rio == r, m, arr)
        return rank, arr

    s1 = st_ref[2 * h, :, lanes]
    s2 = st_ref[2 * h + 1, :, lanes]
    rank1, a1 = top16(s1)
    rank2, a2 = top16(s2)
    blocks = ([a1 + a2[r2:r2 + 1] for r2 in range(n_lead)]
              + [a1[r1:r1 + 1] + a2 for r1 in range(n_lead)])
    c0 = jnp.where(dup, neg, jnp.concatenate(blocks, axis=0))
    c = c0
    sel = jnp.zeros((PEER_NKEYS, tt), F32)
    for _ in range(K):
        m = jnp.max(c, axis=0, keepdims=True)
        first = jnp.min(jnp.where(c == m, pos, float(K * K)), axis=0, keepdims=True)
        hit = pos == first
        sel = jnp.where(hit, 1.0, sel)
        c = jnp.where(hit, neg, c)
    top = a1[0:1] + a2[0:1]
    z = jnp.sum(jnp.where(sel > 0.0, jnp.exp(c0 - top), 0.0), axis=0, keepdims=True)
    tail = sel[n_lead * K:(n_lead + 1) * K]
    for r1 in range(1, n_lead):
        tail = tail + sel[(n_lead + r1) * K:(n_lead + r1 + 1) * K]
    n2 = jnp.zeros((PEER_NKEYS, tt), F32)
    for r2 in range(K):
        if r2 < n_lead:
            cnt = jnp.sum(sel[r2 * K:(r2 + 1) * K], axis=0, keepdims=True)
        else:
            cnt = tail[r2:r2 + 1]
        n2 = jnp.where(rank2 == float(r2), cnt, n2)
    r1_ref[h, :, lanes] = rank1
    c1_ref[h, :, lanes] = jnp.exp(s1 - a1[0:1]) / z
    n2_ref[h, :, lanes] = n2
    e2_ref[h, :, lanes] = jnp.exp(s2 - a2[0:1])


def _router_call(st):
    T = st.shape[-1]
    tt = ROUTE_TOKENS
    shape = (PEER_HEADS, PEER_NKEYS, T)
    outs = [jax.ShapeDtypeStruct(shape, F32)] * 4
    spec = pl.BlockSpec((PEER_HEADS, PEER_NKEYS, tt), lambda i: (0, 0, i))
    keys = pltpu.VMEM((PEER_NKEYS * SUBLANES, LANES), F32)
    tops = pltpu.VMEM((PEER_TOPK * SUBLANES, LANES), F32)
    one = pltpu.VMEM((SUBLANES, LANES), F32)
    return pl.pallas_call(
        _router_kernel,
        grid=(T // tt,),
        in_specs=[pl.BlockSpec((2 * PEER_HEADS, PEER_NKEYS, tt), lambda i: (0, 0, i))],
        out_specs=[spec] * 4,
        out_shape=outs,
        scratch_shapes=[keys, keys, tops, tops, keys, keys, one, one, tops, tops, keys],
        compiler_params=_params("parallel"),
        name="peer_router",
    )(st)


PEER_CHUNK_ROWS = 256
PEER_IB_GROUP = 2
PEER_PART_ROWS = 64


def _peer_dense_kernel(h_ref, r1_ref, c1_ref, n2_ref, e2_ref, u_first_ref, u_a_ref, u_b_ref,
                       vt_a_ref, vt_b_ref, vt_last_ref, g_ref, b_ref, o_ref,
                       xt_ref, acc_ref, n2s_ref, e2s_ref, act_a_ref, act_b_ref, gt_a_ref, gt_b_ref,
                       *, alpha):
    j = pl.program_id(1)
    tt = h_ref.shape[0]
    te = vt_a_ref.shape[1]
    n_sub = te // PEER_NKEYS
    tile = (PEER_PART_ROWS, LANES)
    sqrt_half = float(np.sqrt(0.5))

    def gelu_bf16(a):
        return (a * (0.5 + 0.5 * lax.erf(a * sqrt_half))).astype(BF16)

    @pl.when(j == 0)
    def _():
        xt_ref[:, :tt] = h_ref[...].T.astype(BF16)
        acc_ref[:, :tt] = jnp.zeros((acc_ref.shape[0], tt), F32)
        n2s_ref[:, :, :tt] = n2_ref[...].astype(BF16)
        e2s_ref[:, :, :tt] = e2_ref[...].astype(BF16)
        gt_b_ref[:, :tt] = jnp.zeros((te, tt), BF16)
        act_a_ref[:, :tt] = gelu_bf16(jnp.dot(_unpack_bf16(u_first_ref[...]), xt_ref[:, :tt],
                                              preferred_element_type=F32))

    def phase(u_next_ref, act_next_ref, act_cur_ref, gt_cur_ref, vt_prev_ref, gt_prev_ref, key_off):
        def chunk(c, carry):
            rows = pl.ds(pl.multiple_of(c * PEER_CHUNK_ROWS, PEER_CHUNK_ROWS), PEER_CHUNK_ROWS)
            prows = pl.ds(pl.multiple_of(c * (PEER_CHUNK_ROWS // 2), PEER_CHUNK_ROWS // 2),
                          PEER_CHUNK_ROWS // 2)
            n_ib = PEER_CHUNK_ROWS // PEER_NKEYS
            for grp in range(n_ib // PEER_IB_GROUP):
                ibs = [c * n_ib + grp * PEER_IB_GROUP + s for s in range(PEER_IB_GROUP)]
                r1_rows = [[r1_ref[hd, pl.ds(key_off + ib, 1), :] for hd in range(PEER_HEADS)]
                           for ib in ibs]
                c1_rows = [[c1_ref[hd, pl.ds(key_off + ib, 1), :] for hd in range(PEER_HEADS)]
                           for ib in ibs]
                for tg in range(tt // LANES):
                    lanes = slice(tg * LANES, (tg + 1) * LANES)
                    for part in range(PEER_NKEYS // PEER_PART_ROWS):
                        krows = slice(part * PEER_PART_ROWS, (part + 1) * PEER_PART_ROWS)
                        ws = [None] * PEER_IB_GROUP
                        for hd in range(PEER_HEADS):
                            n2t = n2s_ref[hd, krows, lanes]
                            e2t = e2s_ref[hd, krows, lanes]
                            for s in range(PEER_IB_GROUP):
                                r1 = jnp.broadcast_to(r1_rows[s][hd][:, lanes], tile).astype(BF16)
                                c1 = jnp.broadcast_to(c1_rows[s][hd][:, lanes], tile).astype(BF16)
                                term = jnp.where(r1 < n2t, e2t, jnp.zeros(tile, BF16)) * c1
                                ws[s] = term if ws[s] is None else ws[s] + term
                        for s in range(PEER_IB_GROUP):
                            erows = pl.ds(pl.multiple_of(ibs[s] * PEER_NKEYS + part * PEER_PART_ROWS,
                                                         PEER_PART_ROWS), PEER_PART_ROWS)
                            gt_cur_ref[erows, lanes] = act_cur_ref[erows, lanes] * ws[s]
            act_next_ref[rows, :tt] = gelu_bf16(jnp.dot(_unpack_bf16(u_next_ref[prows, :]),
                                                        xt_ref[:, :tt], preferred_element_type=F32))
            acc_ref[rows, :tt] += jnp.dot(_unpack_bf16(vt_prev_ref[prows, :]), gt_prev_ref[:, :tt],
                                        preferred_element_type=F32)
            return carry

        lax.fori_loop(0, te // PEER_CHUNK_ROWS, chunk, 0)

    phase(u_a_ref, act_b_ref, act_a_ref, gt_a_ref, vt_a_ref, gt_b_ref, 0)
    phase(u_b_ref, act_a_ref, act_b_ref, gt_b_ref, vt_b_ref, gt_a_ref, n_sub)

    @pl.when(j == pl.num_programs(1) - 1)
    def _():
        acc = acc_ref[:, :tt] + jnp.dot(_unpack_bf16(vt_last_ref[...]), gt_b_ref[:, :tt],
                                     preferred_element_type=F32)
        z = alpha * h_ref[...] + acc.T
        o_ref[...] = _layer_norm(z, g_ref[...], b_ref[...])


def _peer_dense_call(h, r1, c1, n2, e2, u, vt, g, b, alpha, tt, te):
    T, D = h.shape
    n_sub = te // PEER_NKEYS
    n_blk = PEER_EXPERTS // te
    n_steps = n_blk // 2
    assert D == te and n_blk % 2 == 0
    spec1 = pl.BlockSpec((PEER_HEADS, 2 * n_sub, tt), lambda i, j: (0, j, i))
    spec2 = pl.BlockSpec((PEER_HEADS, PEER_NKEYS, tt), lambda i, j: (0, 0, i))
    last = n_blk - 1
    ttp = tt + LANES
    u_spec = lambda f: pl.BlockSpec((te // 2, D), lambda i, j: (f(j), 0))
    vt_spec = lambda f: pl.BlockSpec((D // 2, te), lambda i, j: (0, f(j)))
    return pl.pallas_call(
        functools.partial(_peer_dense_kernel, alpha=alpha),
        grid=(T // tt, n_steps),
        in_specs=[pl.BlockSpec((tt, D), lambda i, j: (i, 0)),
                  spec1, spec1, spec2, spec2,
                  u_spec(lambda j: 0),
                  u_spec(lambda j: 2 * j + 1),
                  u_spec(lambda j: jnp.minimum(2 * j + 2, last)),
                  vt_spec(lambda j: jnp.maximum(2 * j - 1, 0)),
                  vt_spec(lambda j: 2 * j),
                  vt_spec(lambda j: last),
                  pl.BlockSpec((1, D), lambda i, j: (0, 0)),
                  pl.BlockSpec((1, D), lambda i, j: (0, 0))],
        out_specs=pl.BlockSpec((tt, D), lambda i, j: (i, 0)),
        out_shape=jax.ShapeDtypeStruct((T, D), F32),
        scratch_shapes=[pltpu.VMEM((D, ttp), BF16), pltpu.VMEM((D, ttp), F32),
                        pltpu.VMEM((PEER_HEADS, PEER_NKEYS, ttp), BF16),
                        pltpu.VMEM((PEER_HEADS, PEER_NKEYS, ttp), BF16),
                        pltpu.VMEM((te, ttp), BF16), pltpu.VMEM((te, ttp), BF16),
                        pltpu.VMEM((te, ttp), BF16), pltpu.VMEM((te, ttp), BF16)],
        compiler_params=_params("parallel", "arbitrary"),
        name="peer_dense",
    )(h, r1, c1, n2, e2, u, u, u, vt, vt, vt, g, b)


def _pair_perm():
    idx = []
    for i in range(N_A_HEADS // 2):
        for grp in range(2):
            hd = i + (N_A_HEADS // 2) * grp
            idx.extend(range(hd * HEAD_DIM, (hd + 1) * HEAD_DIM))
    return np.asarray(idx, dtype=np.int32)


def _rope_tables(S):
    pos = np.arange(S)
    row = (pos // GRID_W).astype(np.float32)
    col = (pos % GRID_W).astype(np.float32)
    inv_freq = jnp.asarray(ROPE_THETA, F32) ** (-jnp.arange(ROPE_FREQS, dtype=F32) / ROPE_FREQS)
    ang_r = jnp.asarray(row)[:, None] * inv_freq
    ang_c = jnp.asarray(col)[:, None] * inv_freq
    cr, sr, cc, sc = jnp.cos(ang_r), jnp.sin(ang_r), jnp.cos(ang_c), jnp.sin(ang_c)
    zero = jnp.zeros_like(sr)
    cos = jnp.concatenate([cr, cr, cc, cc], axis=-1)
    sa = jnp.concatenate([-sr, zero, -sc, zero], axis=-1)
    sb = jnp.concatenate([zero, sr, zero, sc], axis=-1)
    tile2 = lambda t: jnp.concatenate([t, t], axis=-1)
    return tile2(cos), tile2(sa), tile2(sb)


def kernel(x, ln_in_g, ln_in_b, w_in, qn_g, kn_g, sink, gn_a_g, gn_b_g, w_o, ln1_g, ln1_b,
           peer_wq, peer_keys, peer_u, peer_v, ln2_g, ln2_b):
    B, S, D = x.shape
    depth = w_in.shape[0]
    T = B * S
    alpha = float((2.0 * depth) ** 0.25)
    tm = min(1024, S)
    tq = min(256, S)
    tt_dense = min(512, T)
    te = 1024

    perm = _pair_perm()
    col_perm = np.concatenate([perm, Q_W + np.arange(2 * KV_W, dtype=np.int32),
                               Q_W + 2 * KV_W + perm,
                               2 * Q_W + 2 * KV_W + np.arange(2 * KV_W, dtype=np.int32)])
    cos, sa, sb = _rope_tables(S)
    row2 = lambda v: v.reshape(1, -1)
    tile2 = lambda v: jnp.concatenate([v, v]).reshape(1, -1)

    h = _ln_call(x.reshape(T, D), row2(ln_in_g), row2(ln_in_b), tm)
    for l in range(depth):
        w = w_in[l][:, col_perm].astype(BF16)
        qa, ka, va, qb, kb, vb = _proj_call(h, w, tile2(qn_g[l]), tile2(kn_g[l]), cos, sa, sb, S, tm)
        oa = _attn_a_call(qa, ka, va, row2(gn_a_g[l][perm]), B, S, tq)
        ob = _attn_b_call(sink[l], qb, kb, vb, row2(gn_b_g[l][perm]), B, S)
        wa = w_o[l][:Q_W][perm].astype(BF16)
        wb = w_o[l][Q_W:][perm].astype(BF16)
        h = _oproj_call(oa, ob, wa, wb, h, row2(ln1_g[l]), row2(ln1_b[l]), alpha, tm)
        keys = peer_keys[l].reshape(2 * PEER_HEADS, PEER_NKEYS, PEER_DKEY // 2).astype(BF16)
        st = _peer_scores_call(h, peer_wq[l].astype(BF16), keys, tm)
        r1, c1, n2, e2 = _router_call(st)
        h = _peer_dense_call(h, r1, c1, n2, e2, _pack_bf16_rows(peer_u, l),
                             _pack_bf16_rows(peer_v, l, transpose=True),
                             row2(ln2_g[l]), row2(ln2_b[l]), alpha, tt_dense, te)
    return h.reshape(B, S, D)
```

```python
import functools

import numpy as np
import jax
import jax.numpy as jnp
from jax import lax
from jax.experimental import pallas as pl
from jax.experimental.pallas import tpu as pltpu

D_MODEL = 1024
HEAD_DIM = 64
N_A_HEADS = 8
N_B_HEADS = 8
LANES = 128
SUBLANES = 8
PAIR_W = 2 * HEAD_DIM
Q_W = N_A_HEADS * HEAD_DIM
KV_W = 2 * HEAD_DIM
IN_WIDTH = 2 * (Q_W + 2 * KV_W)
BLOCK = 128
WINDOW = 128
GRID_W = 64
ROPE_THETA = 10000.0
ROPE_FREQS = HEAD_DIM // 4

PEER_HEADS = 8
PEER_NKEYS = 128
PEER_EXPERTS = PEER_NKEYS * PEER_NKEYS
PEER_DKEY = 256
PEER_TOPK = 16

LN_EPS = 1e-5
RMS_EPS = 1e-6
NEG_INF = -1e30

VMEM_LIMIT_BYTES = 56 * 1024 * 1024

BF16 = jnp.bfloat16
F32 = jnp.float32


def _alibi_slopes(n):
    return [float(2.0 ** (-((i + 1) * 8.0 / n))) for i in range(n)]


def _params(*sem):
    return pltpu.CompilerParams(dimension_semantics=sem,
                                vmem_limit_bytes=VMEM_LIMIT_BYTES)


def _pack_kernel(w_ref, o_ref, *, transpose):
    w = w_ref[...]
    if transpose:
        w = w.T
    o_ref[...] = pltpu.bitcast(w.astype(BF16), jnp.uint32)


def _pack_bf16_rows(stacked, layer, transpose=False):
    _, n, c = stacked.shape
    blk = 1024
    assert n % blk == 0 and c % blk == 0
    if transpose:
        grid, out_map = (n // blk, c // blk), (lambda i, j: (j, i))
        out_shape = (c // 2, n)
    else:
        grid, out_map = (n // blk, c // blk), (lambda i, j: (i, j))
        out_shape = (n // 2, c)
    return pl.pallas_call(
        functools.partial(_pack_kernel, transpose=transpose),
        grid=grid,
        in_specs=[pl.BlockSpec((None, blk, blk), lambda i, j: (layer, i, j))],
        out_specs=pl.BlockSpec((blk // 2, blk), out_map),
        out_shape=jax.ShapeDtypeStruct(out_shape, jnp.uint32),
        compiler_params=_params("parallel", "parallel"),
        name="pack_bf16",
    )(stacked)


def _unpack_bf16(words):
    return pltpu.bitcast(words, BF16)


def _layer_norm(z, g, b):
    mu = jnp.mean(z, axis=-1, keepdims=True)
    zc = z - mu
    var = jnp.mean(zc * zc, axis=-1, keepdims=True)
    return zc * lax.rsqrt(var + LN_EPS) * g + b


def _half_rms(t, lo):
    sq = t * t
    s_lo = jnp.sum(jnp.where(lo, sq, 0.0), axis=-1, keepdims=True)
    s_hi = jnp.sum(jnp.where(lo, 0.0, sq), axis=-1, keepdims=True)
    return jnp.where(lo, lax.rsqrt(s_lo / HEAD_DIM + RMS_EPS),
                     lax.rsqrt(s_hi / HEAD_DIM + RMS_EPS))


def _ln_kernel(x_ref, g_ref, b_ref, o_ref):
    o_ref[...] = _layer_norm(x_ref[...], g_ref[...], b_ref[...])


def _ln_call(x, g, b, tm):
    T, D = x.shape
    return pl.pallas_call(
        _ln_kernel,
        grid=(T // tm,),
        in_specs=[pl.BlockSpec((tm, D), lambda i: (i, 0)),
                  pl.BlockSpec((1, D), lambda i: (0, 0)),
                  pl.BlockSpec((1, D), lambda i: (0, 0))],
        out_specs=pl.BlockSpec((tm, D), lambda i: (i, 0)),
        out_shape=jax.ShapeDtypeStruct((T, D), F32),
        compiler_params=_params("parallel"),
        name="ln_in",
    )(x, g, b)


def _proj_kernel(h_ref, w_ref, qg_ref, kg_ref, cos_ref, sa_ref, sb_ref,
                 qa_ref, ka_ref, va_ref, qb_ref, kb_ref, vb_ref):
    tm = h_ref.shape[0]
    y = jnp.dot(h_ref[...].astype(BF16), w_ref[...], preferred_element_type=F32)
    lo = lax.broadcasted_iota(jnp.int32, (tm, PAIR_W), 1) < HEAD_DIM
    cos, sa, sb = cos_ref[...], sa_ref[...], sb_ref[...]
    scale = HEAD_DIM ** -0.5

    def norm_rope(t, g):
        xn = t * _half_rms(t, lo) * g
        return xn * cos + pltpu.roll(xn, PAIR_W - ROPE_FREQS, 1) * sa + pltpu.roll(xn, ROPE_FREQS, 1) * sb

    for i in range(Q_W // PAIR_W):
        t = y[:, i * PAIR_W:(i + 1) * PAIR_W]
        qa_ref[:, i * PAIR_W:(i + 1) * PAIR_W] = (norm_rope(t, qg_ref[...]) * scale).astype(BF16)
    off = Q_W
    ka_ref[...] = norm_rope(y[:, off:off + KV_W], kg_ref[...]).astype(BF16)
    off += KV_W
    va_ref[...] = y[:, off:off + KV_W].astype(BF16)
    off += KV_W
    qb_ref[...] = (y[:, off:off + Q_W] * scale).astype(BF16)
    off += Q_W
    kb_ref[...] = y[:, off:off + KV_W].astype(BF16)
    off += KV_W
    vb_ref[...] = y[:, off:off + KV_W].astype(BF16)


def _proj_call(h, w, qg, kg, cos, sa, sb, S, tm):
    T, D = h.shape
    n_pos = S // tm
    row = lambda i: (i, 0)
    const = lambda i: (0, 0)
    pos = lambda i: (i % n_pos, 0)
    outs = [jax.ShapeDtypeStruct((T, w_), BF16) for w_ in (Q_W, KV_W, KV_W, Q_W, KV_W, KV_W)]
    return pl.pallas_call(
        _proj_kernel,
        grid=(T // tm,),
        in_specs=[pl.BlockSpec((tm, D), row),
                  pl.BlockSpec((D, IN_WIDTH), const),
                  pl.BlockSpec((1, PAIR_W), const),
                  pl.BlockSpec((1, PAIR_W), const),
                  pl.BlockSpec((tm, PAIR_W), pos),
                  pl.BlockSpec((tm, PAIR_W), pos),
                  pl.BlockSpec((tm, PAIR_W), pos)],
        out_specs=[pl.BlockSpec((tm, o.shape[1]), row) for o in outs],
        out_shape=outs,
        compiler_params=_params("parallel"),
        name="in_proj",
    )(h, w, qg, kg, cos, sa, sb)


def _attn_a_kernel(q_ref, k_ref, v_ref, g_ref, o_ref):
    tq = q_ref.shape[0]
    k = k_ref[...]
    v = v_ref[...]
    lo = lax.broadcasted_iota(jnp.int32, (tq, PAIR_W), 1) < HEAD_DIM
    nt = (((1,), (1,)), ((), ()))
    for i in range(Q_W // PAIR_W):
        qp = q_ref[:, i * PAIR_W:(i + 1) * PAIR_W]
        halves = []
        for grp in range(2):
            qz = jnp.where(lo if grp == 0 else jnp.logical_not(lo), qp, jnp.zeros_like(qp))
            s = lax.dot_general(qz, k, nt, preferred_element_type=F32)
            m = jnp.max(s, axis=-1, keepdims=True)
            p = jnp.exp(s - m)
            l = jnp.sum(p, axis=-1, keepdims=True)
            halves.append(jnp.dot(p.astype(BF16), v, preferred_element_type=F32) / l)
        o = jnp.where(lo, halves[0], halves[1])
        o = o * _half_rms(o, lo) * g_ref[:, i * PAIR_W:(i + 1) * PAIR_W]
        o_ref[:, i * PAIR_W:(i + 1) * PAIR_W] = o.astype(BF16)


def _attn_a_call(q, k, v, g, B, S, tq):
    T = q.shape[0]
    nq = S // tq
    return pl.pallas_call(
        _attn_a_kernel,
        grid=(B, nq),
        in_specs=[pl.BlockSpec((tq, Q_W), lambda b, i: (b * nq + i, 0)),
                  pl.BlockSpec((S, KV_W), lambda b, i: (b, 0)),
                  pl.BlockSpec((S, KV_W), lambda b, i: (b, 0)),
                  pl.BlockSpec((1, Q_W), lambda b, i: (0, 0))],
        out_specs=pl.BlockSpec((tq, Q_W), lambda b, i: (b * nq + i, 0)),
        out_shape=jax.ShapeDtypeStruct((T, Q_W), BF16),
        compiler_params=_params("parallel", "parallel"),
        name="attn_global",
    )(q, k, v, g)


def _attn_b_kernel(sink_ref, q_ref, k_ref, v_ref, g_ref, o_ref, *, slopes):
    n_sub = q_ref.shape[0] // BLOCK
    for u in range(n_sub):
        _attn_b_block(sink_ref, q_ref, k_ref, v_ref, g_ref, o_ref, slopes,
                      pl.program_id(1) * n_sub + u, slice(u * BLOCK, (u + 1) * BLOCK))


def _attn_b_block(sink_ref, q_ref, k_ref, v_ref, g_ref, o_ref, slopes, j, rows):
    nb = k_ref.shape[0] // BLOCK
    span = BLOCK + 2 * WINDOW
    ks, vs = [], []
    for c in (-1, 0, 1):
        jc = j + c
        start = pl.multiple_of(jnp.clip(jc, 0, nb - 1) * BLOCK, BLOCK)
        ks.append(k_ref[pl.ds(start, BLOCK), :])
        vs.append(v_ref[pl.ds(start, BLOCK), :])
    kslab = jnp.concatenate(ks, axis=0)
    vslab = jnp.concatenate(vs, axis=0)
    qi = lax.broadcasted_iota(jnp.int32, (BLOCK, span), 0)
    ki = lax.broadcasted_iota(jnp.int32, (BLOCK, span), 1)
    dist = jnp.abs(qi + WINDOW - ki)
    s_pos = (j - 1) * BLOCK + ki
    in_seq = jnp.logical_and(s_pos >= 0, s_pos < nb * BLOCK)
    valid = jnp.logical_and(dist <= WINDOW, in_seq)
    distf = dist.astype(F32)

    lo = lax.broadcasted_iota(jnp.int32, (BLOCK, PAIR_W), 1) < HEAD_DIM
    hi = jnp.logical_not(lo)
    n_pairs = Q_W // PAIR_W
    heads = [(i, grp) for grp in range(2) for i in range(n_pairs)]
    qs = []
    for i, grp in heads:
        qp = q_ref[rows, i * PAIR_W:(i + 1) * PAIR_W]
        qs.append(jnp.where(lo if grp == 0 else hi, qp, jnp.zeros_like(qp)))
    s_all = lax.dot_general(jnp.concatenate(qs, axis=0), kslab, (((1,), (1,)), ((), ())),
                            preferred_element_type=F32)
    ps, inv = [], []
    for n, (i, grp) in enumerate(heads):
        hd = i + n_pairs * grp
        s = s_all[n * BLOCK:(n + 1) * BLOCK] - slopes[hd] * distf
        s = jnp.where(valid, s, NEG_INF)
        sink = sink_ref[hd]
        m = jnp.maximum(jnp.max(s, axis=-1, keepdims=True), sink)
        p = jnp.exp(s - m)
        inv.append(1.0 / (jnp.sum(p, axis=-1, keepdims=True) + jnp.exp(sink - m)))
        ps.append(p.astype(BF16))
    o_all = jnp.dot(jnp.concatenate(ps, axis=0), vslab, preferred_element_type=F32)
    for i in range(n_pairs):
        o0 = o_all[i * BLOCK:(i + 1) * BLOCK] * inv[i]
        o1 = o_all[(i + n_pairs) * BLOCK:(i + n_pairs + 1) * BLOCK] * inv[i + n_pairs]
        o = jnp.where(lo, o0, o1)
        o = o * _half_rms(o, lo) * g_ref[:, i * PAIR_W:(i + 1) * PAIR_W]
        o_ref[rows, i * PAIR_W:(i + 1) * PAIR_W] = o.astype(BF16)


def _attn_b_call(sink, q, k, v, g, B, S, n_sub):
    T = q.shape[0]
    nb = S // (BLOCK * n_sub)
    tqb = BLOCK * n_sub
    kern = functools.partial(_attn_b_kernel, slopes=_alibi_slopes(N_B_HEADS))
    return pl.pallas_call(
        kern,
        grid=(B, nb),
        in_specs=[pl.BlockSpec(memory_space=pltpu.SMEM),
                  pl.BlockSpec((tqb, Q_W), lambda b, i: (b * nb + i, 0)),
                  pl.BlockSpec((S, KV_W), lambda b, i: (b, 0)),
                  pl.BlockSpec((S, KV_W), lambda b, i: (b, 0)),
                  pl.BlockSpec((1, Q_W), lambda b, i: (0, 0))],
        out_specs=pl.BlockSpec((tqb, Q_W), lambda b, i: (b * nb + i, 0)),
        out_shape=jax.ShapeDtypeStruct((T, Q_W), BF16),
        compiler_params=_params("parallel", "parallel"),
        name="attn_window",
    )(sink, q, k, v, g)


def _oproj_kernel(oa_ref, ob_ref, wa_ref, wb_ref, h_ref, g_ref, b_ref, o_ref, *, alpha):
    mix = jnp.dot(oa_ref[...], wa_ref[...], preferred_element_type=F32)
    mix = mix + jnp.dot(ob_ref[...], wb_ref[...], preferred_element_type=F32)
    o_ref[...] = _layer_norm(alpha * h_ref[...] + mix, g_ref[...], b_ref[...])


def _oproj_call(oa, ob, wa, wb, h, g, b, alpha, tm):
    T, D = h.shape
    row = lambda i: (i, 0)
    const = lambda i: (0, 0)
    return pl.pallas_call(
        functools.partial(_oproj_kernel, alpha=alpha),
        grid=(T // tm,),
        in_specs=[pl.BlockSpec((tm, Q_W), row), pl.BlockSpec((tm, Q_W), row),
                  pl.BlockSpec((Q_W, D), const), pl.BlockSpec((Q_W, D), const),
                  pl.BlockSpec((tm, D), row),
                  pl.BlockSpec((1, D), const), pl.BlockSpec((1, D), const)],
        out_specs=pl.BlockSpec((tm, D), row),
        out_shape=jax.ShapeDtypeStruct((T, D), F32),
        compiler_params=_params("parallel"),
        name="out_proj_ln",
    )(oa, ob, wa, wb, h, g, b)


def _peer_scores_kernel(h_ref, wq_ref, keys_ref, st_ref):
    q = jnp.dot(h_ref[...].astype(BF16), wq_ref[...], preferred_element_type=F32).astype(BF16)
    half = PEER_DKEY // 2
    for n in range(2 * PEER_HEADS):
        qs = q[:, n * half:(n + 1) * half]
        st_ref[n] = lax.dot_general(keys_ref[n], qs, (((1,), (1,)), ((), ())),
                                    preferred_element_type=F32)


def _peer_scores_call(h, wq, keys, tm):
    T, D = h.shape
    n_sets = 2 * PEER_HEADS
    half = PEER_DKEY // 2
    return pl.pallas_call(
        _peer_scores_kernel,
        grid=(T // tm,),
        in_specs=[pl.BlockSpec((tm, D), lambda i: (i, 0)),
                  pl.BlockSpec((D, PEER_HEADS * PEER_DKEY), lambda i: (0, 0)),
                  pl.BlockSpec((n_sets, PEER_NKEYS, half), lambda i: (0, 0, 0))],
        out_specs=pl.BlockSpec((n_sets, PEER_NKEYS, tm), lambda i: (0, 0, i)),
        out_shape=jax.ShapeDtypeStruct((n_sets, PEER_NKEYS, T), F32),
        compiler_params=_params("parallel"),
        name="peer_scores",
    )(h, wq, keys)


ROUTE_TOKENS = SUBLANES * LANES
ROUTE_PASS = 4
ROUTE_PAIRS = [(r1, r2) for r1 in range(PEER_TOPK) for r2 in range(PEER_TOPK)
               if (r1 + 1) * (r2 + 1) <= PEER_TOPK]


def _slab(i):
    if isinstance(i, int):
        return pl.ds(i * SUBLANES, SUBLANES)
    return pl.ds(pl.multiple_of(i * SUBLANES, SUBLANES), SUBLANES)


def _insert_sorted(tops, x):
    out = []
    for t in tops[:-1]:
        out.append(jnp.maximum(t, x))
        x = jnp.minimum(t, x)
    out.append(jnp.maximum(tops[-1], x))
    return out


def _router_kernel(st_ref, r1_ref, c1_ref, n2_ref, e2_ref,
                   s1_ref, s2_ref, a1_ref, a2_ref, rk1_ref, n2k_ref, z_ref, bad_ref,
                   cand_ref, n2r_ref, out_ref):
    K = PEER_TOPK
    neg = -jnp.inf
    shape = (SUBLANES, LANES)
    ones = jnp.ones(shape, F32)
    zeros = jnp.zeros(shape, F32)

    def load_keys_in(src_set, dst_ref):
        for g in range(SUBLANES):
            dst_ref[pl.ds(g, PEER_NKEYS, stride=SUBLANES), :] = st_ref[src_set, :, g * LANES:(g + 1) * LANES]

    def store_keys_out(dst_ref, h):
        for g in range(SUBLANES):
            dst_ref[h, :, g * LANES:(g + 1) * LANES] = out_ref[pl.ds(g, PEER_NKEYS, stride=SUBLANES), :]

    def top_values(value_of, n, a_ref):
        thr = None
        for p in range(K // ROUTE_PASS):
            groups = [[jnp.full(shape, neg, F32)] * ROUTE_PASS for _ in range(2)]
            for k in range(n):
                x = value_of(k)
                if thr is not None:
                    x = jnp.where(x < thr, x, neg)
                groups[k % 2] = _insert_sorted(groups[k % 2], x)
            tops = groups[0]
            for x in groups[1]:
                tops = _insert_sorted(tops, x)
            for i, t in enumerate(tops):
                a_ref[_slab(p * ROUTE_PASS + i), :] = t
            thr = tops[-1]

    def adjacent_equal(a_ref):
        bad = zeros
        prev = a_ref[_slab(0), :]
        for r in range(1, K):
            cur = a_ref[_slab(r), :]
            bad = jnp.where(cur == prev, ones, bad)
            prev = cur
        return bad

    def cand(r1, r2):
        return a1_ref[_slab(r1), :] + a2_ref[_slab(r2), :]

    def finish_candidates(selected):
        top = cand(0, 0)
        z = zeros
        counts = [zeros] * K
        for idx, (r1, r2) in enumerate(ROUTE_PAIRS):
            s = selected(idx)
            z = z + s * jnp.exp(cand(r1, r2) - top)
            counts[r2] = counts[r2] + s
        z_ref[...] = z
        for r2 in range(K):
            n2r_ref[_slab(r2), :] = counts[r2]

    def head(h, carry):
        load_keys_in(2 * h, s1_ref)
        load_keys_in(2 * h + 1, s2_ref)

        top_values(lambda k: s1_ref[_slab(k), :], PEER_NKEYS, a1_ref)
        top_values(lambda k: s2_ref[_slab(k), :], PEER_NKEYS, a2_ref)
        a1 = [a1_ref[_slab(r), :] for r in range(K)]
        n_ge1 = zeros
        for k in range(PEER_NKEYS):
            s = s1_ref[_slab(k), :]
            rank = jnp.full(shape, float(K), F32)
            for r in reversed(range(K)):
                rank = jnp.where(s >= a1[r], float(r), rank)
            rk1_ref[_slab(k), :] = rank
            n_ge1 = n_ge1 + jnp.where(rank < float(K), ones, zeros)
        top_values(lambda i: cand(*ROUTE_PAIRS[i]), len(ROUTE_PAIRS), cand_ref)
        tau = cand_ref[_slab(K - 1), :]
        finish_candidates(lambda i: jnp.where(cand(*ROUTE_PAIRS[i]) >= tau, ones, zeros))
        a2 = [a2_ref[_slab(r), :] for r in range(K)]
        counts = [n2r_ref[_slab(r), :] for r in range(K)]
        n_sel = zeros
        for r in range(K):
            n_sel = n_sel + counts[r]
        n_ge2 = zeros
        for k in range(PEER_NKEYS):
            s = s2_ref[_slab(k), :]
            v = zeros
            for r in reversed(range(K)):
                v = jnp.where(s >= a2[r], counts[r], v)
            n2k_ref[_slab(k), :] = v
            n_ge2 = n_ge2 + jnp.where(s >= a2[K - 1], ones, zeros)
        bad = adjacent_equal(a1_ref) + adjacent_equal(a2_ref) + adjacent_equal(cand_ref)
        bad = bad + jnp.where(n_ge1 == float(K), zeros, ones) + jnp.where(n_ge2 == float(K), zeros, ones)
        bad = bad + jnp.where(n_sel == float(K), zeros, ones)

        bad_ref[...] = bad

        top1 = a1_ref[_slab(0), :]
        top2 = a2_ref[_slab(0), :]
        inv_z = 1.0 / z_ref[...]
        for k in range(PEER_NKEYS):
            out_ref[_slab(k), :] = rk1_ref[_slab(k), :]
        store_keys_out(r1_ref, h)
        for k in range(PEER_NKEYS):
            out_ref[_slab(k), :] = jnp.exp(s1_ref[_slab(k), :] - top1) * inv_z
        store_keys_out(c1_ref, h)
        for k in range(PEER_NKEYS):
            out_ref[_slab(k), :] = n2k_ref[_slab(k), :]
        store_keys_out(n2_ref, h)
        for k in range(PEER_NKEYS):
            out_ref[_slab(k), :] = jnp.exp(s2_ref[_slab(k), :] - top2)
        store_keys_out(e2_ref, h)

        def fix_group(g, carry2):
            @pl.when(jnp.sum(bad_ref[pl.ds(g, 1), :]) > 0.0)
            def _():
                _route_group_exact(st_ref, r1_ref, c1_ref, n2_ref, e2_ref, h,
                                   pl.ds(pl.multiple_of(g * LANES, LANES), LANES))
            return carry2

        @pl.when(jnp.sum(bad) > 0.0)
        def _():
            lax.fori_loop(0, SUBLANES, fix_group, 0)

        return carry

    lax.fori_loop(0, PEER_HEADS, head, 0)


def _route_group_exact(st_ref, r1_ref, c1_ref, n2_ref, e2_ref, h, lanes):
    tt = LANES
    K = PEER_TOPK
    n_lead = 4
    neg = -jnp.inf
    kio_i = lax.broadcasted_iota(jnp.int32, (PEER_NKEYS, tt), 0)
    rio = lax.broadcasted_iota(jnp.int32, (K, tt), 0)
    blk = kio_i // K
    rr = kio_i % K
    pos = jnp.where(blk < n_lead, rr * K + blk, (blk - n_lead) * K + rr).astype(F32)
    dup = jnp.logical_and(blk >= n_lead, rr < n_lead)
    kio = kio_i.astype(F32)

    def top16(s):
        rank = jnp.full((PEER_NKEYS, tt), float(K), F32)
        arr = jnp.zeros((K, tt), F32)
        for r in range(K):
            m = jnp.max(s, axis=0, keepdims=True)
            first = jnp.min(jnp.where(s == m, kio, float(PEER_NKEYS)), axis=0, keepdims=True)
            hit = kio == first
            rank = jnp.where(hit, float(r), rank)
            s = jnp.where(hit, neg, s)
            arr = jnp.where(rio == r, m, arr)
        return rank, arr

    s1 = st_ref[2 * h, :, lanes]
    s2 = st_ref[2 * h + 1, :, lanes]
    rank1, a1 = top16(s1)
    rank2, a2 = top16(s2)
    blocks = ([a1 + a2[r2:r2 + 1] for r2 in range(n_lead)]
              + [a1[r1:r1 + 1] + a2 for r1 in range(n_lead)])
    c0 = jnp.where(dup, neg, jnp.concatenate(blocks, axis=0))
    c = c0
    sel = jnp.zeros((PEER_NKEYS, tt), F32)
    for _ in range(K):
        m = jnp.max(c, axis=0, keepdims=True)
        first = jnp.min(jnp.where(c == m, pos, float(K * K)), axis=0, keepdims=True)
        hit = pos == first
        sel = jnp.where(hit, 1.0, sel)
        c = jnp.where(hit, neg, c)
    top = a1[0:1] + a2[0:1]
    z = jnp.sum(jnp.where(sel > 0.0, jnp.exp(c0 - top), 0.0), axis=0, keepdims=True)
    tail = sel[n_lead * K:(n_lead + 1) * K]
    for r1 in range(1, n_lead):
        tail = tail + sel[(n_lead + r1) * K:(n_lead + r1 + 1) * K]
    n2 = jnp.zeros((PEER_NKEYS, tt), F32)
    for r2 in range(K):
        if r2 < n_lead:
            cnt = jnp.sum(sel[r2 * K:(r2 + 1) * K], axis=0, keepdims=True)
        else:
            cnt = tail[r2:r2 + 1]
        n2 = jnp.where(rank2 == float(r2), cnt, n2)
    r1_ref[h, :, lanes] = rank1
    c1_ref[h, :, lanes] = jnp.exp(s1 - a1[0:1]) / z
    n2_ref[h, :, lanes] = n2
    e2_ref[h, :, lanes] = jnp.exp(s2 - a2[0:1])


def _router_call(st):
    T = st.shape[-1]
    tt = ROUTE_TOKENS
    shape = (PEER_HEADS, PEER_NKEYS, T)
    outs = [jax.ShapeDtypeStruct(shape, F32)] * 4
    spec = pl.BlockSpec((PEER_HEADS, PEER_NKEYS, tt), lambda i: (0, 0, i))
    keys = pltpu.VMEM((PEER_NKEYS * SUBLANES, LANES), F32)
    tops = pltpu.VMEM((PEER_TOPK * SUBLANES, LANES), F32)
    one = pltpu.VMEM((SUBLANES, LANES), F32)
    return pl.pallas_call(
        _router_kernel,
        grid=(T // tt,),
        in_specs=[pl.BlockSpec((2 * PEER_HEADS, PEER_NKEYS, tt), lambda i: (0, 0, i))],
        out_specs=[spec] * 4,
        out_shape=outs,
        scratch_shapes=[keys, keys, tops, tops, keys, keys, one, one, tops, tops, keys],
        compiler_params=_params("parallel"),
        name="peer_router",
    )(st)


PEER_CHUNK_ROWS = 256
PEER_IB_GROUP = 2
PEER_PART_ROWS = 64


def _peer_dense_kernel(h_ref, r1_ref, c1_ref, n2_ref, e2_ref, u_first_ref, u_a_ref, u_b_ref,
                       vt_a_ref, vt_b_ref, vt_last_ref, g_ref, b_ref, o_ref,
                       xt_ref, acc_ref, n2s_ref, e2s_ref, act_a_ref, act_b_ref, gt_a_ref, gt_b_ref,
                       *, alpha):
    j = pl.program_id(1)
    tt = h_ref.shape[0]
    te = vt_a_ref.shape[1]
    n_sub = te // PEER_NKEYS
    tile = (PEER_PART_ROWS, LANES)
    sqrt_half = float(np.sqrt(0.5))

    def gelu_bf16(a):
        return (a * (0.5 + 0.5 * lax.erf(a * sqrt_half))).astype(BF16)

    @pl.when(j == 0)
    def _():
        xt_ref[:, :tt] = h_ref[...].T.astype(BF16)
        acc_ref[:, :tt] = jnp.zeros((acc_ref.shape[0], tt), F32)
        n2s_ref[:, :, :tt] = n2_ref[...].astype(BF16)
        e2s_ref[:, :, :tt] = e2_ref[...].astype(BF16)
        gt_b_ref[:, :tt] = jnp.zeros((te, tt), BF16)
        act_a_ref[:, :tt] = gelu_bf16(jnp.dot(_unpack_bf16(u_first_ref[...]), xt_ref[:, :tt],
                                              preferred_element_type=F32))

    def phase(u_next_ref, act_next_ref, act_cur_ref, gt_cur_ref, vt_prev_ref, gt_prev_ref, key_off):
        def chunk(c, carry):
            rows = pl.ds(pl.multiple_of(c * PEER_CHUNK_ROWS, PEER_CHUNK_ROWS), PEER_CHUNK_ROWS)
            prows = pl.ds(pl.multiple_of(c * (PEER_CHUNK_ROWS // 2), PEER_CHUNK_ROWS // 2),
                          PEER_CHUNK_ROWS // 2)
            n_ib = PEER_CHUNK_ROWS // PEER_NKEYS
            for grp in range(n_ib // PEER_IB_GROUP):
                ibs = [c * n_ib + grp * PEER_IB_GROUP + s for s in range(PEER_IB_GROUP)]
                r1_rows = [[r1_ref[hd, pl.ds(key_off + ib, 1), :] for hd in range(PEER_HEADS)]
                           for ib in ibs]
                c1_rows = [[c1_ref[hd, pl.ds(key_off + ib, 1), :] for hd in range(PEER_HEADS)]
                           for ib in ibs]
                for tg in range(tt // LANES):
                    lanes = slice(tg * LANES, (tg + 1) * LANES)
                    for part in range(PEER_NKEYS // PEER_PART_ROWS):
                        krows = slice(part * PEER_PART_ROWS, (part + 1) * PEER_PART_ROWS)
                        ws = [None] * PEER_IB_GROUP
                        for hd in range(PEER_HEADS):
                            n2t = n2s_ref[hd, krows, lanes]
                            e2t = e2s_ref[hd, krows, lanes]
                            for s in range(PEER_IB_GROUP):
                                r1 = jnp.broadcast_to(r1_rows[s][hd][:, lanes], tile).astype(BF16)
                                c1 = jnp.broadcast_to(c1_rows[s][hd][:, lanes], tile).astype(BF16)
                                term = jnp.where(r1 < n2t, e2t, jnp.zeros(tile, BF16)) * c1
                                ws[s] = term if ws[s] is None else ws[s] + term
                        for s in range(PEER_IB_GROUP):
                            erows = pl.ds(pl.multiple_of(ibs[s] * PEER_NKEYS + part * PEER_PART_ROWS,
                                                         PEER_PART_ROWS), PEER_PART_ROWS)
                            gt_cur_ref[erows, lanes] = act_cur_ref[erows, lanes] * ws[s]
            act_next_ref[rows, :tt] = gelu_bf16(jnp.dot(_unpack_bf16(u_next_ref[prows, :]),
                                                        xt_ref[:, :tt], preferred_element_type=F32))
            acc_ref[rows, :tt] += jnp.dot(_unpack_bf16(vt_prev_ref[prows, :]), gt_prev_ref[:, :tt],
                                        preferred_element_type=F32)
            return carry

        lax.fori_loop(0, te // PEER_CHUNK_ROWS, chunk, 0)

    phase(u_a_ref, act_b_ref, act_a_ref, gt_a_ref, vt_a_ref, gt_b_ref, 0)
    phase(u_b_ref, act_a_ref, act_b_ref, gt_b_ref, vt_b_ref, gt_a_ref, n_sub)

    @pl.when(j == pl.num_programs(1) - 1)
    def _():
        acc = acc_ref[:, :tt] + jnp.dot(_unpack_bf16(vt_last_ref[...]), gt_b_ref[:, :tt],
                                     preferred_element_type=F32)
        z = alpha * h_ref[...] + acc.T
        o_ref[...] = _layer_norm(z, g_ref[...], b_ref[...])


def _peer_dense_call(h, r1, c1, n2, e2, u, vt, g, b, alpha, tt, te):
    T, D = h.shape
    n_sub = te // PEER_NKEYS
    n_blk = PEER_EXPERTS // te
    n_steps = n_blk // 2
    assert D == te and n_blk % 2 == 0
    spec1 = pl.BlockSpec((PEER_HEADS, 2 * n_sub, tt), lambda i, j: (0, j, i))
    spec2 = pl.BlockSpec((PEER_HEADS, PEER_NKEYS, tt), lambda i, j: (0, 0, i))
    last = n_blk - 1
    ttp = tt + LANES
    u_spec = lambda f: pl.BlockSpec((te // 2, D), lambda i, j: (f(j), 0))
    vt_spec = lambda f: pl.BlockSpec((D // 2, te), lambda i, j: (0, f(j)))
    return pl.pallas_call(
        functools.partial(_peer_dense_kernel, alpha=alpha),
        grid=(T // tt, n_steps),
        in_specs=[pl.BlockSpec((tt, D), lambda i, j: (i, 0)),
                  spec1, spec1, spec2, spec2,
                  u_spec(lambda j: 0),
                  u_spec(lambda j: 2 * j + 1),
                  u_spec(lambda j: jnp.minimum(2 * j + 2, last)),
                  vt_spec(lambda j: jnp.maximum(2 * j - 1, 0)),
                  vt_spec(lambda j: 2 * j),
                  vt_spec(lambda j: last),
                  pl.BlockSpec((1, D), lambda i, j: (0, 0)),
                  pl.BlockSpec((1, D), lambda i, j: (0, 0))],
        out_specs=pl.BlockSpec((tt, D), lambda i, j: (i, 0)),
        out_shape=jax.ShapeDtypeStruct((T, D), F32),
        scratch_shapes=[pltpu.VMEM((D, ttp), BF16), pltpu.VMEM((D, ttp), F32),
                        pltpu.VMEM((PEER_HEADS, PEER_NKEYS, ttp), BF16),
                        pltpu.VMEM((PEER_HEADS, PEER_NKEYS, ttp), BF16),
                        pltpu.VMEM((te, ttp), BF16), pltpu.VMEM((te, ttp), BF16),
                        pltpu.VMEM((te, ttp), BF16), pltpu.VMEM((te, ttp), BF16)],
        compiler_params=_params("parallel", "arbitrary"),
        name="peer_dense",
    )(h, r1, c1, n2, e2, u, u, u, vt, vt, vt, g, b)


def _pair_perm():
    idx = []
    for i in range(N_A_HEADS // 2):
        for grp in range(2):
            hd = i + (N_A_HEADS // 2) * grp
            idx.extend(range(hd * HEAD_DIM, (hd + 1) * HEAD_DIM))
    return np.asarray(idx, dtype=np.int32)


def _rope_tables(S):
    pos = np.arange(S)
    row = (pos // GRID_W).astype(np.float32)
    col = (pos % GRID_W).astype(np.float32)
    inv_freq = jnp.asarray(ROPE_THETA, F32) ** (-jnp.arange(ROPE_FREQS, dtype=F32) / ROPE_FREQS)
    ang_r = jnp.asarray(row)[:, None] * inv_freq
    ang_c = jnp.asarray(col)[:, None] * inv_freq
    cr, sr, cc, sc = jnp.cos(ang_r), jnp.sin(ang_r), jnp.cos(ang_c), jnp.sin(ang_c)
    zero = jnp.zeros_like(sr)
    cos = jnp.concatenate([cr, cr, cc, cc], axis=-1)
    sa = jnp.concatenate([-sr, zero, -sc, zero], axis=-1)
    sb = jnp.concatenate([zero, sr, zero, sc], axis=-1)
    tile2 = lambda t: jnp.concatenate([t, t], axis=-1)
    return tile2(cos), tile2(sa), tile2(sb)


def kernel(x, ln_in_g, ln_in_b, w_in, qn_g, kn_g, sink, gn_a_g, gn_b_g, w_o, ln1_g, ln1_b,
           peer_wq, peer_keys, peer_u, peer_v, ln2_g, ln2_b):
    B, S, D = x.shape
    depth = w_in.shape[0]
    T = B * S
    alpha = float((2.0 * depth) ** 0.25)
    tm = min(1024, S)
    tq = min(512, S)
    n_win = min(4, S // BLOCK)
    tt_dense = min(512, T)
    te = 1024

    perm = _pair_perm()
    col_perm = np.concatenate([perm, Q_W + np.arange(2 * KV_W, dtype=np.int32),
                               Q_W + 2 * KV_W + perm,
                               2 * Q_W + 2 * KV_W + np.arange(2 * KV_W, dtype=np.int32)])
    cos, sa, sb = _rope_tables(S)
    row2 = lambda v: v.reshape(1, -1)
    tile2 = lambda v: jnp.concatenate([v, v]).reshape(1, -1)

    h = _ln_call(x.reshape(T, D), row2(ln_in_g), row2(ln_in_b), tm)
    for l in range(depth):
        w = w_in[l][:, col_perm].astype(BF16)
        qa, ka, va, qb, kb, vb = _proj_call(h, w, tile2(qn_g[l]), tile2(kn_g[l]), cos, sa, sb, S, tm)
        oa = _attn_a_call(qa, ka, va, row2(gn_a_g[l][perm]), B, S, tq)
        ob = _attn_b_call(sink[l], qb, kb, vb, row2(gn_b_g[l][perm]), B, S, n_win)
        wa = w_o[l][:Q_W][perm].astype(BF16)
        wb = w_o[l][Q_W:][perm].astype(BF16)
        h = _oproj_call(oa, ob, wa, wb, h, row2(ln1_g[l]), row2(ln1_b[l]), alpha, tm)
        keys = peer_keys[l].reshape(2 * PEER_HEADS, PEER_NKEYS, PEER_DKEY // 2).astype(BF16)
        st = _peer_scores_call(h, peer_wq[l].astype(BF16), keys, tm)
        r1, c1, n2, e2 = _router_call(st)
        h = _peer_dense_call(h, r1, c1, n2, e2, _pack_bf16_rows(peer_u, l),
                             _pack_bf16_rows(peer_v, l, transpose=True),
                             row2(ln2_g[l]), row2(ln2_b[l]), alpha, tt_dense, te)
    return h.reshape(B, S, D)
```

```python
import functools

import numpy as np
import jax
import jax.numpy as jnp
from jax import lax
from jax.experimental import pallas as pl
from jax.experimental.pallas import tpu as pltpu

D_MODEL = 1024
HEAD_DIM = 64
N_A_HEADS = 8
N_B_HEADS = 8
LANES = 128
SUBLANES = 8
PAIR_W = 2 * HEAD_DIM
Q_W = N_A_HEADS * HEAD_DIM
KV_W = 2 * HEAD_DIM
IN_WIDTH = 2 * (Q_W + 2 * KV_W)
BLOCK = 128
WINDOW = 128
GRID_W = 64
ROPE_THETA = 10000.0
ROPE_FREQS = HEAD_DIM // 4

PEER_HEADS = 8
PEER_NKEYS = 128
PEER_EXPERTS = PEER_NKEYS * PEER_NKEYS
PEER_DKEY = 256
PEER_TOPK = 16

LN_EPS = 1e-5
RMS_EPS = 1e-6
NEG_INF = -1e30

VMEM_LIMIT_BYTES = 56 * 1024 * 1024

BF16 = jnp.bfloat16
F32 = jnp.float32


def _alibi_slopes(n):
    return [float(2.0 ** (-((i + 1) * 8.0 / n))) for i in range(n)]


def _params(*sem):
    return pltpu.CompilerParams(dimension_semantics=sem,
                                vmem_limit_bytes=VMEM_LIMIT_BYTES)


def _pack_kernel(w_ref, o_ref, *, transpose):
    w = w_ref[...]
    if transpose:
        w = w.T
    o_ref[...] = pltpu.bitcast(w.astype(BF16), jnp.uint32)


def _pack_bf16_rows(stacked, layer, transpose=False):
    _, n, c = stacked.shape
    blk = 1024
    assert n % blk == 0 and c % blk == 0
    if transpose:
        grid, out_map = (n // blk, c // blk), (lambda i, j: (j, i))
        out_shape = (c // 2, n)
    else:
        grid, out_map = (n // blk, c // blk), (lambda i, j: (i, j))
        out_shape = (n // 2, c)
    return pl.pallas_call(
        functools.partial(_pack_kernel, transpose=transpose),
        grid=grid,
        in_specs=[pl.BlockSpec((None, blk, blk), lambda i, j: (layer, i, j))],
        out_specs=pl.BlockSpec((blk // 2, blk), out_map),
        out_shape=jax.ShapeDtypeStruct(out_shape, jnp.uint32),
        compiler_params=_params("parallel", "parallel"),
        name="pack_bf16",
    )(stacked)


def _unpack_bf16(words):
    return pltpu.bitcast(words, BF16)


def _layer_norm(z, g, b):
    mu = jnp.mean(z, axis=-1, keepdims=True)
    zc = z - mu
    var = jnp.mean(zc * zc, axis=-1, keepdims=True)
    return zc * lax.rsqrt(var + LN_EPS) * g + b


def _half_rms(t, lo):
    sq = t * t
    s_lo = jnp.sum(jnp.where(lo, sq, 0.0), axis=-1, keepdims=True)
    s_hi = jnp.sum(jnp.where(lo, 0.0, sq), axis=-1, keepdims=True)
    return jnp.where(lo, lax.rsqrt(s_lo / HEAD_DIM + RMS_EPS),
                     lax.rsqrt(s_hi / HEAD_DIM + RMS_EPS))


def _ln_kernel(x_ref, g_ref, b_ref, o_ref):
    o_ref[...] = _layer_norm(x_ref[...], g_ref[...], b_ref[...])


def _ln_call(x, g, b, tm):
    T, D = x.shape
    return pl.pallas_call(
        _ln_kernel,
        grid=(T // tm,),
        in_specs=[pl.BlockSpec((tm, D), lambda i: (i, 0)),
                  pl.BlockSpec((1, D), lambda i: (0, 0)),
                  pl.BlockSpec((1, D), lambda i: (0, 0))],
        out_specs=pl.BlockSpec((tm, D), lambda i: (i, 0)),
        out_shape=jax.ShapeDtypeStruct((T, D), F32),
        compiler_params=_params("parallel"),
        name="ln_in",
    )(x, g, b)


def _proj_kernel(h_ref, w_ref, qg_ref, kg_ref, cos_ref, sa_ref, sb_ref,
                 qa_ref, ka_ref, va_ref, qb_ref, kb_ref, vb_ref):
    tm = h_ref.shape[0]
    y = jnp.dot(h_ref[...].astype(BF16), w_ref[...], preferred_element_type=F32)
    lo = lax.broadcasted_iota(jnp.int32, (tm, PAIR_W), 1) < HEAD_DIM
    cos, sa, sb = cos_ref[...], sa_ref[...], sb_ref[...]
    scale = HEAD_DIM ** -0.5

    def norm_rope(t, g):
        xn = t * _half_rms(t, lo) * g
        return xn * cos + pltpu.roll(xn, PAIR_W - ROPE_FREQS, 1) * sa + pltpu.roll(xn, ROPE_FREQS, 1) * sb

    for i in range(Q_W // PAIR_W):
        t = y[:, i * PAIR_W:(i + 1) * PAIR_W]
        qa_ref[:, i * PAIR_W:(i + 1) * PAIR_W] = (norm_rope(t, qg_ref[...]) * scale).astype(BF16)
    off = Q_W
    ka_ref[...] = norm_rope(y[:, off:off + KV_W], kg_ref[...]).astype(BF16)
    off += KV_W
    va_ref[...] = y[:, off:off + KV_W].astype(BF16)
    off += KV_W
    qb_ref[...] = (y[:, off:off + Q_W] * scale).astype(BF16)
    off += Q_W
    kb_ref[...] = y[:, off:off + KV_W].astype(BF16)
    off += KV_W
    vb_ref[...] = y[:, off:off + KV_W].astype(BF16)


def _proj_call(h, w, qg, kg, cos, sa, sb, S, tm):
    T, D = h.shape
    n_pos = S // tm
    row = lambda i: (i, 0)
    const = lambda i: (0, 0)
    pos = lambda i: (i % n_pos, 0)
    outs = [jax.ShapeDtypeStruct((T, w_), BF16) for w_ in (Q_W, KV_W, KV_W, Q_W, KV_W, KV_W)]
    return pl.pallas_call(
        _proj_kernel,
        grid=(T // tm,),
        in_specs=[pl.BlockSpec((tm, D), row),
                  pl.BlockSpec((D, IN_WIDTH), const),
                  pl.BlockSpec((1, PAIR_W), const),
                  pl.BlockSpec((1, PAIR_W), const),
                  pl.BlockSpec((tm, PAIR_W), pos),
                  pl.BlockSpec((tm, PAIR_W), pos),
                  pl.BlockSpec((tm, PAIR_W), pos)],
        out_specs=[pl.BlockSpec((tm, o.shape[1]), row) for o in outs],
        out_shape=outs,
        compiler_params=_params("parallel"),
        name="in_proj",
    )(h, w, qg, kg, cos, sa, sb)


def _attn_a_kernel(q_ref, k_ref, v_ref, g_ref, o_ref):
    tq = q_ref.shape[0]
    k = k_ref[...]
    v = v_ref[...]
    lo = lax.broadcasted_iota(jnp.int32, (tq, PAIR_W), 1) < HEAD_DIM
    nt = (((1,), (1,)), ((), ()))
    for i in range(Q_W // PAIR_W):
        qp = q_ref[:, i * PAIR_W:(i + 1) * PAIR_W]
        halves = []
        for grp in range(2):
            qz = jnp.where(lo if grp == 0 else jnp.logical_not(lo), qp, jnp.zeros_like(qp))
            s = lax.dot_general(qz, k, nt, preferred_element_type=F32)
            m = jnp.max(s, axis=-1, keepdims=True)
            p = jnp.exp(s - m)
            l = jnp.sum(p, axis=-1, keepdims=True)
            halves.append(jnp.dot(p.astype(BF16), v, preferred_element_type=F32) / l)
        o = jnp.where(lo, halves[0], halves[1])
        o = o * _half_rms(o, lo) * g_ref[:, i * PAIR_W:(i + 1) * PAIR_W]
        o_ref[:, i * PAIR_W:(i + 1) * PAIR_W] = o.astype(BF16)


def _attn_a_call(q, k, v, g, B, S, tq):
    T = q.shape[0]
    nq = S // tq
    return pl.pallas_call(
        _attn_a_kernel,
        grid=(B, nq),
        in_specs=[pl.BlockSpec((tq, Q_W), lambda b, i: (b * nq + i, 0)),
                  pl.BlockSpec((S, KV_W), lambda b, i: (b, 0)),
                  pl.BlockSpec((S, KV_W), lambda b, i: (b, 0)),
                  pl.BlockSpec((1, Q_W), lambda b, i: (0, 0))],
        out_specs=pl.BlockSpec((tq, Q_W), lambda b, i: (b * nq + i, 0)),
        out_shape=jax.ShapeDtypeStruct((T, Q_W), BF16),
        compiler_params=_params("parallel", "parallel"),
        name="attn_global",
    )(q, k, v, g)


def _attn_b_kernel(sink_ref, q_ref, k_ref, v_ref, g_ref, o_ref, *, slopes):
    n_sub = q_ref.shape[0] // BLOCK
    for u in range(n_sub):
        _attn_b_block(sink_ref, q_ref, k_ref, v_ref, g_ref, o_ref, slopes,
                      pl.program_id(1) * n_sub + u, slice(u * BLOCK, (u + 1) * BLOCK))


def _attn_b_block(sink_ref, q_ref, k_ref, v_ref, g_ref, o_ref, slopes, j, rows):
    nb = k_ref.shape[0] // BLOCK
    span = BLOCK + 2 * WINDOW
    ks, vs = [], []
    for c in (-1, 0, 1):
        jc = j + c
        start = pl.multiple_of(jnp.clip(jc, 0, nb - 1) * BLOCK, BLOCK)
        ks.append(k_ref[pl.ds(start, BLOCK), :])
        vs.append(v_ref[pl.ds(start, BLOCK), :])
    kslab = jnp.concatenate(ks, axis=0)
    vslab = jnp.concatenate(vs, axis=0)
    qi = lax.broadcasted_iota(jnp.int32, (BLOCK, span), 0)
    ki = lax.broadcasted_iota(jnp.int32, (BLOCK, span), 1)
    dist = jnp.abs(qi + WINDOW - ki)
    s_pos = (j - 1) * BLOCK + ki
    in_seq = jnp.logical_and(s_pos >= 0, s_pos < nb * BLOCK)
    valid = jnp.logical_and(dist <= WINDOW, in_seq)
    distf = dist.astype(F32)

    lo = lax.broadcasted_iota(jnp.int32, (BLOCK, PAIR_W), 1) < HEAD_DIM
    hi = jnp.logical_not(lo)
    n_pairs = Q_W // PAIR_W
    heads = [(i, grp) for grp in range(2) for i in range(n_pairs)]
    qs = []
    for i, grp in heads:
        qp = q_ref[rows, i * PAIR_W:(i + 1) * PAIR_W]
        qs.append(jnp.where(lo if grp == 0 else hi, qp, jnp.zeros_like(qp)))
    s_all = lax.dot_general(jnp.concatenate(qs, axis=0), kslab, (((1,), (1,)), ((), ())),
                            preferred_element_type=F32)
    ps, inv = [], []
    for n, (i, grp) in enumerate(heads):
        hd = i + n_pairs * grp
        s = s_all[n * BLOCK:(n + 1) * BLOCK] - slopes[hd] * distf
        s = jnp.where(valid, s, NEG_INF)
        sink = sink_ref[hd]
        m = jnp.maximum(jnp.max(s, axis=-1, keepdims=True), sink)
        p = jnp.exp(s - m)
        inv.append(1.0 / (jnp.sum(p, axis=-1, keepdims=True) + jnp.exp(sink - m)))
        ps.append(p.astype(BF16))
    o_all = jnp.dot(jnp.concatenate(ps, axis=0), vslab, preferred_element_type=F32)
    for i in range(n_pairs):
        o0 = o_all[i * BLOCK:(i + 1) * BLOCK] * inv[i]
        o1 = o_all[(i + n_pairs) * BLOCK:(i + n_pairs + 1) * BLOCK] * inv[i + n_pairs]
        o = jnp.where(lo, o0, o1)
        o = o * _half_rms(o, lo) * g_ref[:, i * PAIR_W:(i + 1) * PAIR_W]
        o_ref[rows, i * PAIR_W:(i + 1) * PAIR_W] = o.astype(BF16)


def _attn_b_call(sink, q, k, v, g, B, S, n_sub):
    T = q.shape[0]
    nb = S // (BLOCK * n_sub)
    tqb = BLOCK * n_sub
    kern = functools.partial(_attn_b_kernel, slopes=_alibi_slopes(N_B_HEADS))
    return pl.pallas_call(
        kern,
        grid=(B, nb),
        in_specs=[pl.BlockSpec(memory_space=pltpu.SMEM),
                  pl.BlockSpec((tqb, Q_W), lambda b, i: (b * nb + i, 0)),
                  pl.BlockSpec((S, KV_W), lambda b, i: (b, 0)),
                  pl.BlockSpec((S, KV_W), lambda b, i: (b, 0)),
                  pl.BlockSpec((1, Q_W), lambda b, i: (0, 0))],
        out_specs=pl.BlockSpec((tqb, Q_W), lambda b, i: (b * nb + i, 0)),
        out_shape=jax.ShapeDtypeStruct((T, Q_W), BF16),
        compiler_params=_params("parallel", "parallel"),
        name="attn_window",
    )(sink, q, k, v, g)


def _oproj_kernel(oa_ref, ob_ref, wa_ref, wb_ref, h_ref, g_ref, b_ref, o_ref, *, alpha):
    mix = jnp.dot(oa_ref[...], wa_ref[...], preferred_element_type=F32)
    mix = mix + jnp.dot(ob_ref[...], wb_ref[...], preferred_element_type=F32)
    o_ref[...] = _layer_norm(alpha * h_ref[...] + mix, g_ref[...], b_ref[...])


def _oproj_call(oa, ob, wa, wb, h, g, b, alpha, tm):
    T, D = h.shape
    row = lambda i: (i, 0)
    const = lambda i: (0, 0)
    return pl.pallas_call(
        functools.partial(_oproj_kernel, alpha=alpha),
        grid=(T // tm,),
        in_specs=[pl.BlockSpec((tm, Q_W), row), pl.BlockSpec((tm, Q_W), row),
                  pl.BlockSpec((Q_W, D), const), pl.BlockSpec((Q_W, D), const),
                  pl.BlockSpec((tm, D), row),
                  pl.BlockSpec((1, D), const), pl.BlockSpec((1, D), const)],
        out_specs=pl.BlockSpec((tm, D), row),
        out_shape=jax.ShapeDtypeStruct((T, D), F32),
        compiler_params=_params("parallel"),
        name="out_proj_ln",
    )(oa, ob, wa, wb, h, g, b)


def _peer_scores_kernel(h_ref, wq_ref, keys_ref, st_ref):
    q = jnp.dot(h_ref[...].astype(BF16), wq_ref[...], preferred_element_type=F32).astype(BF16)
    half = PEER_DKEY // 2
    for n in range(2 * PEER_HEADS):
        qs = q[:, n * half:(n + 1) * half]
        st_ref[n] = lax.dot_general(keys_ref[n], qs, (((1,), (1,)), ((), ())),
                                    preferred_element_type=F32)


def _peer_scores_call(h, wq, keys, tm):
    T, D = h.shape
    n_sets = 2 * PEER_HEADS
    half = PEER_DKEY // 2
    return pl.pallas_call(
        _peer_scores_kernel,
        grid=(T // tm,),
        in_specs=[pl.BlockSpec((tm, D), lambda i: (i, 0)),
                  pl.BlockSpec((D, PEER_HEADS * PEER_DKEY), lambda i: (0, 0)),
                  pl.BlockSpec((n_sets, PEER_NKEYS, half), lambda i: (0, 0, 0))],
        out_specs=pl.BlockSpec((n_sets, PEER_NKEYS, tm), lambda i: (0, 0, i)),
        out_shape=jax.ShapeDtypeStruct((n_sets, PEER_NKEYS, T), F32),
        compiler_params=_params("parallel"),
        name="peer_scores",
    )(h, wq, keys)


ROUTE_TOKENS = SUBLANES * LANES
ROUTE_PASS = 4
ROUTE_PAIRS = [(r1, r2) for r1 in range(PEER_TOPK) for r2 in range(PEER_TOPK)
               if (r1 + 1) * (r2 + 1) <= PEER_TOPK]


def _slab(i):
    if isinstance(i, int):
        return pl.ds(i * SUBLANES, SUBLANES)
    return pl.ds(pl.multiple_of(i * SUBLANES, SUBLANES), SUBLANES)


def _insert_sorted(tops, x):
    out = []
    for t in tops[:-1]:
        out.append(jnp.maximum(t, x))
        x = jnp.minimum(t, x)
    out.append(jnp.maximum(tops[-1], x))
    return out


def _router_kernel(st_ref, r1_ref, c1_ref, n2_ref, e2_ref,
                   s1_ref, s2_ref, a1_ref, a2_ref, rk1_ref, n2k_ref, z_ref, bad_ref,
                   cand_ref, n2r_ref, out_ref):
    K = PEER_TOPK
    neg = -jnp.inf
    shape = (SUBLANES, LANES)
    ones = jnp.ones(shape, F32)
    zeros = jnp.zeros(shape, F32)

    def load_keys_in(src_set, dst_ref):
        for g in range(SUBLANES):
            dst_ref[pl.ds(g, PEER_NKEYS, stride=SUBLANES), :] = st_ref[src_set, :, g * LANES:(g + 1) * LANES]

    def store_keys_out(dst_ref, h):
        for g in range(SUBLANES):
            dst_ref[h, :, g * LANES:(g + 1) * LANES] = out_ref[pl.ds(g, PEER_NKEYS, stride=SUBLANES), :]

    def top_values(value_of, n, a_ref):
        thr = None
        for p in range(K // ROUTE_PASS):
            groups = [[jnp.full(shape, neg, F32)] * ROUTE_PASS for _ in range(2)]
            for k in range(n):
                x = value_of(k)
                if thr is not None:
                    x = jnp.where(x < thr, x, neg)
                groups[k % 2] = _insert_sorted(groups[k % 2], x)
            tops = groups[0]
            for x in groups[1]:
                tops = _insert_sorted(tops, x)
            for i, t in enumerate(tops):
                a_ref[_slab(p * ROUTE_PASS + i), :] = t
            thr = tops[-1]

    def adjacent_equal(a_ref):
        bad = zeros
        prev = a_ref[_slab(0), :]
        for r in range(1, K):
            cur = a_ref[_slab(r), :]
            bad = jnp.where(cur == prev, ones, bad)
            prev = cur
        return bad

    def cand(r1, r2):
        return a1_ref[_slab(r1), :] + a2_ref[_slab(r2), :]

    def finish_candidates(selected):
        top = cand(0, 0)
        z = zeros
        counts = [zeros] * K
        for idx, (r1, r2) in enumerate(ROUTE_PAIRS):
            s = selected(idx)
            z = z + s * jnp.exp(cand(r1, r2) - top)
            counts[r2] = counts[r2] + s
        z_ref[...] = z
        for r2 in range(K):
            n2r_ref[_slab(r2), :] = counts[r2]

    def head(h, carry):
        load_keys_in(2 * h, s1_ref)
        load_keys_in(2 * h + 1, s2_ref)

        top_values(lambda k: s1_ref[_slab(k), :], PEER_NKEYS, a1_ref)
        top_values(lambda k: s2_ref[_slab(k), :], PEER_NKEYS, a2_ref)
        a1 = [a1_ref[_slab(r), :] for r in range(K)]
        n_ge1 = zeros
        for k in range(PEER_NKEYS):
            s = s1_ref[_slab(k), :]
            rank = jnp.full(shape, float(K), F32)
            for r in reversed(range(K)):
                rank = jnp.where(s >= a1[r], float(r), rank)
            rk1_ref[_slab(k), :] = rank
            n_ge1 = n_ge1 + jnp.where(rank < float(K), ones, zeros)
        top_values(lambda i: cand(*ROUTE_PAIRS[i]), len(ROUTE_PAIRS), cand_ref)
        tau = cand_ref[_slab(K - 1), :]
        finish_candidates(lambda i: jnp.where(cand(*ROUTE_PAIRS[i]) >= tau, ones, zeros))
        a2 = [a2_ref[_slab(r), :] for r in range(K)]
        counts = [n2r_ref[_slab(r), :] for r in range(K)]
        n_sel = zeros
        for r in range(K):
            n_sel = n_sel + counts[r]
        n_ge2 = zeros
        for k in range(PEER_NKEYS):
            s = s2_ref[_slab(k), :]
            v = zeros
            for r in reversed(range(K)):
                v = jnp.where(s >= a2[r], counts[r], v)
            n2k_ref[_slab(k), :] = v
            n_ge2 = n_ge2 + jnp.where(s >= a2[K - 1], ones, zeros)
        bad = adjacent_equal(a1_ref) + adjacent_equal(a2_ref) + adjacent_equal(cand_ref)
        bad = bad + jnp.where(n_ge1 == float(K), zeros, ones) + jnp.where(n_ge2 == float(K), zeros, ones)
        bad = bad + jnp.where(n_sel == float(K), zeros, ones)

        bad_ref[...] = bad

        top1 = a1_ref[_slab(0), :]
        top2 = a2_ref[_slab(0), :]
        inv_z = 1.0 / z_ref[...]
        for k in range(PEER_NKEYS):
            out_ref[_slab(k), :] = rk1_ref[_slab(k), :]
        store_keys_out(r1_ref, h)
        for k in range(PEER_NKEYS):
            out_ref[_slab(k), :] = jnp.exp(s1_ref[_slab(k), :] - top1) * inv_z
        store_keys_out(c1_ref, h)
        for k in range(PEER_NKEYS):
            out_ref[_slab(k), :] = n2k_ref[_slab(k), :]
        store_keys_out(n2_ref, h)
        for k in range(PEER_NKEYS):
            out_ref[_slab(k), :] = jnp.exp(s2_ref[_slab(k), :] - top2)
        store_keys_out(e2_ref, h)

        def fix_group(g, carry2):
            @pl.when(jnp.sum(bad_ref[pl.ds(g, 1), :]) > 0.0)
            def _():
                _route_group_exact(st_ref, r1_ref, c1_ref, n2_ref, e2_ref, h,
                                   pl.ds(pl.multiple_of(g * LANES, LANES), LANES))
            return carry2

        @pl.when(jnp.sum(bad) > 0.0)
        def _():
            lax.fori_loop(0, SUBLANES, fix_group, 0)

        return carry

    lax.fori_loop(0, PEER_HEADS, head, 0)


def _route_group_exact(st_ref, r1_ref, c1_ref, n2_ref, e2_ref, h, lanes):
    tt = LANES
    K = PEER_TOPK
    n_lead = 4
    neg = -jnp.inf
    kio_i = lax.broadcasted_iota(jnp.int32, (PEER_NKEYS, tt), 0)
    rio = lax.broadcasted_iota(jnp.int32, (K, tt), 0)
    blk = kio_i // K
    rr = kio_i % K
    pos = jnp.where(blk < n_lead, rr * K + blk, (blk - n_lead) * K + rr).astype(F32)
    dup = jnp.logical_and(blk >= n_lead, rr < n_lead)
    kio = kio_i.astype(F32)

    def top16(s):
        rank = jnp.full((PEER_NKEYS, tt), float(K), F32)
        arr = jnp.zeros((K, tt), F32)
        for r in range(K):
            m = jnp.max(s, axis=0, keepdims=True)
            first = jnp.min(jnp.where(s == m, kio, float(PEER_NKEYS)), axis=0, keepdims=True)
            hit = kio == first
            rank = jnp.where(hit, float(r), rank)
            s = jnp.where(hit, neg, s)
            arr = jnp.where(rio == r, m, arr)
        return rank, arr

    s1 = st_ref[2 * h, :, lanes]
    s2 = st_ref[2 * h + 1, :, lanes]
    rank1, a1 = top16(s1)
    rank2, a2 = top16(s2)
    blocks = ([a1 + a2[r2:r2 + 1] for r2 in range(n_lead)]
              + [a1[r1:r1 + 1] + a2 for r1 in range(n_lead)])
    c0 = jnp.where(dup, neg, jnp.concatenate(blocks, axis=0))
    c = c0
    sel = jnp.zeros((PEER_NKEYS, tt), F32)
    for _ in range(K):
        m = jnp.max(c, axis=0, keepdims=True)
        first = jnp.min(jnp.where(c == m, pos, float(K * K)), axis=0, keepdims=True)
        hit = pos == first
        sel = jnp.where(hit, 1.0, sel)
        c = jnp.where(hit, neg, c)
    top = a1[0:1] + a2[0:1]
    z = jnp.sum(jnp.where(sel > 0.0, jnp.exp(c0 - top), 0.0), axis=0, keepdims=True)
    tail = sel[n_lead * K:(n_lead + 1) * K]
    for r1 in range(1, n_lead):
        tail = tail + sel[(n_lead + r1) * K:(n_lead + r1 + 1) * K]
    n2 = jnp.zeros((PEER_NKEYS, tt), F32)
    for r2 in range(K):
        if r2 < n_lead:
            cnt = jnp.sum(sel[r2 * K:(r2 + 1) * K], axis=0, keepdims=True)
        else:
            cnt = tail[r2:r2 + 1]
        n2 = jnp.where(rank2 == float(r2), cnt, n2)
    r1_ref[h, :, lanes] = rank1
    c1_ref[h, :, lanes] = jnp.exp(s1 - a1[0:1]) / z
    n2_ref[h, :, lanes] = n2
    e2_ref[h, :, lanes] = jnp.exp(s2 - a2[0:1])


def _router_call(st):
    T = st.shape[-1]
    tt = ROUTE_TOKENS
    shape = (PEER_HEADS, PEER_NKEYS, T)
    outs = [jax.ShapeDtypeStruct(shape, F32)] * 4
    spec = pl.BlockSpec((PEER_HEADS, PEER_NKEYS, tt), lambda i: (0, 0, i))
    keys = pltpu.VMEM((PEER_NKEYS * SUBLANES, LANES), F32)
    tops = pltpu.VMEM((PEER_TOPK * SUBLANES, LANES), F32)
    one = pltpu.VMEM((SUBLANES, LANES), F32)
    return pl.pallas_call(
        _router_kernel,
        grid=(T // tt,),
        in_specs=[pl.BlockSpec((2 * PEER_HEADS, PEER_NKEYS, tt), lambda i: (0, 0, i))],
        out_specs=[spec] * 4,
        out_shape=outs,
        scratch_shapes=[keys, keys, tops, tops, keys, keys, one, one, tops, tops, keys],
        compiler_params=_params("parallel"),
        name="peer_router",
    )(st)


PEER_CHUNK_ROWS = 256
PEER_IB_GROUP = 2
PEER_PART_ROWS = 64


def _peer_dense_kernel(h_ref, r1_ref, c1_ref, n2_ref, e2_ref, u_first_ref, u_a_ref, u_b_ref,
                       vt_a_ref, vt_b_ref, vt_last_ref, g_ref, b_ref, o_ref,
                       xt_ref, acc_ref, n2s_ref, e2s_ref, act_a_ref, act_b_ref, gt_a_ref, gt_b_ref,
                       *, alpha):
    j = pl.program_id(1)
    tt = h_ref.shape[0]
    te = vt_a_ref.shape[1]
    n_sub = te // PEER_NKEYS
    tile = (PEER_PART_ROWS, LANES)
    sqrt_half = float(np.sqrt(0.5))

    def gelu_bf16(a):
        t = lax.erf(a * sqrt_half).astype(BF16)
        return a.astype(BF16) * (0.5 + 0.5 * t)

    @pl.when(j == 0)
    def _():
        xt_ref[:, :tt] = h_ref[...].T.astype(BF16)
        acc_ref[:, :tt] = jnp.zeros((acc_ref.shape[0], tt), F32)
        n2s_ref[:, :, :tt] = n2_ref[...].astype(BF16)
        e2s_ref[:, :, :tt] = e2_ref[...].astype(BF16)
        gt_b_ref[:, :tt] = jnp.zeros((te, tt), BF16)
        act_a_ref[:, :tt] = gelu_bf16(jnp.dot(_unpack_bf16(u_first_ref[...]), xt_ref[:, :tt],
                                              preferred_element_type=F32))

    def phase(u_next_ref, act_next_ref, act_cur_ref, gt_cur_ref, vt_prev_ref, gt_prev_ref, key_off):
        def chunk(c, carry):
            rows = pl.ds(pl.multiple_of(c * PEER_CHUNK_ROWS, PEER_CHUNK_ROWS), PEER_CHUNK_ROWS)
            prows = pl.ds(pl.multiple_of(c * (PEER_CHUNK_ROWS // 2), PEER_CHUNK_ROWS // 2),
                          PEER_CHUNK_ROWS // 2)
            n_ib = PEER_CHUNK_ROWS // PEER_NKEYS
            for grp in range(n_ib // PEER_IB_GROUP):
                ibs = [c * n_ib + grp * PEER_IB_GROUP + s for s in range(PEER_IB_GROUP)]
                r1_rows = [[r1_ref[hd, pl.ds(key_off + ib, 1), :] for hd in range(PEER_HEADS)]
                           for ib in ibs]
                c1_rows = [[c1_ref[hd, pl.ds(key_off + ib, 1), :] for hd in range(PEER_HEADS)]
                           for ib in ibs]
                for tg in range(tt // LANES):
                    lanes = slice(tg * LANES, (tg + 1) * LANES)
                    n_part = PEER_NKEYS // PEER_PART_ROWS
                    ws = [[None] * PEER_IB_GROUP for _ in range(n_part)]
                    for hd in range(PEER_HEADS):
                        r1 = [jnp.broadcast_to(r1_rows[s][hd][:, lanes], tile).astype(BF16)
                              for s in range(PEER_IB_GROUP)]
                        c1 = [jnp.broadcast_to(c1_rows[s][hd][:, lanes], tile).astype(BF16)
                              for s in range(PEER_IB_GROUP)]
                        for part in range(n_part):
                            krows = slice(part * PEER_PART_ROWS, (part + 1) * PEER_PART_ROWS)
                            n2t = n2s_ref[hd, krows, lanes]
                            e2t = e2s_ref[hd, krows, lanes]
                            for s in range(PEER_IB_GROUP):
                                term = jnp.where(r1[s] < n2t, e2t, jnp.zeros(tile, BF16)) * c1[s]
                                ws[part][s] = term if ws[part][s] is None else ws[part][s] + term
                    for part in range(n_part):
                        for s in range(PEER_IB_GROUP):
                            erows = pl.ds(pl.multiple_of(ibs[s] * PEER_NKEYS + part * PEER_PART_ROWS,
                                                         PEER_PART_ROWS), PEER_PART_ROWS)
                            gt_cur_ref[erows, lanes] = act_cur_ref[erows, lanes] * ws[part][s]
            act_next_ref[rows, :tt] = gelu_bf16(jnp.dot(_unpack_bf16(u_next_ref[prows, :]),
                                                        xt_ref[:, :tt], preferred_element_type=F32))
            acc_ref[rows, :tt] += jnp.dot(_unpack_bf16(vt_prev_ref[prows, :]), gt_prev_ref[:, :tt],
                                        preferred_element_type=F32)
            return carry

        lax.fori_loop(0, te // PEER_CHUNK_ROWS, chunk, 0)

    phase(u_a_ref, act_b_ref, act_a_ref, gt_a_ref, vt_a_ref, gt_b_ref, 0)
    phase(u_b_ref, act_a_ref, act_b_ref, gt_b_ref, vt_b_ref, gt_a_ref, n_sub)

    @pl.when(j == pl.num_programs(1) - 1)
    def _():
        acc = acc_ref[:, :tt] + jnp.dot(_unpack_bf16(vt_last_ref[...]), gt_b_ref[:, :tt],
                                     preferred_element_type=F32)
        z = alpha * h_ref[...] + acc.T
        o_ref[...] = _layer_norm(z, g_ref[...], b_ref[...])


def _peer_dense_call(h, r1, c1, n2, e2, u, vt, g, b, alpha, tt, te):
    T, D = h.shape
    n_sub = te // PEER_NKEYS
    n_blk = PEER_EXPERTS // te
    n_steps = n_blk // 2
    assert D == te and n_blk % 2 == 0
    spec1 = pl.BlockSpec((PEER_HEADS, 2 * n_sub, tt), lambda i, j: (0, j, i))
    spec2 = pl.BlockSpec((PEER_HEADS, PEER_NKEYS, tt), lambda i, j: (0, 0, i))
    last = n_blk - 1
    ttp = tt + LANES
    u_spec = lambda f: pl.BlockSpec((te // 2, D), lambda i, j: (f(j), 0))
    vt_spec = lambda f: pl.BlockSpec((D // 2, te), lambda i, j: (0, f(j)))
    return pl.pallas_call(
        functools.partial(_peer_dense_kernel, alpha=alpha),
        grid=(T // tt, n_steps),
        in_specs=[pl.BlockSpec((tt, D), lambda i, j: (i, 0)),
                  spec1, spec1, spec2, spec2,
                  u_spec(lambda j: 0),
                  u_spec(lambda j: 2 * j + 1),
                  u_spec(lambda j: jnp.minimum(2 * j + 2, last)),
                  vt_spec(lambda j: jnp.maximum(2 * j - 1, 0)),
                  vt_spec(lambda j: 2 * j),
                  vt_spec(lambda j: last),
                  pl.BlockSpec((1, D), lambda i, j: (0, 0)),
                  pl.BlockSpec((1, D), lambda i, j: (0, 0))],
        out_specs=pl.BlockSpec((tt, D), lambda i, j: (i, 0)),
        out_shape=jax.ShapeDtypeStruct((T, D), F32),
        scratch_shapes=[pltpu.VMEM((D, ttp), BF16), pltpu.VMEM((D, ttp), F32),
                        pltpu.VMEM((PEER_HEADS, PEER_NKEYS, ttp), BF16),
                        pltpu.VMEM((PEER_HEADS, PEER_NKEYS, ttp), BF16),
                        pltpu.VMEM((te, ttp), BF16), pltpu.VMEM((te, ttp), BF16),
                        pltpu.VMEM((te, ttp), BF16), pltpu.VMEM((te, ttp), BF16)],
        compiler_params=_params("parallel", "arbitrary"),
        name="peer_dense",
    )(h, r1, c1, n2, e2, u, u, u, vt, vt, vt, g, b)


def _pair_perm():
    idx = []
    for i in range(N_A_HEADS // 2):
        for grp in range(2):
            hd = i + (N_A_HEADS // 2) * grp
            idx.extend(range(hd * HEAD_DIM, (hd + 1) * HEAD_DIM))
    return np.asarray(idx, dtype=np.int32)


def _rope_tables(S):
    pos = np.arange(S)
    row = (pos // GRID_W).astype(np.float32)
    col = (pos % GRID_W).astype(np.float32)
    inv_freq = jnp.asarray(ROPE_THETA, F32) ** (-jnp.arange(ROPE_FREQS, dtype=F32) / ROPE_FREQS)
    ang_r = jnp.asarray(row)[:, None] * inv_freq
    ang_c = jnp.asarray(col)[:, None] * inv_freq
    cr, sr, cc, sc = jnp.cos(ang_r), jnp.sin(ang_r), jnp.cos(ang_c), jnp.sin(ang_c)
    zero = jnp.zeros_like(sr)
    cos = jnp.concatenate([cr, cr, cc, cc], axis=-1)
    sa = jnp.concatenate([-sr, zero, -sc, zero], axis=-1)
    sb = jnp.concatenate([zero, sr, zero, sc], axis=-1)
    tile2 = lambda t: jnp.concatenate([t, t], axis=-1)
    return tile2(cos), tile2(sa), tile2(sb)


def kernel(x, ln_in_g, ln_in_b, w_in, qn_g, kn_g, sink, gn_a_g, gn_b_g, w_o, ln1_g, ln1_b,
           peer_wq, peer_keys, peer_u, peer_v, ln2_g, ln2_b):
    B, S, D = x.shape
    depth = w_in.shape[0]
    T = B * S
    alpha = float((2.0 * depth) ** 0.25)
    tm = min(1024, S)
    tq = min(512, S)
    n_win = min(4, S // BLOCK)
    tt_dense = min(512, T)
    te = 1024

    perm = _pair_perm()
    col_perm = np.concatenate([perm, Q_W + np.arange(2 * KV_W, dtype=np.int32),
                               Q_W + 2 * KV_W + perm,
                               2 * Q_W + 2 * KV_W + np.arange(2 * KV_W, dtype=np.int32)])
    cos, sa, sb = _rope_tables(S)
    row2 = lambda v: v.reshape(1, -1)
    tile2 = lambda v: jnp.concatenate([v, v]).reshape(1, -1)

    h = _ln_call(x.reshape(T, D), row2(ln_in_g), row2(ln_in_b), tm)
    for l in range(depth):
        w = w_in[l][:, col_perm].astype(BF16)
        qa, ka, va, qb, kb, vb = _proj_call(h, w, tile2(qn_g[l]), tile2(kn_g[l]), cos, sa, sb, S, tm)
        oa = _attn_a_call(qa, ka, va, row2(gn_a_g[l][perm]), B, S, tq)
        ob = _attn_b_call(sink[l], qb, kb, vb, row2(gn_b_g[l][perm]), B, S, n_win)
        wa = w_o[l][:Q_W][perm].astype(BF16)
        wb = w_o[l][Q_W:][perm].astype(BF16)
        h = _oproj_call(oa, ob, wa, wb, h, row2(ln1_g[l]), row2(ln1_b[l]), alpha, tm)
        keys = peer_keys[l].reshape(2 * PEER_HEADS, PEER_NKEYS, PEER_DKEY // 2).astype(BF16)
        st = _peer_scores_call(h, peer_wq[l].astype(BF16), keys, tm)
        r1, c1, n2, e2 = _router_call(st)
        h = _peer_dense_call(h, r1, c1, n2, e2, _pack_bf16_rows(peer_u, l),
                             _pack_bf16_rows(peer_v, l, transpose=True),
                             row2(ln2_g[l]), row2(ln2_b[l]), alpha, tt_dense, te)
    return h.reshape(B, S, D)
```

```python
import functools

import numpy as np
import jax
import jax.numpy as jnp
from jax import lax
from jax.experimental import pallas as pl
from jax.experimental.pallas import tpu as pltpu

D_MODEL = 1024
HEAD_DIM = 64
N_A_HEADS = 8
N_B_HEADS = 8
LANES = 128
SUBLANES = 8
PAIR_W = 2 * HEAD_DIM
Q_W = N_A_HEADS * HEAD_DIM
KV_W = 2 * HEAD_DIM
IN_WIDTH = 2 * (Q_W + 2 * KV_W)
BLOCK = 128
WINDOW = 128
GRID_W = 64
ROPE_THETA = 10000.0
ROPE_FREQS = HEAD_DIM // 4

PEER_HEADS = 8
PEER_NKEYS = 128
PEER_EXPERTS = PEER_NKEYS * PEER_NKEYS
PEER_DKEY = 256
PEER_TOPK = 16

LN_EPS = 1e-5
RMS_EPS = 1e-6
NEG_INF = -1e30

VMEM_LIMIT_BYTES = 56 * 1024 * 1024

BF16 = jnp.bfloat16
F32 = jnp.float32


def _alibi_slopes(n):
    return [float(2.0 ** (-((i + 1) * 8.0 / n))) for i in range(n)]


def _params(*sem):
    return pltpu.CompilerParams(dimension_semantics=sem,
                                vmem_limit_bytes=VMEM_LIMIT_BYTES)


def _pack_kernel(w_ref, o_ref, *, transpose):
    w = w_ref[...]
    if transpose:
        w = w.T
    o_ref[...] = pltpu.bitcast(w.astype(BF16), jnp.uint32)


def _pack_bf16_rows(stacked, layer, transpose=False):
    _, n, c = stacked.shape
    blk = 1024
    assert n % blk == 0 and c % blk == 0
    if transpose:
        grid, out_map = (n // blk, c // blk), (lambda i, j: (j, i))
        out_shape = (c // 2, n)
    else:
        grid, out_map = (n // blk, c // blk), (lambda i, j: (i, j))
        out_shape = (n // 2, c)
    return pl.pallas_call(
        functools.partial(_pack_kernel, transpose=transpose),
        grid=grid,
        in_specs=[pl.BlockSpec((None, blk, blk), lambda i, j: (layer, i, j))],
        out_specs=pl.BlockSpec((blk // 2, blk), out_map),
        out_shape=jax.ShapeDtypeStruct(out_shape, jnp.uint32),
        compiler_params=_params("parallel", "parallel"),
        name="pack_bf16",
    )(stacked)


def _unpack_bf16(words):
    return pltpu.bitcast(words, BF16)


def _layer_norm(z, g, b):
    mu = jnp.mean(z, axis=-1, keepdims=True)
    zc = z - mu
    var = jnp.mean(zc * zc, axis=-1, keepdims=True)
    return zc * lax.rsqrt(var + LN_EPS) * g + b


def _half_rms(t, lo):
    sq = t * t
    s_lo = jnp.sum(jnp.where(lo, sq, 0.0), axis=-1, keepdims=True)
    s_hi = jnp.sum(jnp.where(lo, 0.0, sq), axis=-1, keepdims=True)
    return jnp.where(lo, lax.rsqrt(s_lo / HEAD_DIM + RMS_EPS),
                     lax.rsqrt(s_hi / HEAD_DIM + RMS_EPS))


def _ln_kernel(x_ref, g_ref, b_ref, o_ref):
    o_ref[...] = _layer_norm(x_ref[...], g_ref[...], b_ref[...])


def _ln_call(x, g, b, tm):
    T, D = x.shape
    return pl.pallas_call(
        _ln_kernel,
        grid=(T // tm,),
        in_specs=[pl.BlockSpec((tm, D), lambda i: (i, 0)),
                  pl.BlockSpec((1, D), lambda i: (0, 0)),
                  pl.BlockSpec((1, D), lambda i: (0, 0))],
        out_specs=pl.BlockSpec((tm, D), lambda i: (i, 0)),
        out_shape=jax.ShapeDtypeStruct((T, D), F32),
        compiler_params=_params("parallel"),
        name="ln_in",
    )(x, g, b)


def _proj_kernel(h_ref, w_ref, qg_ref, kg_ref, cos_ref, sa_ref, sb_ref,
                 qa_ref, ka_ref, va_ref, qb_ref, kb_ref, vb_ref):
    tm = h_ref.shape[0]
    y = jnp.dot(h_ref[...].astype(BF16), w_ref[...], preferred_element_type=F32)
    lo = lax.broadcasted_iota(jnp.int32, (tm, PAIR_W), 1) < HEAD_DIM
    cos, sa, sb = cos_ref[...], sa_ref[...], sb_ref[...]
    scale = HEAD_DIM ** -0.5

    def norm_rope(t, g):
        xn = t * _half_rms(t, lo) * g
        return xn * cos + pltpu.roll(xn, PAIR_W - ROPE_FREQS, 1) * sa + pltpu.roll(xn, ROPE_FREQS, 1) * sb

    for i in range(Q_W // PAIR_W):
        t = y[:, i * PAIR_W:(i + 1) * PAIR_W]
        qa_ref[:, i * PAIR_W:(i + 1) * PAIR_W] = (norm_rope(t, qg_ref[...]) * scale).astype(BF16)
    off = Q_W
    ka_ref[...] = norm_rope(y[:, off:off + KV_W], kg_ref[...]).astype(BF16)
    off += KV_W
    va_ref[...] = y[:, off:off + KV_W].astype(BF16)
    off += KV_W
    qb_ref[...] = (y[:, off:off + Q_W] * scale).astype(BF16)
    off += Q_W
    kb_ref[...] = y[:, off:off + KV_W].astype(BF16)
    off += KV_W
    vb_ref[...] = y[:, off:off + KV_W].astype(BF16)


def _proj_call(h, w, qg, kg, cos, sa, sb, S, tm):
    T, D = h.shape
    n_pos = S // tm
    row = lambda i: (i, 0)
    const = lambda i: (0, 0)
    pos = lambda i: (i % n_pos, 0)
    outs = [jax.ShapeDtypeStruct((T, w_), BF16) for w_ in (Q_W, KV_W, KV_W, Q_W, KV_W, KV_W)]
    return pl.pallas_call(
        _proj_kernel,
        grid=(T // tm,),
        in_specs=[pl.BlockSpec((tm, D), row),
                  pl.BlockSpec((D, IN_WIDTH), const),
                  pl.BlockSpec((1, PAIR_W), const),
                  pl.BlockSpec((1, PAIR_W), const),
                  pl.BlockSpec((tm, PAIR_W), pos),
                  pl.BlockSpec((tm, PAIR_W), pos),
                  pl.BlockSpec((tm, PAIR_W), pos)],
        out_specs=[pl.BlockSpec((tm, o.shape[1]), row) for o in outs],
        out_shape=outs,
        compiler_params=_params("parallel"),
        name="in_proj",
    )(h, w, qg, kg, cos, sa, sb)


def _attn_a_kernel(q_ref, k_ref, v_ref, g_ref, o_ref):
    tq = q_ref.shape[0]
    k = k_ref[...]
    v = v_ref[...]
    lo = lax.broadcasted_iota(jnp.int32, (tq, PAIR_W), 1) < HEAD_DIM
    nt = (((1,), (1,)), ((), ()))
    for i in range(Q_W // PAIR_W):
        qp = q_ref[:, i * PAIR_W:(i + 1) * PAIR_W]
        halves = []
        for grp in range(2):
            qz = jnp.where(lo if grp == 0 else jnp.logical_not(lo), qp, jnp.zeros_like(qp))
            s = lax.dot_general(qz, k, nt, preferred_element_type=F32)
            m = jnp.max(s, axis=-1, keepdims=True)
            p = jnp.exp(s - m)
            l = jnp.sum(p, axis=-1, keepdims=True)
            halves.append(jnp.dot(p.astype(BF16), v, preferred_element_type=F32) / l)
        o = jnp.where(lo, halves[0], halves[1])
        o = o * _half_rms(o, lo) * g_ref[:, i * PAIR_W:(i + 1) * PAIR_W]
        o_ref[:, i * PAIR_W:(i + 1) * PAIR_W] = o.astype(BF16)


def _attn_a_call(q, k, v, g, B, S, tq):
    T = q.shape[0]
    nq = S // tq
    return pl.pallas_call(
        _attn_a_kernel,
        grid=(B, nq),
        in_specs=[pl.BlockSpec((tq, Q_W), lambda b, i: (b * nq + i, 0)),
                  pl.BlockSpec((S, KV_W), lambda b, i: (b, 0)),
                  pl.BlockSpec((S, KV_W), lambda b, i: (b, 0)),
                  pl.BlockSpec((1, Q_W), lambda b, i: (0, 0))],
        out_specs=pl.BlockSpec((tq, Q_W), lambda b, i: (b * nq + i, 0)),
        out_shape=jax.ShapeDtypeStruct((T, Q_W), BF16),
        compiler_params=_params("parallel", "parallel"),
        name="attn_global",
    )(q, k, v, g)


def _attn_b_kernel(sink_ref, q_ref, k_ref, v_ref, g_ref, o_ref, *, slopes):
    n_sub = q_ref.shape[0] // BLOCK
    for u in range(n_sub):
        _attn_b_block(sink_ref, q_ref, k_ref, v_ref, g_ref, o_ref, slopes,
                      pl.program_id(1) * n_sub + u, slice(u * BLOCK, (u + 1) * BLOCK))


def _attn_b_block(sink_ref, q_ref, k_ref, v_ref, g_ref, o_ref, slopes, j, rows):
    nb = k_ref.shape[0] // BLOCK
    span = BLOCK + 2 * WINDOW
    ks, vs = [], []
    for c in (-1, 0, 1):
        jc = j + c
        start = pl.multiple_of(jnp.clip(jc, 0, nb - 1) * BLOCK, BLOCK)
        ks.append(k_ref[pl.ds(start, BLOCK), :])
        vs.append(v_ref[pl.ds(start, BLOCK), :])
    kslab = jnp.concatenate(ks, axis=0)
    vslab = jnp.concatenate(vs, axis=0)
    qi = lax.broadcasted_iota(jnp.int32, (BLOCK, span), 0)
    ki = lax.broadcasted_iota(jnp.int32, (BLOCK, span), 1)
    dist = jnp.abs(qi + WINDOW - ki)
    s_pos = (j - 1) * BLOCK + ki
    in_seq = jnp.logical_and(s_pos >= 0, s_pos < nb * BLOCK)
    valid = jnp.logical_and(dist <= WINDOW, in_seq)
    distf = dist.astype(F32)

    lo = lax.broadcasted_iota(jnp.int32, (BLOCK, PAIR_W), 1) < HEAD_DIM
    hi = jnp.logical_not(lo)
    n_pairs = Q_W // PAIR_W
    heads = [(i, grp) for grp in range(2) for i in range(n_pairs)]
    qs = []
    for i, grp in heads:
        qp = q_ref[rows, i * PAIR_W:(i + 1) * PAIR_W]
        qs.append(jnp.where(lo if grp == 0 else hi, qp, jnp.zeros_like(qp)))
    s_all = lax.dot_general(jnp.concatenate(qs, axis=0), kslab, (((1,), (1,)), ((), ())),
                            preferred_element_type=F32)
    ps, inv = [], []
    for n, (i, grp) in enumerate(heads):
        hd = i + n_pairs * grp
        s = s_all[n * BLOCK:(n + 1) * BLOCK] - slopes[hd] * distf
        s = jnp.where(valid, s, NEG_INF)
        sink = sink_ref[hd]
        m = jnp.maximum(jnp.max(s, axis=-1, keepdims=True), sink)
        p = jnp.exp(s - m)
        inv.append(1.0 / (jnp.sum(p, axis=-1, keepdims=True) + jnp.exp(sink - m)))
        ps.append(p.astype(BF16))
    o_all = jnp.dot(jnp.concatenate(ps, axis=0), vslab, preferred_element_type=F32)
    for i in range(n_pairs):
        o0 = o_all[i * BLOCK:(i + 1) * BLOCK] * inv[i]
        o1 = o_all[(i + n_pairs) * BLOCK:(i + n_pairs + 1) * BLOCK] * inv[i + n_pairs]
        o = jnp.where(lo, o0, o1)
        o = o * _half_rms(o, lo) * g_ref[:, i * PAIR_W:(i + 1) * PAIR_W]
        o_ref[rows, i * PAIR_W:(i + 1) * PAIR_W] = o.astype(BF16)


def _attn_b_call(sink, q, k, v, g, B, S, n_sub):
    T = q.shape[0]
    nb = S // (BLOCK * n_sub)
    tqb = BLOCK * n_sub
    kern = functools.partial(_attn_b_kernel, slopes=_alibi_slopes(N_B_HEADS))
    return pl.pallas_call(
        kern,
        grid=(B, nb),
        in_specs=[pl.BlockSpec(memory_space=pltpu.SMEM),
                  pl.BlockSpec((tqb, Q_W), lambda b, i: (b * nb + i, 0)),
                  pl.BlockSpec((S, KV_W), lambda b, i: (b, 0)),
                  pl.BlockSpec((S, KV_W), lambda b, i: (b, 0)),
                  pl.BlockSpec((1, Q_W), lambda b, i: (0, 0))],
        out_specs=pl.BlockSpec((tqb, Q_W), lambda b, i: (b * nb + i, 0)),
        out_shape=jax.ShapeDtypeStruct((T, Q_W), BF16),
        compiler_params=_params("parallel", "parallel"),
        name="attn_window",
    )(sink, q, k, v, g)


def _oproj_kernel(oa_ref, ob_ref, wa_ref, wb_ref, h_ref, g_ref, b_ref, o_ref, *, alpha):
    mix = jnp.dot(oa_ref[...], wa_ref[...], preferred_element_type=F32)
    mix = mix + jnp.dot(ob_ref[...], wb_ref[...], preferred_element_type=F32)
    o_ref[...] = _layer_norm(alpha * h_ref[...] + mix, g_ref[...], b_ref[...])


def _oproj_call(oa, ob, wa, wb, h, g, b, alpha, tm):
    T, D = h.shape
    row = lambda i: (i, 0)
    const = lambda i: (0, 0)
    return pl.pallas_call(
        functools.partial(_oproj_kernel, alpha=alpha),
        grid=(T // tm,),
        in_specs=[pl.BlockSpec((tm, Q_W), row), pl.BlockSpec((tm, Q_W), row),
                  pl.BlockSpec((Q_W, D), const), pl.BlockSpec((Q_W, D), const),
                  pl.BlockSpec((tm, D), row),
                  pl.BlockSpec((1, D), const), pl.BlockSpec((1, D), const)],
        out_specs=pl.BlockSpec((tm, D), row),
        out_shape=jax.ShapeDtypeStruct((T, D), F32),
        compiler_params=_params("parallel"),
        name="out_proj_ln",
    )(oa, ob, wa, wb, h, g, b)


def _peer_scores_kernel(h_ref, wq_ref, keys_ref, st_ref):
    q = jnp.dot(h_ref[...].astype(BF16), wq_ref[...], preferred_element_type=F32).astype(BF16)
    half = PEER_DKEY // 2
    for n in range(2 * PEER_HEADS):
        qs = q[:, n * half:(n + 1) * half]
        st_ref[n] = lax.dot_general(keys_ref[n], qs, (((1,), (1,)), ((), ())),
                                    preferred_element_type=F32)


def _peer_scores_call(h, wq, keys, tm):
    T, D = h.shape
    n_sets = 2 * PEER_HEADS
    half = PEER_DKEY // 2
    return pl.pallas_call(
        _peer_scores_kernel,
        grid=(T // tm,),
        in_specs=[pl.BlockSpec((tm, D), lambda i: (i, 0)),
                  pl.BlockSpec((D, PEER_HEADS * PEER_DKEY), lambda i: (0, 0)),
                  pl.BlockSpec((n_sets, PEER_NKEYS, half), lambda i: (0, 0, 0))],
        out_specs=pl.BlockSpec((n_sets, PEER_NKEYS, tm), lambda i: (0, 0, i)),
        out_shape=jax.ShapeDtypeStruct((n_sets, PEER_NKEYS, T), F32),
        compiler_params=_params("parallel"),
        name="peer_scores",
    )(h, wq, keys)


ROUTE_TOKENS = SUBLANES * LANES
ROUTE_PASS = 4
ROUTE_PAIRS = [(r1, r2) for r1 in range(PEER_TOPK) for r2 in range(PEER_TOPK)
               if (r1 + 1) * (r2 + 1) <= PEER_TOPK]


def _slab(i):
    if isinstance(i, int):
        return pl.ds(i * SUBLANES, SUBLANES)
    return pl.ds(pl.multiple_of(i * SUBLANES, SUBLANES), SUBLANES)


def _insert_sorted(tops, x):
    out = []
    for t in tops[:-1]:
        out.append(jnp.maximum(t, x))
        x = jnp.minimum(t, x)
    out.append(jnp.maximum(tops[-1], x))
    return out


def _router_kernel(st_ref, r1_ref, c1_ref, n2_ref, e2_ref,
                   s1_ref, s2_ref, a1_ref, a2_ref, rk1_ref, n2k_ref, z_ref, bad_ref,
                   cand_ref, n2r_ref, out_ref):
    K = PEER_TOPK
    neg = -jnp.inf
    shape = (SUBLANES, LANES)
    ones = jnp.ones(shape, F32)
    zeros = jnp.zeros(shape, F32)

    def load_keys_in(src_set, dst_ref):
        for g in range(SUBLANES):
            dst_ref[pl.ds(g, PEER_NKEYS, stride=SUBLANES), :] = st_ref[src_set, :, g * LANES:(g + 1) * LANES]

    def store_keys_out(dst_ref, h):
        for g in range(SUBLANES):
            dst_ref[h, :, g * LANES:(g + 1) * LANES] = out_ref[pl.ds(g, PEER_NKEYS, stride=SUBLANES), :]

    def top_values(value_of, n, a_ref):
        thr = None
        for p in range(K // ROUTE_PASS):
            groups = [[jnp.full(shape, neg, F32)] * ROUTE_PASS for _ in range(2)]
            for k in range(n):
                x = value_of(k)
                if thr is not None:
                    x = jnp.where(x < thr, x, neg)
                groups[k % 2] = _insert_sorted(groups[k % 2], x)
            tops = groups[0]
            for x in groups[1]:
                tops = _insert_sorted(tops, x)
            for i, t in enumerate(tops):
                a_ref[_slab(p * ROUTE_PASS + i), :] = t
            thr = tops[-1]

    def adjacent_equal(a_ref):
        bad = zeros
        prev = a_ref[_slab(0), :]
        for r in range(1, K):
            cur = a_ref[_slab(r), :]
            bad = jnp.where(cur == prev, ones, bad)
            prev = cur
        return bad

    def cand(r1, r2):
        return a1_ref[_slab(r1), :] + a2_ref[_slab(r2), :]

    def finish_candidates(selected):
        top = cand(0, 0)
        z = zeros
        counts = [zeros] * K
        for idx, (r1, r2) in enumerate(ROUTE_PAIRS):
            s = selected(idx)
            z = z + s * jnp.exp(cand(r1, r2) - top)
            counts[r2] = counts[r2] + s
        z_ref[...] = z
        for r2 in range(K):
            n2r_ref[_slab(r2), :] = counts[r2]

    def head(h, carry):
        load_keys_in(2 * h, s1_ref)
        load_keys_in(2 * h + 1, s2_ref)

        top_values(lambda k: s1_ref[_slab(k), :], PEER_NKEYS, a1_ref)
        top_values(lambda k: s2_ref[_slab(k), :], PEER_NKEYS, a2_ref)
        a1 = [a1_ref[_slab(r), :] for r in range(K)]
        n_ge1 = zeros
        for k in range(PEER_NKEYS):
            s = s1_ref[_slab(k), :]
            rank = jnp.full(shape, float(K), F32)
            for r in reversed(range(K)):
                rank = jnp.where(s >= a1[r], float(r), rank)
            rk1_ref[_slab(k), :] = rank
            n_ge1 = n_ge1 + jnp.where(rank < float(K), ones, zeros)
        top_values(lambda i: cand(*ROUTE_PAIRS[i]), len(ROUTE_PAIRS), cand_ref)
        tau = cand_ref[_slab(K - 1), :]
        finish_candidates(lambda i: jnp.where(cand(*ROUTE_PAIRS[i]) >= tau, ones, zeros))
        a2 = [a2_ref[_slab(r), :] for r in range(K)]
        counts = [n2r_ref[_slab(r), :] for r in range(K)]
        n_sel = zeros
        for r in range(K):
            n_sel = n_sel + counts[r]
        n_ge2 = zeros
        for k in range(PEER_NKEYS):
            s = s2_ref[_slab(k), :]
            v = zeros
            for r in reversed(range(K)):
                v = jnp.where(s >= a2[r], counts[r], v)
            n2k_ref[_slab(k), :] = v
            n_ge2 = n_ge2 + jnp.where(s >= a2[K - 1], ones, zeros)
        bad = adjacent_equal(a1_ref) + adjacent_equal(a2_ref) + adjacent_equal(cand_ref)
        bad = bad + jnp.where(n_ge1 == float(K), zeros, ones) + jnp.where(n_ge2 == float(K), zeros, ones)
        bad = bad + jnp.where(n_sel == float(K), zeros, ones)

        bad_ref[...] = bad

        top1 = a1_ref[_slab(0), :]
        top2 = a2_ref[_slab(0), :]
        inv_z = 1.0 / z_ref[...]
        for k in range(PEER_NKEYS):
            out_ref[_slab(k), :] = rk1_ref[_slab(k), :]
        store_keys_out(r1_ref, h)
        for k in range(PEER_NKEYS):
            out_ref[_slab(k), :] = jnp.exp(s1_ref[_slab(k), :] - top1) * inv_z
        store_keys_out(c1_ref, h)
        for k in range(PEER_NKEYS):
            out_ref[_slab(k), :] = n2k_ref[_slab(k), :]
        store_keys_out(n2_ref, h)
        for k in range(PEER_NKEYS):
            out_ref[_slab(k), :] = jnp.exp(s2_ref[_slab(k), :] - top2)
        store_keys_out(e2_ref, h)

        def fix_group(g, carry2):
            @pl.when(jnp.sum(bad_ref[pl.ds(g, 1), :]) > 0.0)
            def _():
                _route_group_exact(st_ref, r1_ref, c1_ref, n2_ref, e2_ref, h,
                                   pl.ds(pl.multiple_of(g * LANES, LANES), LANES))
            return carry2

        @pl.when(jnp.sum(bad) > 0.0)
        def _():
            lax.fori_loop(0, SUBLANES, fix_group, 0)

        return carry

    lax.fori_loop(0, PEER_HEADS, head, 0)


def _route_group_exact(st_ref, r1_ref, c1_ref, n2_ref, e2_ref, h, lanes):
    tt = LANES
    K = PEER_TOPK
    n_lead = 4
    neg = -jnp.inf
    kio_i = lax.broadcasted_iota(jnp.int32, (PEER_NKEYS, tt), 0)
    rio = lax.broadcasted_iota(jnp.int32, (K, tt), 0)
    blk = kio_i // K
    rr = kio_i % K
    pos = jnp.where(blk < n_lead, rr * K + blk, (blk - n_lead) * K + rr).astype(F32)
    dup = jnp.logical_and(blk >= n_lead, rr < n_lead)
    kio = kio_i.astype(F32)

    def top16(s):
        rank = jnp.full((PEER_NKEYS, tt), float(K), F32)
        arr = jnp.zeros((K, tt), F32)
        for r in range(K):
            m = jnp.max(s, axis=0, keepdims=True)
            first = jnp.min(jnp.where(s == m, kio, float(PEER_NKEYS)), axis=0, keepdims=True)
            hit = kio == first
            rank = jnp.where(hit, float(r), rank)
            s = jnp.where(hit, neg, s)
            arr = jnp.where(rio == r, m, arr)
        return rank, arr

    s1 = st_ref[2 * h, :, lanes]
    s2 = st_ref[2 * h + 1, :, lanes]
    rank1, a1 = top16(s1)
    rank2, a2 = top16(s2)
    blocks = ([a1 + a2[r2:r2 + 1] for r2 in range(n_lead)]
              + [a1[r1:r1 + 1] + a2 for r1 in range(n_lead)])
    c0 = jnp.where(dup, neg, jnp.concatenate(blocks, axis=0))
    c = c0
    sel = jnp.zeros((PEER_NKEYS, tt), F32)
    for _ in range(K):
        m = jnp.max(c, axis=0, keepdims=True)
        first = jnp.min(jnp.where(c == m, pos, float(K * K)), axis=0, keepdims=True)
        hit = pos == first
        sel = jnp.where(hit, 1.0, sel)
        c = jnp.where(hit, neg, c)
    top = a1[0:1] + a2[0:1]
    z = jnp.sum(jnp.where(sel > 0.0, jnp.exp(c0 - top), 0.0), axis=0, keepdims=True)
    tail = sel[n_lead * K:(n_lead + 1) * K]
    for r1 in range(1, n_lead):
        tail = tail + sel[(n_lead + r1) * K:(n_lead + r1 + 1) * K]
    n2 = jnp.zeros((PEER_NKEYS, tt), F32)
    for r2 in range(K):
        if r2 < n_lead:
            cnt = jnp.sum(sel[r2 * K:(r2 + 1) * K], axis=0, keepdims=True)
        else:
            cnt = tail[r2:r2 + 1]
        n2 = jnp.where(rank2 == float(r2), cnt, n2)
    r1_ref[h, :, lanes] = rank1
    c1_ref[h, :, lanes] = jnp.exp(s1 - a1[0:1]) / z
    n2_ref[h, :, lanes] = n2
    e2_ref[h, :, lanes] = jnp.exp(s2 - a2[0:1])


def _router_call(st):
    T = st.shape[-1]
    tt = ROUTE_TOKENS
    shape = (PEER_HEADS, PEER_NKEYS, T)
    outs = [jax.ShapeDtypeStruct(shape, F32)] * 4
    spec = pl.BlockSpec((PEER_HEADS, PEER_NKEYS, tt), lambda i: (0, 0, i))
    keys = pltpu.VMEM((PEER_NKEYS * SUBLANES, LANES), F32)
    tops = pltpu.VMEM((PEER_TOPK * SUBLANES, LANES), F32)
    one = pltpu.VMEM((SUBLANES, LANES), F32)
    return pl.pallas_call(
        _router_kernel,
        grid=(T // tt,),
        in_specs=[pl.BlockSpec((2 * PEER_HEADS, PEER_NKEYS, tt), lambda i: (0, 0, i))],
        out_specs=[spec] * 4,
        out_shape=outs,
        scratch_shapes=[keys, keys, tops, tops, keys, keys, one, one, tops, tops, keys],
        compiler_params=_params("parallel"),
        name="peer_router",
    )(st)


PEER_CHUNK_ROWS = 256
PEER_IB_GROUP = 2
PEER_PART_ROWS = 64


def _peer_dense_kernel(h_ref, r1_ref, c1_ref, n2_ref, e2_ref, u_first_ref, u_a_ref, u_b_ref,
                       vt_a_ref, vt_b_ref, vt_last_ref, g_ref, b_ref, o_ref,
                       xt_ref, acc_ref, n2s_ref, e2s_ref, act_a_ref, act_b_ref, gt_a_ref, gt_b_ref,
                       *, alpha):
    j = pl.program_id(1)
    tt = h_ref.shape[0]
    te = vt_a_ref.shape[1]
    n_sub = te // PEER_NKEYS
    tile = (PEER_PART_ROWS, LANES)
    sqrt_half = float(np.sqrt(0.5))

    def gelu_bf16(a):
        return (a * (0.5 + 0.5 * lax.erf(a * sqrt_half))).astype(BF16)

    @pl.when(j == 0)
    def _():
        xt_ref[:, :tt] = h_ref[...].T.astype(BF16)
        acc_ref[:, :tt] = jnp.zeros((acc_ref.shape[0], tt), F32)
        n2s_ref[:, :, :tt] = n2_ref[...].astype(BF16)
        e2s_ref[:, :, :tt] = e2_ref[...].astype(BF16)
        gt_b_ref[:, :tt] = jnp.zeros((te, tt), BF16)
        act_a_ref[:, :tt] = gelu_bf16(jnp.dot(_unpack_bf16(u_first_ref[...]), xt_ref[:, :tt],
                                              preferred_element_type=F32))

    def phase(u_next_ref, act_next_ref, act_cur_ref, gt_cur_ref, vt_prev_ref, gt_prev_ref, key_off):
        def chunk(c, carry):
            rows = pl.ds(pl.multiple_of(c * PEER_CHUNK_ROWS, PEER_CHUNK_ROWS), PEER_CHUNK_ROWS)
            prows = pl.ds(pl.multiple_of(c * (PEER_CHUNK_ROWS // 2), PEER_CHUNK_ROWS // 2),
                          PEER_CHUNK_ROWS // 2)
            n_ib = PEER_CHUNK_ROWS // PEER_NKEYS
            for grp in range(n_ib // PEER_IB_GROUP):
                ibs = [c * n_ib + grp * PEER_IB_GROUP + s for s in range(PEER_IB_GROUP)]
                r1_rows = [[r1_ref[hd, pl.ds(key_off + ib, 1), :] for hd in range(PEER_HEADS)]
                           for ib in ibs]
                c1_rows = [[c1_ref[hd, pl.ds(key_off + ib, 1), :] for hd in range(PEER_HEADS)]
                           for ib in ibs]
                for tg in range(tt // LANES):
                    lanes = slice(tg * LANES, (tg + 1) * LANES)
                    n_part = PEER_NKEYS // PEER_PART_ROWS
                    ws = [[None] * PEER_IB_GROUP for _ in range(n_part)]
                    for hd in range(PEER_HEADS):
                        r1 = [jnp.broadcast_to(r1_rows[s][hd][:, lanes], tile).astype(BF16)
                              for s in range(PEER_IB_GROUP)]
                        c1 = [jnp.broadcast_to(c1_rows[s][hd][:, lanes], tile).astype(BF16)
                              for s in range(PEER_IB_GROUP)]
                        for part in range(n_part):
                            krows = slice(part * PEER_PART_ROWS, (part + 1) * PEER_PART_ROWS)
                            n2t = n2s_ref[hd, krows, lanes]
                            e2t = e2s_ref[hd, krows, lanes]
                            for s in range(PEER_IB_GROUP):
                                term = jnp.where(r1[s] < n2t, e2t, jnp.zeros(tile, BF16)) * c1[s]
                                ws[part][s] = term if ws[part][s] is None else ws[part][s] + term
                    for part in range(n_part):
                        for s in range(PEER_IB_GROUP):
                            erows = pl.ds(pl.multiple_of(ibs[s] * PEER_NKEYS + part * PEER_PART_ROWS,
                                                         PEER_PART_ROWS), PEER_PART_ROWS)
                            gt_cur_ref[erows, lanes] = act_cur_ref[erows, lanes] * ws[part][s]
            act_next_ref[rows, :tt] = gelu_bf16(jnp.dot(_unpack_bf16(u_next_ref[prows, :]),
                                                        xt_ref[:, :tt], preferred_element_type=F32))
            acc_ref[rows, :tt] += jnp.dot(_unpack_bf16(vt_prev_ref[prows, :]), gt_prev_ref[:, :tt],
                                        preferred_element_type=F32)
            return carry

        lax.fori_loop(0, te // PEER_CHUNK_ROWS, chunk, 0)

    phase(u_a_ref, act_b_ref, act_a_ref, gt_a_ref, vt_a_ref, gt_b_ref, 0)
    phase(u_b_ref, act_a_ref, act_b_ref, gt_b_ref, vt_b_ref, gt_a_ref, n_sub)

    @pl.when(j == pl.num_programs(1) - 1)
    def _():
        acc = acc_ref[:, :tt] + jnp.dot(_unpack_bf16(vt_last_ref[...]), gt_b_ref[:, :tt],
                                     preferred_element_type=F32)
        z = alpha * h_ref[...] + acc.T
        o_ref[...] = _layer_norm(z, g_ref[...], b_ref[...])


def _peer_dense_call(h, r1, c1, n2, e2, u, vt, g, b, alpha, tt, te):
    T, D = h.shape
    n_sub = te // PEER_NKEYS
    n_blk = PEER_EXPERTS // te
    n_steps = n_blk // 2
    assert D == te and n_blk % 2 == 0
    spec1 = pl.BlockSpec((PEER_HEADS, 2 * n_sub, tt), lambda i, j: (0, j, i))
    spec2 = pl.BlockSpec((PEER_HEADS, PEER_NKEYS, tt), lambda i, j: (0, 0, i))
    last = n_blk - 1
    ttp = tt + LANES
    u_spec = lambda f: pl.BlockSpec((te // 2, D), lambda i, j: (f(j), 0))
    vt_spec = lambda f: pl.BlockSpec((D // 2, te), lambda i, j: (0, f(j)))
    return pl.pallas_call(
        functools.partial(_peer_dense_kernel, alpha=alpha),
        grid=(T // tt, n_steps),
        in_specs=[pl.BlockSpec((tt, D), lambda i, j: (i, 0)),
                  spec1, spec1, spec2, spec2,
                  u_spec(lambda j: 0),
                  u_spec(lambda j: 2 * j + 1),
                  u_spec(lambda j: jnp.minimum(2 * j + 2, last)),
                  vt_spec(lambda j: jnp.maximum(2 * j - 1, 0)),
                  vt_spec(lambda j: 2 * j),
                  vt_spec(lambda j: last),
                  pl.BlockSpec((1, D), lambda i, j: (0, 0)),
                  pl.BlockSpec((1, D), lambda i, j: (0, 0))],
        out_specs=pl.BlockSpec((tt, D), lambda i, j: (i, 0)),
        out_shape=jax.ShapeDtypeStruct((T, D), F32),
        scratch_shapes=[pltpu.VMEM((D, ttp), BF16), pltpu.VMEM((D, ttp), F32),
                        pltpu.VMEM((PEER_HEADS, PEER_NKEYS, ttp), BF16),
                        pltpu.VMEM((PEER_HEADS, PEER_NKEYS, ttp), BF16),
                        pltpu.VMEM((te, ttp), BF16), pltpu.VMEM((te, ttp), BF16),
                        pltpu.VMEM((te, ttp), BF16), pltpu.VMEM((te, ttp), BF16)],
        compiler_params=_params("parallel", "arbitrary"),
        name="peer_dense",
    )(h, r1, c1, n2, e2, u, u, u, vt, vt, vt, g, b)


def _pair_perm():
    idx = []
    for i in range(N_A_HEADS // 2):
        for grp in range(2):
            hd = i + (N_A_HEADS // 2) * grp
            idx.extend(range(hd * HEAD_DIM, (hd + 1) * HEAD_DIM))
    return np.asarray(idx, dtype=np.int32)


def _rope_tables(S):
    pos = np.arange(S)
    row = (pos // GRID_W).astype(np.float32)
    col = (pos % GRID_W).astype(np.float32)
    inv_freq = jnp.asarray(ROPE_THETA, F32) ** (-jnp.arange(ROPE_FREQS, dtype=F32) / ROPE_FREQS)
    ang_r = jnp.asarray(row)[:, None] * inv_freq
    ang_c = jnp.asarray(col)[:, None] * inv_freq
    cr, sr, cc, sc = jnp.cos(ang_r), jnp.sin(ang_r), jnp.cos(ang_c), jnp.sin(ang_c)
    zero = jnp.zeros_like(sr)
    cos = jnp.concatenate([cr, cr, cc, cc], axis=-1)
    sa = jnp.concatenate([-sr, zero, -sc, zero], axis=-1)
    sb = jnp.concatenate([zero, sr, zero, sc], axis=-1)
    tile2 = lambda t: jnp.concatenate([t, t], axis=-1)
    return tile2(cos), tile2(sa), tile2(sb)


def kernel(x, ln_in_g, ln_in_b, w_in, qn_g, kn_g, sink, gn_a_g, gn_b_g, w_o, ln1_g, ln1_b,
           peer_wq, peer_keys, peer_u, peer_v, ln2_g, ln2_b):
    B, S, D = x.shape
    depth = w_in.shape[0]
    T = B * S
    alpha = float((2.0 * depth) ** 0.25)
    tm = min(1024, S)
    tq = min(512, S)
    n_win = min(4, S // BLOCK)
    tt_dense = min(512, T)
    te = 1024

    perm = _pair_perm()
    col_perm = np.concatenate([perm, Q_W + np.arange(2 * KV_W, dtype=np.int32),
                               Q_W + 2 * KV_W + perm,
                               2 * Q_W + 2 * KV_W + np.arange(2 * KV_W, dtype=np.int32)])
    cos, sa, sb = _rope_tables(S)
    row2 = lambda v: v.reshape(1, -1)
    tile2 = lambda v: jnp.concatenate([v, v]).reshape(1, -1)

    h = _ln_call(x.reshape(T, D), row2(ln_in_g), row2(ln_in_b), tm)
    for l in range(depth):
        w = w_in[l][:, col_perm].astype(BF16)
        qa, ka, va, qb, kb, vb = _proj_call(h, w, tile2(qn_g[l]), tile2(kn_g[l]), cos, sa, sb, S, tm)
        oa = _attn_a_call(qa, ka, va, row2(gn_a_g[l][perm]), B, S, tq)
        ob = _attn_b_call(sink[l], qb, kb, vb, row2(gn_b_g[l][perm]), B, S, n_win)
        wa = w_o[l][:Q_W][perm].astype(BF16)
        wb = w_o[l][Q_W:][perm].astype(BF16)
        h = _oproj_call(oa, ob, wa, wb, h, row2(ln1_g[l]), row2(ln1_b[l]), alpha, tm)
        keys = peer_keys[l].reshape(2 * PEER_HEADS, PEER_NKEYS, PEER_DKEY // 2).astype(BF16)
        st = _peer_scores_call(h, peer_wq[l].astype(BF16), keys, tm)
        r1, c1, n2, e2 = _router_call(st)
        h = _peer_dense_call(h, r1, c1, n2, e2, _pack_bf16_rows(peer_u, l),
                             _pack_bf16_rows(peer_v, l, transpose=True),
                             row2(ln2_g[l]), row2(ln2_b[l]), alpha, tt_dense, te)
    return h.reshape(B, S, D)
```

```python
import functools

import numpy as np
import jax
import jax.numpy as jnp
from jax import lax
from jax.experimental import pallas as pl
from jax.experimental.pallas import tpu as pltpu

D_MODEL = 1024
HEAD_DIM = 64
N_A_HEADS = 8
N_B_HEADS = 8
LANES = 128
SUBLANES = 8
PAIR_W = 2 * HEAD_DIM
Q_W = N_A_HEADS * HEAD_DIM
KV_W = 2 * HEAD_DIM
IN_WIDTH = 2 * (Q_W + 2 * KV_W)
BLOCK = 128
WINDOW = 128
GRID_W = 64
ROPE_THETA = 10000.0
ROPE_FREQS = HEAD_DIM // 4

PEER_HEADS = 8
PEER_NKEYS = 128
PEER_EXPERTS = PEER_NKEYS * PEER_NKEYS
PEER_DKEY = 256
PEER_TOPK = 16

LN_EPS = 1e-5
RMS_EPS = 1e-6
NEG_INF = -1e30

VMEM_LIMIT_BYTES = 56 * 1024 * 1024

BF16 = jnp.bfloat16
F32 = jnp.float32


def _alibi_slopes(n):
    return [float(2.0 ** (-((i + 1) * 8.0 / n))) for i in range(n)]


def _params(*sem):
    return pltpu.CompilerParams(dimension_semantics=sem,
                                vmem_limit_bytes=VMEM_LIMIT_BYTES)


def _pack_kernel(w_ref, o_ref, *, transpose):
    w = w_ref[...]
    if transpose:
        w = w.T
    o_ref[...] = pltpu.bitcast(w.astype(BF16), jnp.uint32)


def _pack_bf16_rows(stacked, layer, transpose=False):
    _, n, c = stacked.shape
    blk = 1024
    assert n % blk == 0 and c % blk == 0
    if transpose:
        grid, out_map = (n // blk, c // blk), (lambda i, j: (j, i))
        out_shape = (c // 2, n)
    else:
        grid, out_map = (n // blk, c // blk), (lambda i, j: (i, j))
        out_shape = (n // 2, c)
    return pl.pallas_call(
        functools.partial(_pack_kernel, transpose=transpose),
        grid=grid,
        in_specs=[pl.BlockSpec((None, blk, blk), lambda i, j: (layer, i, j))],
        out_specs=pl.BlockSpec((blk // 2, blk), out_map),
        out_shape=jax.ShapeDtypeStruct(out_shape, jnp.uint32),
        compiler_params=_params("parallel", "parallel"),
        name="pack_bf16",
    )(stacked)


def _unpack_bf16(words):
    return pltpu.bitcast(words, BF16)


def _layer_norm(z, g, b):
    mu = jnp.mean(z, axis=-1, keepdims=True)
    zc = z - mu
    var = jnp.mean(zc * zc, axis=-1, keepdims=True)
    return zc * lax.rsqrt(var + LN_EPS) * g + b


def _half_rms(t, lo):
    sq = t * t
    s_lo = jnp.sum(jnp.where(lo, sq, 0.0), axis=-1, keepdims=True)
    s_hi = jnp.sum(jnp.where(lo, 0.0, sq), axis=-1, keepdims=True)
    return jnp.where(lo, lax.rsqrt(s_lo / HEAD_DIM + RMS_EPS),
                     lax.rsqrt(s_hi / HEAD_DIM + RMS_EPS))


def _ln_kernel(x_ref, g_ref, b_ref, o_ref):
    o_ref[...] = _layer_norm(x_ref[...], g_ref[...], b_ref[...])


def _ln_call(x, g, b, tm):
    T, D = x.shape
    return pl.pallas_call(
        _ln_kernel,
        grid=(T // tm,),
        in_specs=[pl.BlockSpec((tm, D), lambda i: (i, 0)),
                  pl.BlockSpec((1, D), lambda i: (0, 0)),
                  pl.BlockSpec((1, D), lambda i: (0, 0))],
        out_specs=pl.BlockSpec((tm, D), lambda i: (i, 0)),
        out_shape=jax.ShapeDtypeStruct((T, D), F32),
        compiler_params=_params("parallel"),
        name="ln_in",
    )(x, g, b)


def _proj_kernel(h_ref, w_ref, qg_ref, kg_ref, cos_ref, sa_ref, sb_ref,
                 qa_ref, ka_ref, va_ref, qb_ref, kb_ref, vb_ref):
    tm = h_ref.shape[0]
    y = jnp.dot(h_ref[...].astype(BF16), w_ref[...], preferred_element_type=F32)
    lo = lax.broadcasted_iota(jnp.int32, (tm, PAIR_W), 1) < HEAD_DIM
    cos, sa, sb = cos_ref[...], sa_ref[...], sb_ref[...]
    scale = HEAD_DIM ** -0.5

    def norm_rope(t, g):
        xn = t * _half_rms(t, lo) * g
        return xn * cos + pltpu.roll(xn, PAIR_W - ROPE_FREQS, 1) * sa + pltpu.roll(xn, ROPE_FREQS, 1) * sb

    for i in range(Q_W // PAIR_W):
        t = y[:, i * PAIR_W:(i + 1) * PAIR_W]
        qa_ref[:, i * PAIR_W:(i + 1) * PAIR_W] = (norm_rope(t, qg_ref[...]) * scale).astype(BF16)
    off = Q_W
    ka_ref[...] = norm_rope(y[:, off:off + KV_W], kg_ref[...]).astype(BF16)
    off += KV_W
    va_ref[...] = y[:, off:off + KV_W].astype(BF16)
    off += KV_W
    qb_ref[...] = (y[:, off:off + Q_W] * scale).astype(BF16)
    off += Q_W
    kb_ref[...] = y[:, off:off + KV_W].astype(BF16)
    off += KV_W
    vb_ref[...] = y[:, off:off + KV_W].astype(BF16)


def _proj_call(h, w, qg, kg, cos, sa, sb, S, tm):
    T, D = h.shape
    n_pos = S // tm
    row = lambda i: (i, 0)
    const = lambda i: (0, 0)
    pos = lambda i: (i % n_pos, 0)
    outs = [jax.ShapeDtypeStruct((T, w_), BF16) for w_ in (Q_W, KV_W, KV_W, Q_W, KV_W, KV_W)]
    return pl.pallas_call(
        _proj_kernel,
        grid=(T // tm,),
        in_specs=[pl.BlockSpec((tm, D), row),
                  pl.BlockSpec((D, IN_WIDTH), const),
                  pl.BlockSpec((1, PAIR_W), const),
                  pl.BlockSpec((1, PAIR_W), const),
                  pl.BlockSpec((tm, PAIR_W), pos),
                  pl.BlockSpec((tm, PAIR_W), pos),
                  pl.BlockSpec((tm, PAIR_W), pos)],
        out_specs=[pl.BlockSpec((tm, o.shape[1]), row) for o in outs],
        out_shape=outs,
        compiler_params=_params("parallel"),
        name="in_proj",
    )(h, w, qg, kg, cos, sa, sb)


def _attn_a_kernel(q_ref, k_ref, v_ref, g_ref, o_ref):
    tq = q_ref.shape[0]
    k = k_ref[...]
    v = v_ref[...]
    lo = lax.broadcasted_iota(jnp.int32, (tq, PAIR_W), 1) < HEAD_DIM
    nt = (((1,), (1,)), ((), ()))
    for i in range(Q_W // PAIR_W):
        qp = q_ref[:, i * PAIR_W:(i + 1) * PAIR_W]
        halves = []
        for grp in range(2):
            qz = jnp.where(lo if grp == 0 else jnp.logical_not(lo), qp, jnp.zeros_like(qp))
            s = lax.dot_general(qz, k, nt, preferred_element_type=F32)
            m = jnp.max(s, axis=-1, keepdims=True)
            p = jnp.exp(s - m)
            l = jnp.sum(p, axis=-1, keepdims=True)
            halves.append(jnp.dot(p.astype(BF16), v, preferred_element_type=F32) / l)
        o = jnp.where(lo, halves[0], halves[1])
        o = o * _half_rms(o, lo) * g_ref[:, i * PAIR_W:(i + 1) * PAIR_W]
        o_ref[:, i * PAIR_W:(i + 1) * PAIR_W] = o.astype(BF16)


def _attn_a_call(q, k, v, g, B, S, tq):
    T = q.shape[0]
    nq = S // tq
    return pl.pallas_call(
        _attn_a_kernel,
        grid=(B, nq),
        in_specs=[pl.BlockSpec((tq, Q_W), lambda b, i: (b * nq + i, 0)),
                  pl.BlockSpec((S, KV_W), lambda b, i: (b, 0)),
                  pl.BlockSpec((S, KV_W), lambda b, i: (b, 0)),
                  pl.BlockSpec((1, Q_W), lambda b, i: (0, 0))],
        out_specs=pl.BlockSpec((tq, Q_W), lambda b, i: (b * nq + i, 0)),
        out_shape=jax.ShapeDtypeStruct((T, Q_W), BF16),
        compiler_params=_params("parallel", "parallel"),
        name="attn_global",
    )(q, k, v, g)


def _attn_b_kernel(sink_ref, q_ref, k_ref, v_ref, g_ref, o_ref, *, slopes):
    n_sub = q_ref.shape[0] // BLOCK
    for u in range(n_sub):
        _attn_b_block(sink_ref, q_ref, k_ref, v_ref, g_ref, o_ref, slopes,
                      pl.program_id(1) * n_sub + u, slice(u * BLOCK, (u + 1) * BLOCK))


def _attn_b_block(sink_ref, q_ref, k_ref, v_ref, g_ref, o_ref, slopes, j, rows):
    nb = k_ref.shape[0] // BLOCK
    span = BLOCK + 2 * WINDOW
    ks, vs = [], []
    for c in (-1, 0, 1):
        jc = j + c
        start = pl.multiple_of(jnp.clip(jc, 0, nb - 1) * BLOCK, BLOCK)
        ks.append(k_ref[pl.ds(start, BLOCK), :])
        vs.append(v_ref[pl.ds(start, BLOCK), :])
    kslab = jnp.concatenate(ks, axis=0)
    vslab = jnp.concatenate(vs, axis=0)
    qi = lax.broadcasted_iota(jnp.int32, (BLOCK, span), 0)
    ki = lax.broadcasted_iota(jnp.int32, (BLOCK, span), 1)
    dist = jnp.abs(qi + WINDOW - ki)
    s_pos = (j - 1) * BLOCK + ki
    in_seq = jnp.logical_and(s_pos >= 0, s_pos < nb * BLOCK)
    valid = jnp.logical_and(dist <= WINDOW, in_seq)
    distf = dist.astype(F32)

    lo = lax.broadcasted_iota(jnp.int32, (BLOCK, PAIR_W), 1) < HEAD_DIM
    hi = jnp.logical_not(lo)
    n_pairs = Q_W // PAIR_W
    heads = [(i, grp) for grp in range(2) for i in range(n_pairs)]
    qs = []
    for i, grp in heads:
        qp = q_ref[rows, i * PAIR_W:(i + 1) * PAIR_W]
        qs.append(jnp.where(lo if grp == 0 else hi, qp, jnp.zeros_like(qp)))
    s_all = lax.dot_general(jnp.concatenate(qs, axis=0), kslab, (((1,), (1,)), ((), ())),
                            preferred_element_type=F32)
    ps, inv = [], []
    for n, (i, grp) in enumerate(heads):
        hd = i + n_pairs * grp
        s = s_all[n * BLOCK:(n + 1) * BLOCK] - slopes[hd] * distf
        s = jnp.where(valid, s, NEG_INF)
        sink = sink_ref[hd]
        m = jnp.maximum(jnp.max(s, axis=-1, keepdims=True), sink)
        p = jnp.exp(s - m)
        inv.append(1.0 / (jnp.sum(p, axis=-1, keepdims=True) + jnp.exp(sink - m)))
        ps.append(p.astype(BF16))
    o_all = jnp.dot(jnp.concatenate(ps, axis=0), vslab, preferred_element_type=F32)
    for i in range(n_pairs):
        o0 = o_all[i * BLOCK:(i + 1) * BLOCK] * inv[i]
        o1 = o_all[(i + n_pairs) * BLOCK:(i + n_pairs + 1) * BLOCK] * inv[i + n_pairs]
        o = jnp.where(lo, o0, o1)
        o = o * _half_rms(o, lo) * g_ref[:, i * PAIR_W:(i + 1) * PAIR_W]
        o_ref[rows, i * PAIR_W:(i + 1) * PAIR_W] = o.astype(BF16)


def _attn_b_call(sink, q, k, v, g, B, S, n_sub):
    T = q.shape[0]
    nb = S // (BLOCK * n_sub)
    tqb = BLOCK * n_sub
    kern = functools.partial(_attn_b_kernel, slopes=_alibi_slopes(N_B_HEADS))
    return pl.pallas_call(
        kern,
        grid=(B, nb),
        in_specs=[pl.BlockSpec(memory_space=pltpu.SMEM),
                  pl.BlockSpec((tqb, Q_W), lambda b, i: (b * nb + i, 0)),
                  pl.BlockSpec((S, KV_W), lambda b, i: (b, 0)),
                  pl.BlockSpec((S, KV_W), lambda b, i: (b, 0)),
                  pl.BlockSpec((1, Q_W), lambda b, i: (0, 0))],
        out_specs=pl.BlockSpec((tqb, Q_W), lambda b, i: (b * nb + i, 0)),
        out_shape=jax.ShapeDtypeStruct((T, Q_W), BF16),
        compiler_params=_params("parallel", "parallel"),
        name="attn_window",
    )(sink, q, k, v, g)


def _oproj_kernel(oa_ref, ob_ref, wa_ref, wb_ref, h_ref, g_ref, b_ref, wq_ref, keys_ref,
                  o_ref, st_ref, *, alpha):
    mix = jnp.dot(oa_ref[...], wa_ref[...], preferred_element_type=F32)
    mix = mix + jnp.dot(ob_ref[...], wb_ref[...], preferred_element_type=F32)
    h1 = _layer_norm(alpha * h_ref[...] + mix, g_ref[...], b_ref[...])
    o_ref[...] = h1
    q = jnp.dot(h1.astype(BF16), wq_ref[...], preferred_element_type=F32).astype(BF16)
    half = PEER_DKEY // 2
    for n in range(2 * PEER_HEADS):
        st_ref[n] = lax.dot_general(keys_ref[n], q[:, n * half:(n + 1) * half],
                                    (((1,), (1,)), ((), ())), preferred_element_type=F32)


def _oproj_call(oa, ob, wa, wb, h, g, b, wq, keys, alpha, tm):
    T, D = h.shape
    n_sets = 2 * PEER_HEADS
    half = PEER_DKEY // 2
    row = lambda i: (i, 0)
    const = lambda i: (0, 0)
    return pl.pallas_call(
        functools.partial(_oproj_kernel, alpha=alpha),
        grid=(T // tm,),
        in_specs=[pl.BlockSpec((tm, Q_W), row), pl.BlockSpec((tm, Q_W), row),
                  pl.BlockSpec((Q_W, D), const), pl.BlockSpec((Q_W, D), const),
                  pl.BlockSpec((tm, D), row),
                  pl.BlockSpec((1, D), const), pl.BlockSpec((1, D), const),
                  pl.BlockSpec((D, PEER_HEADS * PEER_DKEY), const),
                  pl.BlockSpec((n_sets, PEER_NKEYS, half), lambda i: (0, 0, 0))],
        out_specs=[pl.BlockSpec((tm, D), row),
                   pl.BlockSpec((n_sets, PEER_NKEYS, tm), lambda i: (0, 0, i))],
        out_shape=[jax.ShapeDtypeStruct((T, D), F32),
                   jax.ShapeDtypeStruct((n_sets, PEER_NKEYS, T), F32)],
        compiler_params=_params("parallel"),
        name="out_proj_ln_scores",
    )(oa, ob, wa, wb, h, g, b, wq, keys)


ROUTE_TOKENS = SUBLANES * LANES
ROUTE_PASS = 4
ROUTE_PAIRS = [(r1, r2) for r1 in range(PEER_TOPK) for r2 in range(PEER_TOPK)
               if (r1 + 1) * (r2 + 1) <= PEER_TOPK]


def _slab(i):
    if isinstance(i, int):
        return pl.ds(i * SUBLANES, SUBLANES)
    return pl.ds(pl.multiple_of(i * SUBLANES, SUBLANES), SUBLANES)


def _insert_sorted(tops, x):
    out = []
    for t in tops[:-1]:
        out.append(jnp.maximum(t, x))
        x = jnp.minimum(t, x)
    out.append(jnp.maximum(tops[-1], x))
    return out


def _router_kernel(st_ref, r1_ref, c1_ref, n2_ref, e2_ref,
                   s1_ref, s2_ref, a1_ref, a2_ref, rk1_ref, n2k_ref, z_ref, bad_ref,
                   cand_ref, n2r_ref, out_ref):
    K = PEER_TOPK
    neg = -jnp.inf
    shape = (SUBLANES, LANES)
    ones = jnp.ones(shape, F32)
    zeros = jnp.zeros(shape, F32)

    def load_keys_in(src_set, dst_ref):
        for g in range(SUBLANES):
            dst_ref[pl.ds(g, PEER_NKEYS, stride=SUBLANES), :] = st_ref[src_set, :, g * LANES:(g + 1) * LANES]

    def store_keys_out(dst_ref, h):
        for g in range(SUBLANES):
            dst_ref[h, :, g * LANES:(g + 1) * LANES] = out_ref[pl.ds(g, PEER_NKEYS, stride=SUBLANES), :]

    def top_values(value_of, n, a_ref):
        thr = None
        for p in range(K // ROUTE_PASS):
            groups = [[jnp.full(shape, neg, F32)] * ROUTE_PASS for _ in range(2)]
            for k in range(n):
                x = value_of(k)
                if thr is not None:
                    x = jnp.where(x < thr, x, neg)
                groups[k % 2] = _insert_sorted(groups[k % 2], x)
            tops = groups[0]
            for x in groups[1]:
                tops = _insert_sorted(tops, x)
            for i, t in enumerate(tops):
                a_ref[_slab(p * ROUTE_PASS + i), :] = t
            thr = tops[-1]

    def adjacent_equal(a_ref):
        bad = zeros
        prev = a_ref[_slab(0), :]
        for r in range(1, K):
            cur = a_ref[_slab(r), :]
            bad = jnp.where(cur == prev, ones, bad)
            prev = cur
        return bad

    def cand(r1, r2):
        return a1_ref[_slab(r1), :] + a2_ref[_slab(r2), :]

    def finish_candidates(selected):
        top = cand(0, 0)
        z = zeros
        counts = [zeros] * K
        for idx, (r1, r2) in enumerate(ROUTE_PAIRS):
            s = selected(idx)
            z = z + s * jnp.exp(cand(r1, r2) - top)
            counts[r2] = counts[r2] + s
        z_ref[...] = z
        for r2 in range(K):
            n2r_ref[_slab(r2), :] = counts[r2]

    def head(h, carry):
        load_keys_in(2 * h, s1_ref)
        load_keys_in(2 * h + 1, s2_ref)

        top_values(lambda k: s1_ref[_slab(k), :], PEER_NKEYS, a1_ref)
        top_values(lambda k: s2_ref[_slab(k), :], PEER_NKEYS, a2_ref)
        a1 = [a1_ref[_slab(r), :] for r in range(K)]
        n_ge1 = zeros
        for k in range(PEER_NKEYS):
            s = s1_ref[_slab(k), :]
            rank = jnp.full(shape, float(K), F32)
            for r in reversed(range(K)):
                rank = jnp.where(s >= a1[r], float(r), rank)
            rk1_ref[_slab(k), :] = rank
            n_ge1 = n_ge1 + jnp.where(rank < float(K), ones, zeros)
        top_values(lambda i: cand(*ROUTE_PAIRS[i]), len(ROUTE_PAIRS), cand_ref)
        tau = cand_ref[_slab(K - 1), :]
        finish_candidates(lambda i: jnp.where(cand(*ROUTE_PAIRS[i]) >= tau, ones, zeros))
        a2 = [a2_ref[_slab(r), :] for r in range(K)]
        counts = [n2r_ref[_slab(r), :] for r in range(K)]
        n_sel = zeros
        for r in range(K):
            n_sel = n_sel + counts[r]
        n_ge2 = zeros
        for k in range(PEER_NKEYS):
            s = s2_ref[_slab(k), :]
            v = zeros
            for r in reversed(range(K)):
                v = jnp.where(s >= a2[r], counts[r], v)
            n2k_ref[_slab(k), :] = v
            n_ge2 = n_ge2 + jnp.where(s >= a2[K - 1], ones, zeros)
        bad = adjacent_equal(a1_ref) + adjacent_equal(a2_ref) + adjacent_equal(cand_ref)
        bad = bad + jnp.where(n_ge1 == float(K), zeros, ones) + jnp.where(n_ge2 == float(K), zeros, ones)
        bad = bad + jnp.where(n_sel == float(K), zeros, ones)

        bad_ref[...] = bad

        top1 = a1_ref[_slab(0), :]
        top2 = a2_ref[_slab(0), :]
        inv_z = 1.0 / z_ref[...]
        for k in range(PEER_NKEYS):
            out_ref[_slab(k), :] = rk1_ref[_slab(k), :]
        store_keys_out(r1_ref, h)
        for k in range(PEER_NKEYS):
            out_ref[_slab(k), :] = jnp.exp(s1_ref[_slab(k), :] - top1) * inv_z
        store_keys_out(c1_ref, h)
        for k in range(PEER_NKEYS):
            out_ref[_slab(k), :] = n2k_ref[_slab(k), :]
        store_keys_out(n2_ref, h)
        for k in range(PEER_NKEYS):
            out_ref[_slab(k), :] = jnp.exp(s2_ref[_slab(k), :] - top2)
        store_keys_out(e2_ref, h)

        def fix_group(g, carry2):
            @pl.when(jnp.sum(bad_ref[pl.ds(g, 1), :]) > 0.0)
            def _():
                _route_group_exact(st_ref, r1_ref, c1_ref, n2_ref, e2_ref, h,
                                   pl.ds(pl.multiple_of(g * LANES, LANES), LANES))
            return carry2

        @pl.when(jnp.sum(bad) > 0.0)
        def _():
            lax.fori_loop(0, SUBLANES, fix_group, 0)

        return carry

    lax.fori_loop(0, PEER_HEADS, head, 0)


def _route_group_exact(st_ref, r1_ref, c1_ref, n2_ref, e2_ref, h, lanes):
    tt = LANES
    K = PEER_TOPK
    n_lead = 4
    neg = -jnp.inf
    kio_i = lax.broadcasted_iota(jnp.int32, (PEER_NKEYS, tt), 0)
    rio = lax.broadcasted_iota(jnp.int32, (K, tt), 0)
    blk = kio_i // K
    rr = kio_i % K
    pos = jnp.where(blk < n_lead, rr * K + blk, (blk - n_lead) * K + rr).astype(F32)
    dup = jnp.logical_and(blk >= n_lead, rr < n_lead)
    kio = kio_i.astype(F32)

    def top16(s):
        rank = jnp.full((PEER_NKEYS, tt), float(K), F32)
        arr = jnp.zeros((K, tt), F32)
        for r in range(K):
            m = jnp.max(s, axis=0, keepdims=True)
            first = jnp.min(jnp.where(s == m, kio, float(PEER_NKEYS)), axis=0, keepdims=True)
            hit = kio == first
            rank = jnp.where(hit, float(r), rank)
            s = jnp.where(hit, neg, s)
            arr = jnp.where(rio == r, m, arr)
        return rank, arr

    s1 = st_ref[2 * h, :, lanes]
    s2 = st_ref[2 * h + 1, :, lanes]
    rank1, a1 = top16(s1)
    rank2, a2 = top16(s2)
    blocks = ([a1 + a2[r2:r2 + 1] for r2 in range(n_lead)]
              + [a1[r1:r1 + 1] + a2 for r1 in range(n_lead)])
    c0 = jnp.where(dup, neg, jnp.concatenate(blocks, axis=0))
    c = c0
    sel = jnp.zeros((PEER_NKEYS, tt), F32)
    for _ in range(K):
        m = jnp.max(c, axis=0, keepdims=True)
        first = jnp.min(jnp.where(c == m, pos, float(K * K)), axis=0, keepdims=True)
        hit = pos == first
        sel = jnp.where(hit, 1.0, sel)
        c = jnp.where(hit, neg, c)
    top = a1[0:1] + a2[0:1]
    z = jnp.sum(jnp.where(sel > 0.0, jnp.exp(c0 - top), 0.0), axis=0, keepdims=True)
    tail = sel[n_lead * K:(n_lead + 1) * K]
    for r1 in range(1, n_lead):
        tail = tail + sel[(n_lead + r1) * K:(n_lead + r1 + 1) * K]
    n2 = jnp.zeros((PEER_NKEYS, tt), F32)
    for r2 in range(K):
        if r2 < n_lead:
            cnt = jnp.sum(sel[r2 * K:(r2 + 1) * K], axis=0, keepdims=True)
        else:
            cnt = tail[r2:r2 + 1]
        n2 = jnp.where(rank2 == float(r2), cnt, n2)
    r1_ref[h, :, lanes] = rank1
    c1_ref[h, :, lanes] = jnp.exp(s1 - a1[0:1]) / z
    n2_ref[h, :, lanes] = n2
    e2_ref[h, :, lanes] = jnp.exp(s2 - a2[0:1])


def _router_call(st):
    T = st.shape[-1]
    tt = ROUTE_TOKENS
    shape = (PEER_HEADS, PEER_NKEYS, T)
    outs = [jax.ShapeDtypeStruct(shape, F32)] * 4
    spec = pl.BlockSpec((PEER_HEADS, PEER_NKEYS, tt), lambda i: (0, 0, i))
    keys = pltpu.VMEM((PEER_NKEYS * SUBLANES, LANES), F32)
    tops = pltpu.VMEM((PEER_TOPK * SUBLANES, LANES), F32)
    one = pltpu.VMEM((SUBLANES, LANES), F32)
    return pl.pallas_call(
        _router_kernel,
        grid=(T // tt,),
        in_specs=[pl.BlockSpec((2 * PEER_HEADS, PEER_NKEYS, tt), lambda i: (0, 0, i))],
        out_specs=[spec] * 4,
        out_shape=outs,
        scratch_shapes=[keys, keys, tops, tops, keys, keys, one, one, tops, tops, keys],
        compiler_params=_params("parallel"),
        name="peer_router",
    )(st)


PEER_CHUNK_ROWS = 256
PEER_IB_GROUP = 2
PEER_PART_ROWS = 64


def _peer_dense_kernel(h_ref, r1_ref, c1_ref, n2_ref, e2_ref, u_first_ref, u_a_ref, u_b_ref,
                       vt_a_ref, vt_b_ref, vt_last_ref, g_ref, b_ref, o_ref,
                       xt_ref, acc_ref, n2s_ref, e2s_ref, act_a_ref, act_b_ref, gt_a_ref, gt_b_ref,
                       *, alpha):
    j = pl.program_id(1)
    tt = h_ref.shape[0]
    te = vt_a_ref.shape[1]
    n_sub = te // PEER_NKEYS
    tile = (PEER_PART_ROWS, LANES)
    sqrt_half = float(np.sqrt(0.5))

    def gelu_bf16(a):
        return (a * (0.5 + 0.5 * lax.erf(a * sqrt_half))).astype(BF16)

    @pl.when(j == 0)
    def _():
        xt_ref[:, :tt] = h_ref[...].T.astype(BF16)
        acc_ref[:, :tt] = jnp.zeros((acc_ref.shape[0], tt), F32)
        n2s_ref[:, :, :tt] = n2_ref[...].astype(BF16)
        e2s_ref[:, :, :tt] = e2_ref[...].astype(BF16)
        gt_b_ref[:, :tt] = jnp.zeros((te, tt), BF16)
        act_a_ref[:, :tt] = gelu_bf16(jnp.dot(_unpack_bf16(u_first_ref[...]), xt_ref[:, :tt],
                                              preferred_element_type=F32))

    def phase(u_next_ref, act_next_ref, act_cur_ref, gt_cur_ref, vt_prev_ref, gt_prev_ref, key_off):
        def chunk(c, carry):
            rows = pl.ds(pl.multiple_of(c * PEER_CHUNK_ROWS, PEER_CHUNK_ROWS), PEER_CHUNK_ROWS)
            prows = pl.ds(pl.multiple_of(c * (PEER_CHUNK_ROWS // 2), PEER_CHUNK_ROWS // 2),
                          PEER_CHUNK_ROWS // 2)
            n_ib = PEER_CHUNK_ROWS // PEER_NKEYS
            for grp in range(n_ib // PEER_IB_GROUP):
                ibs = [c * n_ib + grp * PEER_IB_GROUP + s for s in range(PEER_IB_GROUP)]
                r1_rows = [[r1_ref[hd, pl.ds(key_off + ib, 1), :] for hd in range(PEER_HEADS)]
                           for ib in ibs]
                c1_rows = [[c1_ref[hd, pl.ds(key_off + ib, 1), :] for hd in range(PEER_HEADS)]
                           for ib in ibs]
                for tg in range(tt // LANES):
                    lanes = slice(tg * LANES, (tg + 1) * LANES)
                    n_part = PEER_NKEYS // PEER_PART_ROWS
                    ws = [[None] * PEER_IB_GROUP for _ in range(n_part)]
                    for hd in range(PEER_HEADS):
                        r1 = [jnp.broadcast_to(r1_rows[s][hd][:, lanes], tile).astype(BF16)
                              for s in range(PEER_IB_GROUP)]
                        c1 = [jnp.broadcast_to(c1_rows[s][hd][:, lanes], tile).astype(BF16)
                              for s in range(PEER_IB_GROUP)]
                        for part in range(n_part):
                            krows = slice(part * PEER_PART_ROWS, (part + 1) * PEER_PART_ROWS)
                            n2t = n2s_ref[hd, krows, lanes]
                            e2t = e2s_ref[hd, krows, lanes]
                            for s in range(PEER_IB_GROUP):
                                term = jnp.where(r1[s] < n2t, e2t, jnp.zeros(tile, BF16)) * c1[s]
                                ws[part][s] = term if ws[part][s] is None else ws[part][s] + term
                    for part in range(n_part):
                        for s in range(PEER_IB_GROUP):
                            erows = pl.ds(pl.multiple_of(ibs[s] * PEER_NKEYS + part * PEER_PART_ROWS,
                                                         PEER_PART_ROWS), PEER_PART_ROWS)
                            gt_cur_ref[erows, lanes] = act_cur_ref[erows, lanes] * ws[part][s]
            act_next_ref[rows, :tt] = gelu_bf16(jnp.dot(_unpack_bf16(u_next_ref[prows, :]),
                                                        xt_ref[:, :tt], preferred_element_type=F32))
            acc_ref[rows, :tt] += jnp.dot(_unpack_bf16(vt_prev_ref[prows, :]), gt_prev_ref[:, :tt],
                                        preferred_element_type=F32)
            return carry

        lax.fori_loop(0, te // PEER_CHUNK_ROWS, chunk, 0)

    phase(u_a_ref, act_b_ref, act_a_ref, gt_a_ref, vt_a_ref, gt_b_ref, 0)
    phase(u_b_ref, act_a_ref, act_b_ref, gt_b_ref, vt_b_ref, gt_a_ref, n_sub)

    @pl.when(j == pl.num_programs(1) - 1)
    def _():
        acc = acc_ref[:, :tt] + jnp.dot(_unpack_bf16(vt_last_ref[...]), gt_b_ref[:, :tt],
                                     preferred_element_type=F32)
        z = alpha * h_ref[...] + acc.T
        o_ref[...] = _layer_norm(z, g_ref[...], b_ref[...])


def _peer_dense_call(h, r1, c1, n2, e2, u, vt, g, b, alpha, tt, te):
    T, D = h.shape
    n_sub = te // PEER_NKEYS
    n_blk = PEER_EXPERTS // te
    n_steps = n_blk // 2
    assert D == te and n_blk % 2 == 0
    spec1 = pl.BlockSpec((PEER_HEADS, 2 * n_sub, tt), lambda i, j: (0, j, i))
    spec2 = pl.BlockSpec((PEER_HEADS, PEER_NKEYS, tt), lambda i, j: (0, 0, i))
    last = n_blk - 1
    ttp = tt + LANES
    u_spec = lambda f: pl.BlockSpec((te // 2, D), lambda i, j: (f(j), 0))
    vt_spec = lambda f: pl.BlockSpec((D // 2, te), lambda i, j: (0, f(j)))
    return pl.pallas_call(
        functools.partial(_peer_dense_kernel, alpha=alpha),
        grid=(T // tt, n_steps),
        in_specs=[pl.BlockSpec((tt, D), lambda i, j: (i, 0)),
                  spec1, spec1, spec2, spec2,
                  u_spec(lambda j: 0),
                  u_spec(lambda j: 2 * j + 1),
                  u_spec(lambda j: jnp.minimum(2 * j + 2, last)),
                  vt_spec(lambda j: jnp.maximum(2 * j - 1, 0)),
                  vt_spec(lambda j: 2 * j),
                  vt_spec(lambda j: last),
                  pl.BlockSpec((1, D), lambda i, j: (0, 0)),
                  pl.BlockSpec((1, D), lambda i, j: (0, 0))],
        out_specs=pl.BlockSpec((tt, D), lambda i, j: (i, 0)),
        out_shape=jax.ShapeDtypeStruct((T, D), F32),
        scratch_shapes=[pltpu.VMEM((D, ttp), BF16), pltpu.VMEM((D, ttp), F32),
                        pltpu.VMEM((PEER_HEADS, PEER_NKEYS, ttp), BF16),
                        pltpu.VMEM((PEER_HEADS, PEER_NKEYS, ttp), BF16),
                        pltpu.VMEM((te, ttp), BF16), pltpu.VMEM((te, ttp), BF16),
                        pltpu.VMEM((te, ttp), BF16), pltpu.VMEM((te, ttp), BF16)],
        compiler_params=_params("parallel", "arbitrary"),
        name="peer_dense",
    )(h, r1, c1, n2, e2, u, u, u, vt, vt, vt, g, b)


def _pair_perm():
    idx = []
    for i in range(N_A_HEADS // 2):
        for grp in range(2):
            hd = i + (N_A_HEADS // 2) * grp
            idx.extend(range(hd * HEAD_DIM, (hd + 1) * HEAD_DIM))
    return np.asarray(idx, dtype=np.int32)


def _rope_tables(S):
    pos = np.arange(S)
    row = (pos // GRID_W).astype(np.float32)
    col = (pos % GRID_W).astype(np.float32)
    inv_freq = jnp.asarray(ROPE_THETA, F32) ** (-jnp.arange(ROPE_FREQS, dtype=F32) / ROPE_FREQS)
    ang_r = jnp.asarray(row)[:, None] * inv_freq
    ang_c = jnp.asarray(col)[:, None] * inv_freq
    cr, sr, cc, sc = jnp.cos(ang_r), jnp.sin(ang_r), jnp.cos(ang_c), jnp.sin(ang_c)
    zero = jnp.zeros_like(sr)
    cos = jnp.concatenate([cr, cr, cc, cc], axis=-1)
    sa = jnp.concatenate([-sr, zero, -sc, zero], axis=-1)
    sb = jnp.concatenate([zero, sr, zero, sc], axis=-1)
    tile2 = lambda t: jnp.concatenate([t, t], axis=-1)
    return tile2(cos), tile2(sa), tile2(sb)


def kernel(x, ln_in_g, ln_in_b, w_in, qn_g, kn_g, sink, gn_a_g, gn_b_g, w_o, ln1_g, ln1_b,
           peer_wq, peer_keys, peer_u, peer_v, ln2_g, ln2_b):
    B, S, D = x.shape
    depth = w_in.shape[0]
    T = B * S
    alpha = float((2.0 * depth) ** 0.25)
    tm = min(1024, S)
    tq = min(512, S)
    n_win = min(4, S // BLOCK)
    tt_dense = min(512, T)
    te = 1024

    perm = _pair_perm()
    col_perm = np.concatenate([perm, Q_W + np.arange(2 * KV_W, dtype=np.int32),
                               Q_W + 2 * KV_W + perm,
                               2 * Q_W + 2 * KV_W + np.arange(2 * KV_W, dtype=np.int32)])
    cos, sa, sb = _rope_tables(S)
    row2 = lambda v: v.reshape(1, -1)
    tile2 = lambda v: jnp.concatenate([v, v]).reshape(1, -1)

    h = _ln_call(x.reshape(T, D), row2(ln_in_g), row2(ln_in_b), tm)
    for l in range(depth):
        w = w_in[l][:, col_perm].astype(BF16)
        qa, ka, va, qb, kb, vb = _proj_call(h, w, tile2(qn_g[l]), tile2(kn_g[l]), cos, sa, sb, S, tm)
        oa = _attn_a_call(qa, ka, va, row2(gn_a_g[l][perm]), B, S, tq)
        ob = _attn_b_call(sink[l], qb, kb, vb, row2(gn_b_g[l][perm]), B, S, n_win)
        wa = w_o[l][:Q_W][perm].astype(BF16)
        wb = w_o[l][Q_W:][perm].astype(BF16)
        keys = peer_keys[l].reshape(2 * PEER_HEADS, PEER_NKEYS, PEER_DKEY // 2).astype(BF16)
        h, st = _oproj_call(oa, ob, wa, wb, h, row2(ln1_g[l]), row2(ln1_b[l]),
                            peer_wq[l].astype(BF16), keys, alpha, min(512, S))
        r1, c1, n2, e2 = _router_call(st)
        h = _peer_dense_call(h, r1, c1, n2, e2, _pack_bf16_rows(peer_u, l),
                             _pack_bf16_rows(peer_v, l, transpose=True),
                             row2(ln2_g[l]), row2(ln2_b[l]), alpha, tt_dense, te)
    return h.reshape(B, S, D)
```
